```python
import math
import jax, jax.numpy as jnp
from jax import lax
import numpy as np

D_MODEL = 1024
BATCH = 4
SEQ = 8192
DEPTH = 2

CHUNK = 64
N_MIXERS = 2
N_S5_LAYERS = (DEPTH + N_MIXERS - 1) // N_MIXERS
N_GM_LAYERS = DEPTH // N_MIXERS
S5_WIDTH = D_MODEL
S5_GROUP_CH = 16
S5_GROUPS = S5_WIDTH // S5_GROUP_CH
S5_STATE = 64
S5_DT_MIN = 1e-3
S5_DT_MAX = 1e-1
GM_HALF = 2 * D_MODEL
GM_GROUPS = 8
GM_SPAN = 128
MEM_LEN = 256
XA_HEADS = 4
XA_HEAD_DIM = D_MODEL // XA_HEADS
MOE_GROUPS = 4
MOE_PER_GROUP = 8
MOE_EXPERTS = MOE_GROUPS * MOE_PER_GROUP
MOE_TOPK = 2
MOE_HIDDEN = D_MODEL // 2
MOE_BLOCK = 128
RMS_EPS = 1e-6

kernel_name = "hybrid_s5_gmlp_memxattn_hmoe"


def rms_norm(x, g):
    xf = x.astype(jnp.float32)
    y = xf * lax.rsqrt(jnp.mean(xf * xf, axis=-1, keepdims=True) + RMS_EPS)
    return (y * g.astype(jnp.float32)).astype(x.dtype)


def _complex_linear_combine(e1, e2):
    a1r, a1i, b1r, b1i = e1
    a2r, a2i, b2r, b2i = e2
    return (a2r * a1r - a2i * a1i,
            a2r * a1i + a2i * a1r,
            a2r * b1r - a2i * b1i + b2r,
            a2r * b1i + a2i * b1r + b2i)


def s5_mixer(h, w_in, lam_re, lam_im, log_dt, b_re, b_im, c_re, c_im, d_skip, w_out):
    f32 = jnp.float32
    bsz, seq, _ = h.shape
    n_chunks = seq // CHUNK
    u = (h @ w_in).astype(f32)
    lr = lam_re.astype(f32)
    li = lam_im.astype(f32)
    dt = jnp.exp(log_dt.astype(f32))[:, None]
    mag = jnp.exp(lr * dt)
    ab_re = mag * jnp.cos(li * dt)
    ab_im = mag * jnp.sin(li * dt)
    den = lr * lr + li * li
    coef_re = ((ab_re - 1.0) * lr + ab_im * li) / den
    coef_im = (ab_im * lr - (ab_re - 1.0) * li) / den
    br = b_re.astype(f32)
    bi = b_im.astype(f32)
    bb_re = coef_re[..., None] * br - coef_im[..., None] * bi
    bb_im = coef_re[..., None] * bi + coef_im[..., None] * br
    cr = c_re.astype(f32)
    ci = c_im.astype(f32)
    uc = u.reshape(bsz, n_chunks, CHUNK, S5_GROUPS, S5_GROUP_CH).transpose(1, 0, 2, 3, 4)

    def chunk_step(carry, u_blk):
        h_re, h_im = carry
        bu_re = jnp.einsum('btgn,gpn->btgp', u_blk, bb_re)
        bu_im = jnp.einsum('btgn,gpn->btgp', u_blk, bb_im)
        a_re = jnp.broadcast_to(ab_re, bu_re.shape)
        a_im = jnp.broadcast_to(ab_im, bu_re.shape)
        pa_re, pa_im, s_re, s_im = lax.associative_scan(
            _complex_linear_combine, (a_re, a_im, bu_re, bu_im), axis=1)
        x_re = s_re + pa_re * h_re[:, None] - pa_im * h_im[:, None]
        x_im = s_im + pa_re * h_im[:, None] + pa_im * h_re[:, None]
        y = (jnp.einsum('btgp,gnp->btgn', x_re, cr)
             - jnp.einsum('btgp,gnp->btgn', x_im, ci))
        return (x_re[:, -1], x_im[:, -1]), y

    init = (jnp.zeros((bsz, S5_GROUPS, S5_STATE), f32),
            jnp.zeros((bsz, S5_GROUPS, S5_STATE), f32))
    _, ys = lax.scan(chunk_step, init, uc)
    y = ys.transpose(1, 0, 2, 3, 4).reshape(bsz, seq, S5_WIDTH) + d_skip.astype(f32) * u
    z = jax.nn.gelu(y).astype(h.dtype) @ w_out
    return z[..., :D_MODEL] * jax.nn.sigmoid(z[..., D_MODEL:])


def gmlp_mixer(h, w_in, v_norm, w_s, b_s, w_out):
    bsz, seq, _ = h.shape
    z = jax.nn.gelu(h @ w_in)
    u = z[..., :GM_HALF]
    v = rms_norm(z[..., GM_HALF:], v_norm)
    pos_chunk = jnp.arange(GM_SPAN) // CHUNK
    mask = pos_chunk[:, None] >= pos_chunk[None, :]
    ws = jnp.where(mask[None], w_s, jnp.zeros_like(w_s))
    vb = v.reshape(bsz, seq // GM_SPAN, GM_SPAN, GM_GROUPS, GM_HALF // GM_GROUPS)
    mixed = jnp.einsum('gij,bnjgc->bnigc', ws, vb) + b_s.T[:, :, None]
    return (u * mixed.reshape(bsz, seq, GM_HALF)) @ w_out


def memory_cross_attention(h, mem_n, w_q, w_kv, w_o):
    bsz, seq, _ = h.shape
    m = mem_n.shape[1]
    q = (h @ w_q).reshape(bsz, seq, XA_HEADS, XA_HEAD_DIM)
    kv = mem_n @ w_kv
    k = kv[..., :D_MODEL].reshape(bsz, m, XA_HEADS, XA_HEAD_DIM)
    v = kv[..., D_MODEL:].reshape(bsz, m, XA_HEADS, XA_HEAD_DIM)
    s = jnp.einsum('bshd,bmhd->bhsm', q, k).astype(jnp.float32) * (XA_HEAD_DIM ** -0.5)
    p = jax.nn.softmax(s, axis=-1).astype(h.dtype)
    o = jnp.einsum('bhsm,bmhd->bshd', p, v).reshape(bsz, seq, D_MODEL)
    return o @ w_o


def hierarchical_moe(h, w_group, b_group, w_expert, b_expert, w_gate, w_up, w_down):
    f32 = jnp.float32
    bsz, seq, d = h.shape
    n_tok = bsz * seq
    ht = h.reshape(n_tok, d)
    p_group = jax.nn.softmax((ht @ w_group).astype(f32) + b_group.astype(f32), axis=-1)
    g = jnp.argmax(p_group, axis=-1)
    w_g = jnp.take_along_axis(p_group, g[:, None], axis=1)[:, 0]
    e_logits = ((ht @ w_expert).astype(f32) + b_expert.astype(f32)).reshape(
        n_tok, MOE_GROUPS, MOE_PER_GROUP)
    within = jnp.take_along_axis(e_logits, g[:, None, None], axis=1)[:, 0, :]
    vals, idx = lax.top_k(within, MOE_TOPK)
    w_e = jax.nn.softmax(vals, axis=-1)
    eid = (g[:, None] * MOE_PER_GROUP + idx).reshape(-1).astype(jnp.int32)
    wts = (w_g[:, None] * w_e).reshape(-1)
    tok = jnp.repeat(jnp.arange(n_tok, dtype=jnp.int32), MOE_TOPK)
    n_assign = n_tok * MOE_TOPK
    n_pad = n_assign + MOE_EXPERTS * MOE_BLOCK
    n_blocks = n_pad // MOE_BLOCK
    order = jnp.argsort(eid)
    e_sorted = eid[order]
    counts = jnp.bincount(eid, length=MOE_EXPERTS).astype(jnp.int32)
    padded = ((counts + MOE_BLOCK - 1) // MOE_BLOCK) * MOE_BLOCK
    start = jnp.cumsum(counts) - counts
    pad_end = jnp.cumsum(padded)
    pad_start = pad_end - padded
    dest = pad_start[e_sorted] + jnp.arange(n_assign, dtype=jnp.int32) - start[e_sorted]
    buf_tok = jnp.full((n_pad,), n_tok, jnp.int32).at[dest].set(tok[order])
    buf_w = jnp.zeros((n_pad,), f32).at[dest].set(wts[order])
    block_expert = jnp.minimum(
        jnp.searchsorted(pad_end, jnp.arange(n_blocks, dtype=jnp.int32) * MOE_BLOCK, side='right'),
        MOE_EXPERTS - 1).astype(jnp.int32)
    h_ext = jnp.concatenate([ht, jnp.zeros((1, d), ht.dtype)], axis=0)
    xb = h_ext[buf_tok].reshape(n_blocks, MOE_BLOCK, d)

    def expert_block(args):
        xblk, e = args
        a = xblk @ w_gate[e]
        b = xblk @ w_up[e]
        return (jax.nn.silu(a) * b) @ w_down[e]

    yb = lax.map(expert_block, (xb, block_expert)).reshape(n_pad, d)
    out = jnp.zeros((n_tok + 1, d), ht.dtype).at[buf_tok].add(yb * buf_w[:, None].astype(ht.dtype))
    return out[:n_tok].reshape(bsz, seq, d)


def setup_inputs(seed: int = 0) -> dict:
    key = jax.random.key(seed)
    ks = iter(jax.random.split(key, 48))
    f32 = jnp.float32

    def nrm(shape, scale):
        return jax.random.normal(next(ks), shape, f32) * scale

    def gain(shape):
        return 1.0 + nrm(shape, 0.01)

    NA, NB, L, D = N_S5_LAYERS, N_GM_LAYERS, DEPTH, D_MODEL
    lam_im = jnp.broadcast_to(jnp.pi * jnp.arange(S5_STATE, dtype=f32), (NA, S5_GROUPS, S5_STATE))
    return {
        "x": nrm((BATCH, SEQ, D), 1.0),
        "mem": nrm((BATCH, MEM_LEN, D), 1.0),
        "s5_norm": gain((NA, D)),
        "s5_w_in": nrm((NA, D, S5_WIDTH), D ** -0.5),
        "s5_lambda_re": -0.5 + nrm((NA, S5_GROUPS, S5_STATE), 0.01),
        "s5_lambda_im": lam_im + nrm((NA, S5_GROUPS, S5_STATE), 0.01),
        "s5_log_dt": jax.random.uniform(next(ks), (NA, S5_GROUPS), f32,
                                        minval=math.log(S5_DT_MIN), maxval=math.log(S5_DT_MAX)),
        "s5_b_re": nrm((NA, S5_GROUPS, S5_STATE, S5_GROUP_CH), (2 * S5_GROUP_CH) ** -0.5),
        "s5_b_im": nrm((NA, S5_GROUPS, S5_STATE, S5_GROUP_CH), (2 * S5_GROUP_CH) ** -0.5),
        "s5_c_re": nrm((NA, S5_GROUPS, S5_GROUP_CH, S5_STATE), (2 * S5_STATE) ** -0.5),
        "s5_c_im": nrm((NA, S5_GROUPS, S5_GROUP_CH, S5_STATE), (2 * S5_STATE) ** -0.5),
        "s5_d": nrm((NA, S5_WIDTH), 1.0),
        "s5_w_out": nrm((NA, S5_WIDTH, 2 * D), S5_WIDTH ** -0.5),
        "gm_norm": gain((NB, D)),
        "gm_w_in": nrm((NB, D, 2 * GM_HALF), D ** -0.5),
        "gm_v_norm": gain((NB, GM_HALF)),
        "gm_w_s": nrm((NB, GM_GROUPS, GM_SPAN, GM_SPAN), GM_SPAN ** -0.5),
        "gm_b_s": gain((NB, GM_GROUPS, GM_SPAN)),
        "gm_w_out": nrm((NB, GM_HALF, D), GM_HALF ** -0.5),
        "mem_norm": gain((D,)),
        "xa_norm": gain((L, D)),
        "xa_w_q": nrm((L, D, D), D ** -0.5),
        "xa_w_kv": nrm((L, D, 2 * D), D ** -0.5),
        "xa_w_o": nrm((L, D, D), D ** -0.5),
        "moe_norm": gain((L, D)),
        "moe_w_group": nrm((L, D, MOE_GROUPS), D ** -0.5),
        "moe_b_group": nrm((L, MOE_GROUPS), 0.01),
        "moe_w_expert": nrm((L, D, MOE_EXPERTS), D ** -0.5),
        "moe_b_expert": nrm((L, MOE_EXPERTS), 0.01),
        "moe_w_gate": nrm((L, MOE_EXPERTS, D, MOE_HIDDEN), D ** -0.5),
        "moe_w_up": nrm((L, MOE_EXPERTS, D, MOE_HIDDEN), D ** -0.5),
        "moe_w_down": nrm((L, MOE_EXPERTS, MOE_HIDDEN, D), MOE_HIDDEN ** -0.5),
        "final_norm": gain((D,)),
    }


def reference(x, mem, s5_norm, s5_w_in, s5_lambda_re, s5_lambda_im, s5_log_dt, s5_b_re, s5_b_im,
              s5_c_re, s5_c_im, s5_d, s5_w_out, gm_norm, gm_w_in, gm_v_norm, gm_w_s, gm_b_s,
              gm_w_out, mem_norm, xa_norm, xa_w_q, xa_w_kv, xa_w_o, moe_norm, moe_w_group,
              moe_b_group, moe_w_expert, moe_b_expert, moe_w_gate, moe_w_up, moe_w_down,
              final_norm):
    mem_n = rms_norm(mem, mem_norm)
    for i in range(DEPTH):
        j = i // N_MIXERS
        if i % N_MIXERS == 0:
            x = x + s5_mixer(rms_norm(x, s5_norm[j]), s5_w_in[j], s5_lambda_re[j], s5_lambda_im[j],
                             s5_log_dt[j], s5_b_re[j], s5_b_im[j], s5_c_re[j], s5_c_im[j],
                             s5_d[j], s5_w_out[j])
        else:
            x = x + gmlp_mixer(rms_norm(x, gm_norm[j]), gm_w_in[j], gm_v_norm[j], gm_w_s[j],
                               gm_b_s[j], gm_w_out[j])
        x = x + memory_cross_attention(rms_norm(x, xa_norm[i]), mem_n, xa_w_q[i], xa_w_kv[i], xa_w_o[i])
        x = x + hierarchical_moe(rms_norm(x, moe_norm[i]), moe_w_group[i], moe_b_group[i],
                                 moe_w_expert[i], moe_b_expert[i], moe_w_gate[i], moe_w_up[i],
                                 moe_w_down[i])
    return rms_norm(x, final_norm)
```

```python
import functools
import math

import jax
import jax.numpy as jnp
from jax import lax
from jax.experimental import pallas as pl
from jax.experimental.pallas import tpu as pltpu

F32 = jnp.float32
BF16 = jnp.bfloat16

D_MODEL = 1024
DEPTH = 2
CHUNK = 64
S5_GROUP_CH = 16
S5_GROUPS = 64
S5_STATE = 64
GM_HALF = 2 * D_MODEL
GM_GROUPS = 8
GM_SPAN = 128
GM_GROUP_CH = GM_HALF // GM_GROUPS
XA_HEADS = 4
XA_HEAD_DIM = D_MODEL // XA_HEADS
MOE_GROUPS = 4
MOE_PER_GROUP = 8
MOE_EXPERTS = MOE_GROUPS * MOE_PER_GROUP
MOE_HIDDEN = D_MODEL // 2
RMS_EPS = 1e-6

LANES = 128
SUBLANES = 8
VMEM_LIMIT = 56 * 1024 * 1024

S5_CT = 256
S5_NCT = D_MODEL // S5_CT
S5_STATES_CT = (S5_CT // S5_GROUP_CH) * S5_STATE
S5_SLABS = S5_STATES_CT // LANES
S5_TC = 128
S5_PITCH = S5_TC + SUBLANES

MOE_BLK = 256
ROUTER_TM = 512
COMBINE_TM = 256


def _rms(x, g):
    ms = jnp.mean(x * x, axis=-1, keepdims=True)
    return x * lax.rsqrt(ms + RMS_EPS) * g


def _const_spec(shape):
    nd = len(shape)
    return pl.BlockSpec(shape, lambda *_: (0,) * nd, pipeline_mode=pl.Buffered(1))


def _s5_kernel(x_ref, g_ref, win_ref, bblk_ref, cblk_ref, are_ref, aim_ref, d_ref,
               wout_ref, o_ref, hre_ref, him_ref, sre_ref, sim_ref):
    nb = x_ref.shape[0]
    tc = x_ref.shape[1]

    @pl.when(pl.program_id(0) == 0)
    def _():
        hre_ref[...] = jnp.zeros_like(hre_ref)
        him_ref[...] = jnp.zeros_like(him_ref)

    x = x_ref[...].reshape(nb * tc, D_MODEL)
    xn = _rms(x, g_ref[...]).astype(BF16)
    u = jnp.dot(xn, win_ref[...], preferred_element_type=F32)

    y_parts = []
    for ct in range(S5_NCT):
        u_ct = u[:, ct * S5_CT:(ct + 1) * S5_CT]
        bu = jnp.dot(u_ct.astype(BF16), bblk_ref[ct], preferred_element_type=F32)
        for b in range(nb):
            for j in range(S5_SLABS):
                rows = slice(b * tc, (b + 1) * tc)
                sre_ref[b, pl.ds(j * S5_PITCH, tc), :] = bu[rows, j * LANES:(j + 1) * LANES]
                sim_ref[b, pl.ds(j * S5_PITCH, tc), :] = bu[
                    rows, S5_STATES_CT + j * LANES:S5_STATES_CT + (j + 1) * LANES]
        a_re = are_ref[ct]
        a_im = aim_ref[ct]

        def step(t, carry):
            new = []
            for b in range(nb):
                hr, hi = carry[2 * b], carry[2 * b + 1]
                idx = pl.ds(t, S5_SLABS, stride=S5_PITCH)
                nr = a_re * hr - a_im * hi + sre_ref[b, idx, :]
                ni = a_re * hi + a_im * hr + sim_ref[b, idx, :]
                sre_ref[b, idx, :] = nr
                sim_ref[b, idx, :] = ni
                new += [nr, ni]
            return tuple(new)

        init = []
        for b in range(nb):
            init += [hre_ref[ct, b], him_ref[ct, b]]
        fin = lax.fori_loop(0, tc, step, tuple(init), unroll=2)
        for b in range(nb):
            hre_ref[ct, b] = fin[2 * b]
            him_ref[ct, b] = fin[2 * b + 1]

        xs = []
        for b in range(nb):
            cols = [sre_ref[b, pl.ds(j * S5_PITCH, tc), :] for j in range(S5_SLABS)]
            cols += [sim_ref[b, pl.ds(j * S5_PITCH, tc), :] for j in range(S5_SLABS)]
            xs.append(jnp.concatenate(cols, axis=1))
        xst = jnp.concatenate(xs, axis=0).astype(BF16)
        y_parts.append(jnp.dot(xst, cblk_ref[ct], preferred_element_type=F32))

    y = jnp.concatenate(y_parts, axis=1) + d_ref[...] * u
    yg = jax.nn.gelu(y).astype(BF16)
    z = jnp.dot(yg, wout_ref[...], preferred_element_type=F32)
    out = x + z[:, :D_MODEL] * (1.0 / (1.0 + jnp.exp(-z[:, D_MODEL:])))
    o_ref[...] = out.reshape(nb, tc, D_MODEL)


def _s5_discretize(lam_re, lam_im, log_dt, b_re, b_im, c_re, c_im):
    lr = lam_re.astype(F32)
    li = lam_im.astype(F32)
    dt = jnp.exp(log_dt.astype(F32))[:, None]
    mag = jnp.exp(lr * dt)
    ab_re = mag * jnp.cos(li * dt)
    ab_im = mag * jnp.sin(li * dt)
    den = lr * lr + li * li
    coef_re = ((ab_re - 1.0) * lr + ab_im * li) / den
    coef_im = (ab_im * lr - (ab_re - 1.0) * li) / den
    br = b_re.astype(F32)
    bi = b_im.astype(F32)
    bb_re = coef_re[..., None] * br - coef_im[..., None] * bi
    bb_im = coef_re[..., None] * bi + coef_im[..., None] * br
    gpt = S5_CT // S5_GROUP_CH
    eye = jnp.eye(gpt, dtype=F32)

    def in_blocks(bb):
        t = bb.reshape(S5_NCT, gpt, S5_STATE, S5_GROUP_CH).transpose(0, 1, 3, 2)
        blk = t[:, :, :, None, :] * eye[None, :, None, :, None]
        return blk.reshape(S5_NCT, S5_CT, S5_STATES_CT)

    def out_blocks(c):
        t = c.reshape(S5_NCT, gpt, S5_GROUP_CH, S5_STATE).transpose(0, 1, 3, 2)
        blk = t[:, :, :, None, :] * eye[None, :, None, :, None]
        return blk.reshape(S5_NCT, S5_STATES_CT, S5_CT)

    bblk = jnp.concatenate([in_blocks(bb_re), in_blocks(bb_im)], axis=2).astype(BF16)
    cblk = jnp.concatenate([out_blocks(c_re.astype(F32)),
                            out_blocks(-c_im.astype(F32))], axis=1).astype(BF16)
    a_re = ab_re.reshape(S5_NCT, S5_SLABS, LANES)
    a_im = ab_im.reshape(S5_NCT, S5_SLABS, LANES)
    return bblk, cblk, a_re, a_im


def _s5_layer(x, norm_g, w_in, lam_re, lam_im, log_dt, b_re, b_im, c_re, c_im, d_skip, w_out):
    bsz, seq, d = x.shape
    bblk, cblk, a_re, a_im = _s5_discretize(lam_re, lam_im, log_dt, b_re, b_im, c_re, c_im)
    xspec = pl.BlockSpec((bsz, S5_TC, d), lambda k: (0, k, 0))
    return pl.pallas_call(
        _s5_kernel,
        grid=(seq // S5_TC,),
        in_specs=[
            xspec,
            _const_spec((1, d)),
            _const_spec((d, d)),
            _const_spec(bblk.shape),
            _const_spec(cblk.shape),
            _const_spec(a_re.shape),
            _const_spec(a_im.shape),
            _const_spec((1, d)),
            _const_spec((d, 2 * d)),
        ],
        out_specs=xspec,
        out_shape=jax.ShapeDtypeStruct(x.shape, F32),
        scratch_shapes=[
            pltpu.VMEM((S5_NCT, bsz, S5_SLABS, LANES), F32),
            pltpu.VMEM((S5_NCT, bsz, S5_SLABS, LANES), F32),
            pltpu.VMEM((bsz, S5_SLABS * S5_PITCH, LANES), F32),
            pltpu.VMEM((bsz, S5_SLABS * S5_PITCH, LANES), F32),
        ],
        compiler_params=pltpu.CompilerParams(
            dimension_semantics=("arbitrary",), vmem_limit_bytes=VMEM_LIMIT),
        name="s5_layer",
    )(x, norm_g.reshape(1, d), w_in.astype(BF16), bblk, cblk, a_re, a_im,
      d_skip.reshape(1, d).astype(F32), w_out.astype(BF16))


def _gmlp_kernel(x_ref, g_ref, win_ref, vn_ref, ws_ref, bst_ref, wout_ref, o_ref):
    tm = x_ref.shape[0]
    x = x_ref[...]
    xn = _rms(x, g_ref[...]).astype(BF16)
    u = jax.nn.gelu(jnp.dot(xn, win_ref[:, :GM_HALF], preferred_element_type=F32))
    v = jax.nn.gelu(jnp.dot(xn, win_ref[:, GM_HALF:], preferred_element_type=F32))
    vb = _rms(v, vn_ref[...]).astype(BF16)
    row = lax.broadcasted_iota(jnp.int32, (GM_SPAN, GM_SPAN), 0) // CHUNK
    col = lax.broadcasted_iota(jnp.int32, (GM_SPAN, GM_SPAN), 1) // CHUNK
    causal = row >= col
    ws = [jnp.where(causal, ws_ref[g], 0.0).astype(BF16) for g in range(GM_GROUPS)]
    spans = []
    for s in range(tm // GM_SPAN):
        parts = []
        for g in range(GM_GROUPS):
            vblk = vb[s * GM_SPAN:(s + 1) * GM_SPAN, g * GM_GROUP_CH:(g + 1) * GM_GROUP_CH]
            parts.append(jnp.dot(ws[g], vblk, preferred_element_type=F32)
                         + bst_ref[:, g:g + 1])
        spans.append(jnp.concatenate(parts, axis=1))
    mixed = jnp.concatenate(spans, axis=0)
    p = (u * mixed).astype(BF16)
    o_ref[...] = x + jnp.dot(p, wout_ref[...], preferred_element_type=F32)


def _gmlp_layer(x, norm_g, w_in, v_norm, w_s, b_s, w_out, tm=256):
    bsz, seq, d = x.shape
    n_tok = bsz * seq
    xt = x.reshape(n_tok, d)
    xspec = pl.BlockSpec((tm, d), lambda i: (i, 0))
    out = pl.pallas_call(
        _gmlp_kernel,
        grid=(n_tok // tm,),
        in_specs=[
            xspec,
            _const_spec((1, d)),
            _const_spec((d, 2 * GM_HALF)),
            _const_spec((1, GM_HALF)),
            _const_spec((GM_GROUPS, GM_SPAN, GM_SPAN)),
            _const_spec((GM_SPAN, GM_GROUPS)),
            _const_spec((GM_HALF, d)),
        ],
        out_specs=xspec,
        out_shape=jax.ShapeDtypeStruct((n_tok, d), F32),
        compiler_params=pltpu.CompilerParams(
            dimension_semantics=("parallel",), vmem_limit_bytes=VMEM_LIMIT),
        name="gmlp_layer",
    )(xt, norm_g.reshape(1, d), w_in.astype(BF16), v_norm.reshape(1, GM_HALF),
      w_s.astype(F32), b_s.T.astype(F32), w_out.astype(BF16))
    return out.reshape(bsz, seq, d)


def _norm_proj_kernel(x_ref, g_ref, w_ref, o_ref):
    xn = _rms(x_ref[...], g_ref[...]).astype(BF16)
    o_ref[...] = jnp.dot(xn, w_ref[...], preferred_element_type=F32).astype(o_ref.dtype)


def _norm_proj(x, g, w, out_dtype, tm=512, tn=1024):
    m, d = x.shape
    n = w.shape[1]
    return pl.pallas_call(
        _norm_proj_kernel,
        grid=(m // tm, n // tn),
        in_specs=[
            pl.BlockSpec((tm, d), lambda i, j: (i, 0)),
            pl.BlockSpec((1, d), lambda i, j: (0, 0)),
            pl.BlockSpec((d, tn), lambda i, j: (0, j)),
        ],
        out_specs=pl.BlockSpec((tm, tn), lambda i, j: (i, j)),
        out_shape=jax.ShapeDtypeStruct((m, n), out_dtype),
        compiler_params=pltpu.CompilerParams(
            dimension_semantics=("parallel", "parallel"), vmem_limit_bytes=VMEM_LIMIT),
        name="norm_proj",
    )(x, g.reshape(1, d), w.astype(BF16))


def _xattn_kernel(x_ref, g_ref, wq_ref, kt_ref, v_ref, wo_ref, o_ref):
    x = x_ref[0]
    xn = _rms(x, g_ref[...]).astype(BF16)
    q = jnp.dot(xn, wq_ref[...], preferred_element_type=F32) * (XA_HEAD_DIM ** -0.5)
    q = q.astype(BF16)
    heads = []
    for h in range(XA_HEADS):
        cols = slice(h * XA_HEAD_DIM, (h + 1) * XA_HEAD_DIM)
        s = jnp.dot(q[:, cols], kt_ref[0, cols, :], preferred_element_type=F32)
        e = jnp.exp(s - jnp.max(s, axis=-1, keepdims=True))
        p = (e / jnp.sum(e, axis=-1, keepdims=True)).astype(BF16)
        heads.append(jnp.dot(p, v_ref[0, :, cols], preferred_element_type=F32))
    o = jnp.concatenate(heads, axis=1).astype(BF16)
    o_ref[0] = x + jnp.dot(o, wo_ref[...], preferred_element_type=F32)


def _xattn_layer(x, mem, mem_g, norm_g, w_q, w_kv, w_o, tm=512):
    bsz, seq, d = x.shape
    m = mem.shape[1]
    kv = _norm_proj(mem.reshape(bsz * m, d), mem_g, w_kv, BF16).reshape(bsz, m, 2 * d)
    kt = kv[..., :d].transpose(0, 2, 1)
    v = kv[..., d:]
    xspec = pl.BlockSpec((1, tm, d), lambda b, i: (b, i, 0))
    return pl.pallas_call(
        _xattn_kernel,
        grid=(bsz, seq // tm),
        in_specs=[
            xspec,
            _const_spec((1, d)),
            _const_spec((d, d)),
            pl.BlockSpec((1, d, m), lambda b, i: (b, 0, 0)),
            pl.BlockSpec((1, m, d), lambda b, i: (b, 0, 0)),
            _const_spec((d, d)),
        ],
        out_specs=xspec,
        out_shape=jax.ShapeDtypeStruct(x.shape, F32),
        compiler_params=pltpu.CompilerParams(
            dimension_semantics=("parallel", "parallel"), vmem_limit_bytes=VMEM_LIMIT),
        name="xattn_layer",
    )(x, norm_g.reshape(1, d), w_q.astype(BF16), kt, v, w_o.astype(BF16))


def _router_kernel(x_ref, g_ref, whi_ref, wlo_ref, bias_ref, ids_ref, wts_ref, cnt_ref,
                   run_ref):
    tm = x_ref.shape[0]

    @pl.when(pl.program_id(0) == 0)
    def _():
        run_ref[...] = jnp.zeros_like(run_ref)

    xn = _rms(x_ref[...], g_ref[...])
    xhi = xn.astype(BF16)
    xlo = (xn - xhi.astype(F32)).astype(BF16)
    logits = (jnp.dot(xhi, whi_ref[...], preferred_element_type=F32)
              + jnp.dot(xhi, wlo_ref[...], preferred_element_type=F32)
              + jnp.dot(xlo, whi_ref[...], preferred_element_type=F32)) + bias_ref[...]
    lane = lax.broadcasted_iota(jnp.int32, (tm, LANES), 1)
    neg = jnp.float32(-jnp.inf)

    def first_argmax(vals):
        mx = jnp.max(vals, axis=-1, keepdims=True)
        idx = jnp.min(jnp.where(vals == mx, lane, LANES), axis=-1, keepdims=True)
        return mx, idx

    gl = jnp.where(lane < MOE_GROUPS, logits, neg)
    gmax, gidx = first_argmax(gl)
    w_g = 1.0 / jnp.sum(jnp.exp(gl - gmax), axis=-1, keepdims=True)
    lo = MOE_GROUPS + MOE_PER_GROUP * gidx
    el = jnp.where((lane >= lo) & (lane < lo + MOE_PER_GROUP), logits, neg)
    m1, i1 = first_argmax(el)
    m2, i2 = first_argmax(jnp.where(lane == i1, neg, el))
    e21 = jnp.exp(m2 - m1)
    w1 = w_g / (1.0 + e21)
    w2 = w_g * e21 / (1.0 + e21)
    e1 = i1 - MOE_GROUPS
    e2 = i2 - MOE_GROUPS

    onehot = ((lane == e1) | (lane == e2)).astype(BF16)
    r = lax.broadcasted_iota(jnp.int32, (tm, tm), 0)
    c = lax.broadcasted_iota(jnp.int32, (tm, tm), 1)
    before = (c < r).astype(BF16)
    tot = run_ref[...] + jnp.dot(before, onehot, preferred_element_type=F32)
    r1 = jnp.sum(jnp.where(lane == e1, tot, 0.0), axis=-1, keepdims=True).astype(jnp.int32)
    r2 = jnp.sum(jnp.where(lane == e2, tot, 0.0), axis=-1, keepdims=True).astype(jnp.int32)
    new_run = run_ref[...] + jnp.sum(onehot.astype(F32), axis=0, keepdims=True)
    run_ref[...] = new_run
    cnt_ref[...] = new_run

    ids_ref[...] = jnp.where(lane == 0, e1, jnp.where(lane == 1, e2,
                             jnp.where(lane == 2, r1, jnp.where(lane == 3, r2, 0))))
    wts_ref[...] = jnp.where(lane == 0, w1, jnp.where(lane == 1, w2, 0.0))


def _router(xt, norm_g, w_group, b_group, w_expert, b_expert):
    n_tok, d = xt.shape
    nr = MOE_GROUPS + MOE_EXPERTS
    w_r = jnp.zeros((d, LANES), F32).at[:, :MOE_GROUPS].set(w_group.astype(F32))
    w_r = w_r.at[:, MOE_GROUPS:nr].set(w_expert.astype(F32))
    bias = jnp.zeros((1, LANES), F32).at[0, :MOE_GROUPS].set(b_group.astype(F32))
    bias = bias.at[0, MOE_GROUPS:nr].set(b_expert.astype(F32))
    w_hi = w_r.astype(BF16)
    w_lo = (w_r - w_hi.astype(F32)).astype(BF16)
    tm = ROUTER_TM
    tok_spec = pl.BlockSpec((tm, LANES), lambda i: (i, 0))
    return pl.pallas_call(
        _router_kernel,
        grid=(n_tok // tm,),
        in_specs=[
            pl.BlockSpec((tm, d), lambda i: (i, 0)),
            _const_spec((1, d)),
            _const_spec((d, LANES)),
            _const_spec((d, LANES)),
            _const_spec((1, LANES)),
        ],
        out_specs=[tok_spec, tok_spec, pl.BlockSpec((1, LANES), lambda i: (0, 0))],
        out_shape=[
            jax.ShapeDtypeStruct((n_tok, LANES), jnp.int32),
            jax.ShapeDtypeStruct((n_tok, LANES), F32),
            jax.ShapeDtypeStruct((1, LANES), F32),
        ],
        scratch_shapes=[pltpu.VMEM((1, LANES), F32)],
        compiler_params=pltpu.CompilerParams(
            dimension_semantics=("arbitrary",), vmem_limit_bytes=VMEM_LIMIT),
        name="moe_router",
    )(xt, norm_g.reshape(1, d), w_hi, w_lo, bias)


def _row_gather_start(idx_ref, n_rows, src_hbm, buf, sem, slot):
    def body(r, carry):
        row = idx_ref[0, 0, r]
        pltpu.make_async_copy(src_hbm.at[pl.ds(row, 1), :],
                              buf.at[slot, pl.ds(r, 1), :], sem.at[slot]).start()
        return carry
    lax.fori_loop(0, n_rows, body, 0, unroll=8)


def _row_gather_wait(n_rows, src_hbm, buf, sem, slot):
    pltpu.make_async_copy(src_hbm.at[pl.ds(0, n_rows), :], buf.at[slot], sem.at[slot]).wait()


def _expert_kernel(bexp_ref, nused_ref, idx_ref, idxn_ref, x_hbm, g_ref, wg_ref, wu_ref,
                   wd_ref, y_ref, buf, sem):
    b = pl.program_id(0)
    n_used = nused_ref[0]
    slot = b % 2

    @pl.when(b == 0)
    def _():
        _row_gather_start(idx_ref, MOE_BLK, x_hbm, buf, sem, 0)

    @pl.when(b + 1 < n_used)
    def _():
        _row_gather_start(idxn_ref, MOE_BLK, x_hbm, buf, sem, 1 - slot)

    @pl.when(b < n_used)
    def _():
        _row_gather_wait(MOE_BLK, x_hbm, buf, sem, slot)
        xb = _rms(buf[slot], g_ref[...]).astype(BF16)
        a = jnp.dot(xb, wg_ref[...], preferred_element_type=F32)
        up = jnp.dot(xb, wu_ref[...], preferred_element_type=F32)
        h = (a * (1.0 / (1.0 + jnp.exp(-a))) * up).astype(BF16)
        y_ref[...] = jnp.dot(h, wd_ref[...], preferred_element_type=F32)

    @pl.when(b >= n_used)
    def _():
        y_ref[...] = jnp.zeros_like(y_ref)


def _combine_kernel(pos_ref, posn_ref, x_ref, wts_ref, fg_ref, y_hbm, o_ref, buf, sem, *,
                    final):
    i = pl.program_id(0)
    n = pl.num_programs(0)
    tm = x_ref.shape[0]
    slot = i % 2

    @pl.when(i == 0)
    def _():
        _row_gather_start(pos_ref, 2 * tm, y_hbm, buf, sem, 0)

    @pl.when(i + 1 < n)
    def _():
        _row_gather_start(posn_ref, 2 * tm, y_hbm, buf, sem, 1 - slot)

    _row_gather_wait(2 * tm, y_hbm, buf, sem, slot)
    w = wts_ref[...]
    out = x_ref[...] + w[:, 0:1] * buf[slot, :tm, :] + w[:, 1:2] * buf[slot, tm:, :]
    if final:
        out = _rms(out, fg_ref[...])
    o_ref[...] = out


def _moe_layer(x, norm_g, w_group, b_group, w_expert, b_expert, w_gate, w_up, w_down,
               final_g=None):
    bsz, seq, d = x.shape
    n_tok = bsz * seq
    xt = x.reshape(n_tok, d)
    ids, wts, cnt = _router(xt, norm_g, w_group, b_group, w_expert, b_expert)

    counts = cnt[0, :MOE_EXPERTS].astype(jnp.int32)
    padded = ((counts + MOE_BLK - 1) // MOE_BLK) * MOE_BLK
    pad_end = jnp.cumsum(padded)
    pad_start = pad_end - padded
    n_pad = n_tok * 2 + MOE_EXPERTS * MOE_BLK
    n_blocks = n_pad // MOE_BLK
    pos1 = pad_start[ids[:, 0]] + ids[:, 2]
    pos2 = pad_start[ids[:, 1]] + ids[:, 3]
    tok = jnp.arange(n_tok, dtype=jnp.int32)
    buf_tok = jnp.zeros((n_pad,), jnp.int32).at[pos1].set(tok).at[pos2].set(tok)
    block_expert = jnp.minimum(
        jnp.searchsorted(pad_end, jnp.arange(n_blocks, dtype=jnp.int32) * MOE_BLK, side='right'),
        MOE_EXPERTS - 1).astype(jnp.int32)
    n_used = (pad_end[-1] // MOE_BLK).astype(jnp.int32).reshape(1)
    buf_tok3 = buf_tok.reshape(n_blocks, 1, MOE_BLK)

    smem_blk = functools.partial(pl.BlockSpec, memory_space=pltpu.SMEM)
    y = pl.pallas_call(
        _expert_kernel,
        grid_spec=pltpu.PrefetchScalarGridSpec(
            num_scalar_prefetch=2,
            grid=(n_blocks,),
            in_specs=[
                smem_blk((1, 1, MOE_BLK), lambda b, be, nu: (b, 0, 0)),
                smem_blk((1, 1, MOE_BLK),
                         lambda b, be, nu: (jnp.minimum(b + 1, n_blocks - 1), 0, 0)),
                pl.BlockSpec(memory_space=pl.ANY),
                pl.BlockSpec((1, d), lambda b, be, nu: (0, 0)),
                pl.BlockSpec((None, d, MOE_HIDDEN), lambda b, be, nu: (be[b], 0, 0)),
                pl.BlockSpec((None, d, MOE_HIDDEN), lambda b, be, nu: (be[b], 0, 0)),
                pl.BlockSpec((None, MOE_HIDDEN, d), lambda b, be, nu: (be[b], 0, 0)),
            ],
            out_specs=pl.BlockSpec((MOE_BLK, d), lambda b, be, nu: (b, 0)),
            scratch_shapes=[pltpu.VMEM((2, MOE_BLK, d), F32),
                            pltpu.SemaphoreType.DMA((2,))],
        ),
        out_shape=jax.ShapeDtypeStruct((n_pad, d), F32),
        compiler_params=pltpu.CompilerParams(
            dimension_semantics=("arbitrary",), vmem_limit_bytes=VMEM_LIMIT),
        name="moe_experts",
    )(block_expert, n_used, buf_tok3, buf_tok3, xt, norm_g.reshape(1, d),
      w_gate.astype(BF16), w_up.astype(BF16), w_down.astype(BF16))

    tm = COMBINE_TM
    n_tiles = n_tok // tm
    pos = jnp.concatenate([pos1.reshape(n_tiles, 1, tm), pos2.reshape(n_tiles, 1, tm)], axis=2)
    final = final_g is not None
    fg = (final_g if final else jnp.ones((d,), F32)).reshape(1, d)
    out = pl.pallas_call(
        functools.partial(_combine_kernel, final=final),
        grid=(n_tiles,),
        in_specs=[
            smem_blk((1, 1, 2 * tm), lambda i: (i, 0, 0)),
            smem_blk((1, 1, 2 * tm), lambda i: (jnp.minimum(i + 1, n_tiles - 1), 0, 0)),
            pl.BlockSpec((tm, d), lambda i: (i, 0)),
            pl.BlockSpec((tm, LANES), lambda i: (i, 0)),
            pl.BlockSpec((1, d), lambda i: (0, 0)),
            pl.BlockSpec(memory_space=pl.ANY),
        ],
        out_specs=pl.BlockSpec((tm, d), lambda i: (i, 0)),
        out_shape=jax.ShapeDtypeStruct((n_tok, d), F32),
        scratch_shapes=[pltpu.VMEM((2, 2 * tm, d), F32), pltpu.SemaphoreType.DMA((2,))],
        compiler_params=pltpu.CompilerParams(
            dimension_semantics=("arbitrary",), vmem_limit_bytes=VMEM_LIMIT),
        name="moe_combine",
    )(pos, pos, xt, wts, fg, y)
    return out.reshape(bsz, seq, d)


def kernel(x, mem, s5_norm, s5_w_in, s5_lambda_re, s5_lambda_im, s5_log_dt, s5_b_re, s5_b_im, s5_c_re, s5_c_im, s5_d, s5_w_out, gm_norm, gm_w_in, gm_v_norm, gm_w_s, gm_b_s, gm_w_out, mem_norm, xa_norm, xa_w_q, xa_w_kv, xa_w_o, moe_norm, moe_w_group, moe_b_group, moe_w_expert, moe_b_expert, moe_w_gate, moe_w_up, moe_w_down, final_norm):
    for i in range(DEPTH):
        j = i // 2
        if i % 2 == 0:
            x = _s5_layer(x, s5_norm[j], s5_w_in[j], s5_lambda_re[j], s5_lambda_im[j],
                          s5_log_dt[j], s5_b_re[j], s5_b_im[j], s5_c_re[j], s5_c_im[j],
                          s5_d[j], s5_w_out[j])
        else:
            x = _gmlp_layer(x, gm_norm[j], gm_w_in[j], gm_v_norm[j], gm_w_s[j], gm_b_s[j],
                            gm_w_out[j])
        x = _xattn_layer(x, mem, mem_norm, xa_norm[i], xa_w_q[i], xa_w_kv[i], xa_w_o[i])
        x = _moe_layer(x, moe_norm[i], moe_w_group[i], moe_b_group[i], moe_w_expert[i],
                       moe_b_expert[i], moe_w_gate[i], moe_w_up[i], moe_w_down[i],
                       final_g=final_norm if i == DEPTH - 1 else None)
    return x
```

```python
import functools

import jax
import jax.numpy as jnp
from jax import lax
from jax.experimental import pallas as pl
from jax.experimental.pallas import tpu as pltpu

F32 = jnp.float32
BF16 = jnp.bfloat16

D_MODEL = 1024
DEPTH = 2
CHUNK = 64
S5_GROUP_CH = 16
S5_GROUPS = 64
S5_STATE = 64
GM_HALF = 2 * D_MODEL
GM_GROUPS = 8
GM_SPAN = 128
GM_GROUP_CH = GM_HALF // GM_GROUPS
XA_HEADS = 4
XA_HEAD_DIM = D_MODEL // XA_HEADS
MOE_GROUPS = 4
MOE_PER_GROUP = 8
MOE_EXPERTS = MOE_GROUPS * MOE_PER_GROUP
MOE_HIDDEN = D_MODEL // 2
RMS_EPS = 1e-6

LANES = 128
SUBLANES = 8
VMEM_LIMIT = 56 * 1024 * 1024

S5_CT = 256
S5_NCT = D_MODEL // S5_CT
S5_STATES_CT = (S5_CT // S5_GROUP_CH) * S5_STATE
S5_SLABS = S5_STATES_CT // LANES
S5_TC = 128
S5_PITCH = S5_TC + SUBLANES

MOE_BLK = 256
DISPATCH_TM = 256
COMBINE_TM = 256


def _rms(x, g):
    ms = jnp.mean(x * x, axis=-1, keepdims=True)
    return x * lax.rsqrt(ms + RMS_EPS) * g


def _const_spec(shape):
    nd = len(shape)
    return pl.BlockSpec(shape, lambda *_: (0,) * nd, pipeline_mode=pl.Buffered(1))


def _s5_kernel(x_ref, g_ref, win_ref, bblk_ref, cblk_ref, are_ref, aim_ref, d_ref,
               wout_ref, o_ref, hre_ref, him_ref, sre_ref, sim_ref):
    nb = x_ref.shape[0]
    tc = x_ref.shape[1]

    @pl.when(pl.program_id(0) == 0)
    def _():
        hre_ref[...] = jnp.zeros_like(hre_ref)
        him_ref[...] = jnp.zeros_like(him_ref)

    x = x_ref[...].reshape(nb * tc, D_MODEL)
    xn = _rms(x, g_ref[...]).astype(BF16)
    u = jnp.dot(xn, win_ref[...], preferred_element_type=F32)

    y_parts = []
    for ct in range(S5_NCT):
        u_ct = u[:, ct * S5_CT:(ct + 1) * S5_CT]
        bu = jnp.dot(u_ct.astype(BF16), bblk_ref[ct], preferred_element_type=F32)
        for b in range(nb):
            for j in range(S5_SLABS):
                rows = slice(b * tc, (b + 1) * tc)
                sre_ref[b, pl.ds(j * S5_PITCH, tc), :] = bu[rows, j * LANES:(j + 1) * LANES]
                sim_ref[b, pl.ds(j * S5_PITCH, tc), :] = bu[
                    rows, S5_STATES_CT + j * LANES:S5_STATES_CT + (j + 1) * LANES]
        a_re = are_ref[ct]
        a_im = aim_ref[ct]

        def step(t, carry):
            new = []
            for b in range(nb):
                hr, hi = carry[2 * b], carry[2 * b + 1]
                idx = pl.ds(t, S5_SLABS, stride=S5_PITCH)
                nr = a_re * hr - a_im * hi + sre_ref[b, idx, :]
                ni = a_re * hi + a_im * hr + sim_ref[b, idx, :]
                sre_ref[b, idx, :] = nr
                sim_ref[b, idx, :] = ni
                new += [nr, ni]
            return tuple(new)

        init = []
        for b in range(nb):
            init += [hre_ref[ct, b], him_ref[ct, b]]
        fin = lax.fori_loop(0, tc, step, tuple(init), unroll=2)
        for b in range(nb):
            hre_ref[ct, b] = fin[2 * b]
            him_ref[ct, b] = fin[2 * b + 1]

        xs = []
        for b in range(nb):
            cols = [sre_ref[b, pl.ds(j * S5_PITCH, tc), :] for j in range(S5_SLABS)]
            cols += [sim_ref[b, pl.ds(j * S5_PITCH, tc), :] for j in range(S5_SLABS)]
            xs.append(jnp.concatenate(cols, axis=1))
        xst = jnp.concatenate(xs, axis=0).astype(BF16)
        y_parts.append(jnp.dot(xst, cblk_ref[ct], preferred_element_type=F32))

    y = jnp.concatenate(y_parts, axis=1) + d_ref[...] * u
    yg = jax.nn.gelu(y).astype(BF16)
    z = jnp.dot(yg, wout_ref[...], preferred_element_type=F32)
    out = x + z[:, :D_MODEL] * (1.0 / (1.0 + jnp.exp(-z[:, D_MODEL:])))
    o_ref[...] = out.reshape(nb, tc, D_MODEL)


def _s5_discretize(lam_re, lam_im, log_dt, b_re, b_im, c_re, c_im):
    lr = lam_re.astype(F32)
    li = lam_im.astype(F32)
    dt = jnp.exp(log_dt.astype(F32))[:, None]
    mag = jnp.exp(lr * dt)
    ab_re = mag * jnp.cos(li * dt)
    ab_im = mag * jnp.sin(li * dt)
    den = lr * lr + li * li
    coef_re = ((ab_re - 1.0) * lr + ab_im * li) / den
    coef_im = (ab_im * lr - (ab_re - 1.0) * li) / den
    br = b_re.astype(F32)
    bi = b_im.astype(F32)
    bb_re = coef_re[..., None] * br - coef_im[..., None] * bi
    bb_im = coef_re[..., None] * bi + coef_im[..., None] * br
    gpt = S5_CT // S5_GROUP_CH
    eye = jnp.eye(gpt, dtype=F32)

    def in_blocks(bb):
        t = bb.reshape(S5_NCT, gpt, S5_STATE, S5_GROUP_CH).transpose(0, 1, 3, 2)
        blk = t[:, :, :, None, :] * eye[None, :, None, :, None]
        return blk.reshape(S5_NCT, S5_CT, S5_STATES_CT)

    def out_blocks(c):
        t = c.reshape(S5_NCT, gpt, S5_GROUP_CH, S5_STATE).transpose(0, 1, 3, 2)
        blk = t[:, :, :, None, :] * eye[None, :, None, :, None]
        return blk.reshape(S5_NCT, S5_STATES_CT, S5_CT)

    bblk = jnp.concatenate([in_blocks(bb_re), in_blocks(bb_im)], axis=2).astype(BF16)
    cblk = jnp.concatenate([out_blocks(c_re.astype(F32)),
                            out_blocks(-c_im.astype(F32))], axis=1).astype(BF16)
    a_re = ab_re.reshape(S5_NCT, S5_SLABS, LANES)
    a_im = ab_im.reshape(S5_NCT, S5_SLABS, LANES)
    return bblk, cblk, a_re, a_im


def _s5_layer(x, norm_g, w_in, lam_re, lam_im, log_dt, b_re, b_im, c_re, c_im, d_skip, w_out):
    bsz, seq, d = x.shape
    bblk, cblk, a_re, a_im = _s5_discretize(lam_re, lam_im, log_dt, b_re, b_im, c_re, c_im)
    xspec = pl.BlockSpec((bsz, S5_TC, d), lambda k: (0, k, 0))
    return pl.pallas_call(
        _s5_kernel,
        grid=(seq // S5_TC,),
        in_specs=[
            xspec,
            _const_spec((1, d)),
            _const_spec((d, d)),
            _const_spec(bblk.shape),
            _const_spec(cblk.shape),
            _const_spec(a_re.shape),
            _const_spec(a_im.shape),
            _const_spec((1, d)),
            _const_spec((d, 2 * d)),
        ],
        out_specs=xspec,
        out_shape=jax.ShapeDtypeStruct(x.shape, F32),
        scratch_shapes=[
            pltpu.VMEM((S5_NCT, bsz, S5_SLABS, LANES), F32),
            pltpu.VMEM((S5_NCT, bsz, S5_SLABS, LANES), F32),
            pltpu.VMEM((bsz, S5_SLABS * S5_PITCH, LANES), F32),
            pltpu.VMEM((bsz, S5_SLABS * S5_PITCH, LANES), F32),
        ],
        compiler_params=pltpu.CompilerParams(
            dimension_semantics=("arbitrary",), vmem_limit_bytes=VMEM_LIMIT),
        name="s5_layer",
    )(x, norm_g.reshape(1, d), w_in.astype(BF16), bblk, cblk, a_re, a_im,
      d_skip.reshape(1, d).astype(F32), w_out.astype(BF16))


def _gmlp_kernel(x_ref, g_ref, win_ref, vn_ref, ws_ref, bst_ref, wout_ref, o_ref):
    tm = x_ref.shape[0]
    x = x_ref[...]
    xn = _rms(x, g_ref[...]).astype(BF16)
    u = jax.nn.gelu(jnp.dot(xn, win_ref[:, :GM_HALF], preferred_element_type=F32))
    v = jax.nn.gelu(jnp.dot(xn, win_ref[:, GM_HALF:], preferred_element_type=F32))
    vb = _rms(v, vn_ref[...]).astype(BF16)
    row = lax.broadcasted_iota(jnp.int32, (GM_SPAN, GM_SPAN), 0) // CHUNK
    col = lax.broadcasted_iota(jnp.int32, (GM_SPAN, GM_SPAN), 1) // CHUNK
    causal = row >= col
    ws = [jnp.where(causal, ws_ref[g], 0.0).astype(BF16) for g in range(GM_GROUPS)]
    spans = []
    for s in range(tm // GM_SPAN):
        parts = []
        for g in range(GM_GROUPS):
            vblk = vb[s * GM_SPAN:(s + 1) * GM_SPAN, g * GM_GROUP_CH:(g + 1) * GM_GROUP_CH]
            parts.append(jnp.dot(ws[g], vblk, preferred_element_type=F32)
                         + bst_ref[:, g:g + 1])
        spans.append(jnp.concatenate(parts, axis=1))
    mixed = jnp.concatenate(spans, axis=0)
    p = (u * mixed).astype(BF16)
    o_ref[...] = x + jnp.dot(p, wout_ref[...], preferred_element_type=F32)


def _gmlp_layer(x, norm_g, w_in, v_norm, w_s, b_s, w_out, tm=256):
    bsz, seq, d = x.shape
    n_tok = bsz * seq
    xt = x.reshape(n_tok, d)
    xspec = pl.BlockSpec((tm, d), lambda i: (i, 0))
    out = pl.pallas_call(
        _gmlp_kernel,
        grid=(n_tok // tm,),
        in_specs=[
            xspec,
            _const_spec((1, d)),
            _const_spec((d, 2 * GM_HALF)),
            _const_spec((1, GM_HALF)),
            _const_spec((GM_GROUPS, GM_SPAN, GM_SPAN)),
            _const_spec((GM_SPAN, GM_GROUPS)),
            _const_spec((GM_HALF, d)),
        ],
        out_specs=xspec,
        out_shape=jax.ShapeDtypeStruct((n_tok, d), F32),
        compiler_params=pltpu.CompilerParams(
            dimension_semantics=("parallel",), vmem_limit_bytes=VMEM_LIMIT),
        name="gmlp_layer",
    )(xt, norm_g.reshape(1, d), w_in.astype(BF16), v_norm.reshape(1, GM_HALF),
      w_s.astype(F32), b_s.T.astype(F32), w_out.astype(BF16))
    return out.reshape(bsz, seq, d)


def _norm_proj_kernel(x_ref, g_ref, w_ref, o_ref):
    xn = _rms(x_ref[...], g_ref[...]).astype(BF16)
    o_ref[...] = jnp.dot(xn, w_ref[...], preferred_element_type=F32).astype(o_ref.dtype)


def _norm_proj(x, g, w, out_dtype, tm=512, tn=1024):
    m, d = x.shape
    n = w.shape[1]
    return pl.pallas_call(
        _norm_proj_kernel,
        grid=(m // tm, n // tn),
        in_specs=[
            pl.BlockSpec((tm, d), lambda i, j: (i, 0)),
            pl.BlockSpec((1, d), lambda i, j: (0, 0)),
            pl.BlockSpec((d, tn), lambda i, j: (0, j)),
        ],
        out_specs=pl.BlockSpec((tm, tn), lambda i, j: (i, j)),
        out_shape=jax.ShapeDtypeStruct((m, n), out_dtype),
        compiler_params=pltpu.CompilerParams(
            dimension_semantics=("parallel", "parallel"), vmem_limit_bytes=VMEM_LIMIT),
        name="norm_proj",
    )(x, g.reshape(1, d), w.astype(BF16))


def _route(x, g_ref, whi_ref, wlo_ref, bias_ref, run_ref):
    tm = x.shape[0]
    xn = _rms(x, g_ref[...])
    xhi = xn.astype(BF16)
    xlo = (xn - xhi.astype(F32)).astype(BF16)
    logits = (jnp.dot(xhi, whi_ref[...], preferred_element_type=F32)
              + jnp.dot(xhi, wlo_ref[...], preferred_element_type=F32)
              + jnp.dot(xlo, whi_ref[...], preferred_element_type=F32)) + bias_ref[...]
    lane = lax.broadcasted_iota(jnp.int32, (tm, LANES), 1)
    neg = jnp.float32(-jnp.inf)

    def first_argmax(vals):
        mx = jnp.max(vals, axis=-1, keepdims=True)
        idx = jnp.min(jnp.where(vals == mx, lane, LANES), axis=-1, keepdims=True)
        return mx, idx

    gl = jnp.where(lane < MOE_GROUPS, logits, neg)
    gmax, gidx = first_argmax(gl)
    w_g = 1.0 / jnp.sum(jnp.exp(gl - gmax), axis=-1, keepdims=True)
    lo = MOE_GROUPS + MOE_PER_GROUP * gidx
    el = jnp.where((lane >= lo) & (lane < lo + MOE_PER_GROUP), logits, neg)
    m1, i1 = first_argmax(el)
    m2, i2 = first_argmax(jnp.where(lane == i1, neg, el))
    e21 = jnp.exp(m2 - m1)
    w1 = w_g / (1.0 + e21)
    w2 = w_g * e21 / (1.0 + e21)
    e1 = i1 - MOE_GROUPS
    e2 = i2 - MOE_GROUPS

    onehot = ((lane == e1) | (lane == e2)).astype(BF16)
    r = lax.broadcasted_iota(jnp.int32, (tm, tm), 0)
    c = lax.broadcasted_iota(jnp.int32, (tm, tm), 1)
    before = (c < r).astype(BF16)
    tot = run_ref[...] + jnp.dot(before, onehot, preferred_element_type=F32)
    r1 = jnp.sum(jnp.where(lane == e1, tot, 0.0), axis=-1, keepdims=True)
    r2 = jnp.sum(jnp.where(lane == e2, tot, 0.0), axis=-1, keepdims=True)
    run_ref[...] = run_ref[...] + jnp.sum(onehot.astype(F32), axis=0, keepdims=True)

    meta = jnp.where(lane == 0, e1.astype(F32), jnp.where(lane == 1, e2.astype(F32),
                     jnp.where(lane == 2, r1, jnp.where(lane == 3, r2, 0.0))))
    meta_t = jnp.transpose(meta)[:SUBLANES, :].astype(jnp.int32)
    wts = jnp.where(lane == 0, w1, jnp.where(lane == 1, w2, 0.0))
    return meta_t, wts


def _xattn_router_kernel(x_ref, g_ref, wq_ref, kt_ref, v_ref, wo_ref, mg_ref, whi_ref,
                         wlo_ref, bias_ref, o_ref, meta_ref, wts_ref, cnt_ref, run_ref):
    @pl.when((pl.program_id(0) == 0) & (pl.program_id(1) == 0))
    def _():
        run_ref[...] = jnp.zeros_like(run_ref)

    x = x_ref[0]
    xn = _rms(x, g_ref[...]).astype(BF16)
    q = jnp.dot(xn, wq_ref[...], preferred_element_type=F32) * (XA_HEAD_DIM ** -0.5)
    q = q.astype(BF16)
    heads = []
    for h in range(XA_HEADS):
        cols = slice(h * XA_HEAD_DIM, (h + 1) * XA_HEAD_DIM)
        s = jnp.dot(q[:, cols], kt_ref[0, cols, :], preferred_element_type=F32)
        e = jnp.exp(s - jnp.max(s, axis=-1, keepdims=True))
        p = (e / jnp.sum(e, axis=-1, keepdims=True)).astype(BF16)
        heads.append(jnp.dot(p, v_ref[0, :, cols], preferred_element_type=F32))
    o = jnp.concatenate(heads, axis=1).astype(BF16)
    out = x + jnp.dot(o, wo_ref[...], preferred_element_type=F32)
    o_ref[0] = out
    meta, wts = _route(out, mg_ref, whi_ref, wlo_ref, bias_ref, run_ref)
    meta_ref[...] = meta
    wts_ref[...] = wts
    cnt_ref[...] = run_ref[...]


def _xattn_router_layer(x, mem, mem_g, norm_g, w_q, w_kv, w_o, moe_g, w_group, b_group,
                        w_expert, b_expert, tm=512):
    bsz, seq, d = x.shape
    m = mem.shape[1]
    nt = seq // tm
    kv = _norm_proj(mem.reshape(bsz * m, d), mem_g, w_kv, BF16).reshape(bsz, m, 2 * d)
    kt = kv[..., :d].transpose(0, 2, 1)
    v = kv[..., d:]
    nr = MOE_GROUPS + MOE_EXPERTS
    w_r = jnp.zeros((d, LANES), F32).at[:, :MOE_GROUPS].set(w_group.astype(F32))
    w_r = w_r.at[:, MOE_GROUPS:nr].set(w_expert.astype(F32))
    bias = jnp.zeros((1, LANES), F32).at[0, :MOE_GROUPS].set(b_group.astype(F32))
    bias = bias.at[0, MOE_GROUPS:nr].set(b_expert.astype(F32))
    w_hi = w_r.astype(BF16)
    w_lo = (w_r - w_hi.astype(F32)).astype(BF16)
    xspec = pl.BlockSpec((1, tm, d), lambda b, i: (b, i, 0))
    return pl.pallas_call(
        _xattn_router_kernel,
        grid=(bsz, nt),
        in_specs=[
            xspec,
            _const_spec((1, d)),
            _const_spec((d, d)),
            pl.BlockSpec((1, d, m), lambda b, i: (b, 0, 0)),
            pl.BlockSpec((1, m, d), lambda b, i: (b, 0, 0)),
            _const_spec((d, d)),
            _const_spec((1, d)),
            _const_spec((d, LANES)),
            _const_spec((d, LANES)),
            _const_spec((1, LANES)),
        ],
        out_specs=[
            xspec,
            pl.BlockSpec((SUBLANES, tm), lambda b, i: (0, b * nt + i)),
            pl.BlockSpec((tm, LANES), lambda b, i: (b * nt + i, 0)),
            pl.BlockSpec((1, LANES), lambda b, i: (0, 0)),
        ],
        out_shape=[
            jax.ShapeDtypeStruct(x.shape, F32),
            jax.ShapeDtypeStruct((SUBLANES, bsz * seq), jnp.int32),
            jax.ShapeDtypeStruct((bsz * seq, LANES), F32),
            jax.ShapeDtypeStruct((1, LANES), F32),
        ],
        scratch_shapes=[pltpu.VMEM((1, LANES), F32)],
        compiler_params=pltpu.CompilerParams(
            dimension_semantics=("arbitrary", "arbitrary"), vmem_limit_bytes=VMEM_LIMIT),
        name="xattn_router",
    )(x, norm_g.reshape(1, d), w_q.astype(BF16), kt, v, w_o.astype(BF16),
      moe_g.reshape(1, d), w_hi, w_lo, bias)


def _row_copy(src, src_row, dst, dst_row, sem):
    return pltpu.make_async_copy(src.at[pl.ds(src_row, 1), :], dst.at[pl.ds(dst_row, 1), :], sem)


def _dispatch_kernel(ps_ref, pe_ref, nused_ref, meta_ref, x_ref, g_ref, xd_hbm, xbuf, zbuf,
                     sem, zsem):
    i = pl.program_id(0)
    n = pl.num_programs(0)
    tm = x_ref.shape[0]
    n_blocks = xd_hbm.shape[0] // MOE_BLK
    slot = i % 2

    def zero_block(row):
        rows = pl.ds(pl.multiple_of(row, MOE_BLK), MOE_BLK)
        return pltpu.make_async_copy(zbuf, xd_hbm.at[rows, :], zsem)

    @pl.when(i == 0)
    def _():
        zbuf[...] = jnp.zeros_like(zbuf)
        for e in range(MOE_EXPERTS):
            @pl.when(pe_ref[e] > ps_ref[e])
            def _():
                zero_block(pe_ref[e] - MOE_BLK).start()

        def start_unused(b, carry):
            zero_block(b * MOE_BLK).start()
            return carry
        lax.fori_loop(nused_ref[0], n_blocks, start_unused, 0)
        for e in range(MOE_EXPERTS):
            @pl.when(pe_ref[e] > ps_ref[e])
            def _():
                zero_block(0).wait()

        def wait_unused(b, carry):
            zero_block(0).wait()
            return carry
        lax.fori_loop(nused_ref[0], n_blocks, wait_unused, 0)

    def wait_slot(s):
        for _ in range(2):
            pltpu.make_async_copy(xbuf.at[s], xd_hbm.at[pl.ds(0, tm), :], sem.at[s]).wait()

    @pl.when(i >= 2)
    def _():
        wait_slot(slot)

    xbuf[slot] = _rms(x_ref[...], g_ref[...])

    def body(r, carry):
        p1 = ps_ref[meta_ref[0, r]] + meta_ref[2, r]
        p2 = ps_ref[meta_ref[1, r]] + meta_ref[3, r]
        _row_copy(xbuf.at[slot], r, xd_hbm, p1, sem.at[slot]).start()
        _row_copy(xbuf.at[slot], r, xd_hbm, p2, sem.at[slot]).start()
        return carry
    lax.fori_loop(0, tm, body, 0, unroll=8)

    @pl.when(i == n - 1)
    def _():
        wait_slot(1 - slot)
        wait_slot(slot)


def _expert_kernel(bexp_ref, nused_ref, xd_ref, wg_ref, wu_ref, wd_ref, y_ref, wgb, wub, wdb):
    b = pl.program_id(0)
    used = b < nused_ref[0]
    e = bexp_ref[b]
    e_prev = bexp_ref[jnp.maximum(b - 1, 0)]

    @pl.when(used & ((b == 0) | (e != e_prev)))
    def _():
        wgb[...] = wg_ref[...].astype(BF16)
        wub[...] = wu_ref[...].astype(BF16)
        wdb[...] = wd_ref[...].astype(BF16)

    @pl.when(used)
    def _():
        xb = xd_ref[...].astype(BF16)
        a = jnp.dot(xb, wgb[...], preferred_element_type=F32)
        up = jnp.dot(xb, wub[...], preferred_element_type=F32)
        h = (a * (1.0 / (1.0 + jnp.exp(-a))) * up).astype(BF16)
        y_ref[...] = jnp.dot(h, wdb[...], preferred_element_type=F32)

    @pl.when(jnp.logical_not(used))
    def _():
        y_ref[...] = jnp.zeros_like(y_ref)


def _combine_start(meta_ref, ps_ref, tm, y_hbm, buf, sem, slot):
    def body(r, carry):
        p1 = ps_ref[meta_ref[0, r]] + meta_ref[2, r]
        p2 = ps_ref[meta_ref[1, r]] + meta_ref[3, r]
        _row_copy(y_hbm, p1, buf.at[slot], r, sem.at[slot]).start()
        _row_copy(y_hbm, p2, buf.at[slot], tm + r, sem.at[slot]).start()
        return carry
    lax.fori_loop(0, tm, body, 0, unroll=8)


def _combine_kernel(ps_ref, meta_ref, metan_ref, x_ref, wts_ref, fg_ref, y_hbm, o_ref, buf, sem,
                    *, final):
    i = pl.program_id(0)
    n = pl.num_programs(0)
    tm = x_ref.shape[0]
    slot = i % 2

    @pl.when(i == 0)
    def _():
        _combine_start(meta_ref, ps_ref, tm, y_hbm, buf, sem, 0)

    @pl.when(i + 1 < n)
    def _():
        _combine_start(metan_ref, ps_ref, tm, y_hbm, buf, sem, 1 - slot)

    pltpu.make_async_copy(y_hbm.at[pl.ds(0, 2 * tm), :], buf.at[slot], sem.at[slot]).wait()
    w = wts_ref[...]
    out = x_ref[...] + w[:, 0:1] * buf[slot, :tm, :] + w[:, 1:2] * buf[slot, tm:, :]
    if final:
        out = _rms(out, fg_ref[...])
    o_ref[...] = out


def _moe_layer(x, meta, wts, cnt, norm_g, w_gate, w_up, w_down, final_g=None):
    bsz, seq, d = x.shape
    n_tok = bsz * seq
    xt = x.reshape(n_tok, d)

    counts = cnt[0, :MOE_EXPERTS].astype(jnp.int32)
    padded = ((counts + MOE_BLK - 1) // MOE_BLK) * MOE_BLK
    pad_end = jnp.cumsum(padded)
    pad_start = pad_end - padded
    n_blocks = (n_tok * 2) // MOE_BLK + MOE_EXPERTS
    n_pad = n_blocks * MOE_BLK
    blk_row = jnp.arange(n_blocks, dtype=jnp.int32) * MOE_BLK
    block_expert = jnp.minimum(jnp.sum(blk_row[:, None] >= pad_end[None, :], axis=1),
                               MOE_EXPERTS - 1).astype(jnp.int32)
    n_used = (pad_end[-1:] // MOE_BLK).astype(jnp.int32)

    smem_blk = functools.partial(pl.BlockSpec, memory_space=pltpu.SMEM)
    tm = DISPATCH_TM
    n_tiles = n_tok // tm
    xd = pl.pallas_call(
        _dispatch_kernel,
        grid_spec=pltpu.PrefetchScalarGridSpec(
            num_scalar_prefetch=3,
            grid=(n_tiles,),
            in_specs=[
                smem_blk((SUBLANES, tm), lambda i, *_: (0, i)),
                pl.BlockSpec((tm, d), lambda i, *_: (i, 0)),
                pl.BlockSpec((1, d), lambda i, *_: (0, 0)),
            ],
            out_specs=pl.BlockSpec(memory_space=pl.ANY),
            scratch_shapes=[pltpu.VMEM((2, tm, d), F32), pltpu.VMEM((MOE_BLK, d), F32),
                            pltpu.SemaphoreType.DMA((2,)), pltpu.SemaphoreType.DMA(())],
        ),
        out_shape=jax.ShapeDtypeStruct((n_pad, d), F32),
        compiler_params=pltpu.CompilerParams(
            dimension_semantics=("arbitrary",), vmem_limit_bytes=VMEM_LIMIT),
        name="moe_dispatch",
    )(pad_start, pad_end, n_used, meta, xt, norm_g.reshape(1, d))

    def used_blk(b, be, nu):
        return (jnp.maximum(jnp.minimum(b, nu[0] - 1), 0), 0)

    y = pl.pallas_call(
        _expert_kernel,
        grid_spec=pltpu.PrefetchScalarGridSpec(
            num_scalar_prefetch=2,
            grid=(n_blocks,),
            in_specs=[
                pl.BlockSpec((MOE_BLK, d), used_blk),
                pl.BlockSpec((None, d, MOE_HIDDEN), lambda b, be, nu: (be[b], 0, 0)),
                pl.BlockSpec((None, d, MOE_HIDDEN), lambda b, be, nu: (be[b], 0, 0)),
                pl.BlockSpec((None, MOE_HIDDEN, d), lambda b, be, nu: (be[b], 0, 0)),
            ],
            out_specs=pl.BlockSpec((MOE_BLK, d), lambda b, be, nu: (b, 0)),
            scratch_shapes=[pltpu.VMEM((d, MOE_HIDDEN), BF16), pltpu.VMEM((d, MOE_HIDDEN), BF16),
                            pltpu.VMEM((MOE_HIDDEN, d), BF16)],
        ),
        out_shape=jax.ShapeDtypeStruct((n_pad, d), F32),
        compiler_params=pltpu.CompilerParams(
            dimension_semantics=("arbitrary",), vmem_limit_bytes=VMEM_LIMIT),
        name="moe_experts",
    )(block_expert, n_used, xd, w_gate, w_up, w_down)

    tm = COMBINE_TM
    n_tiles = n_tok // tm
    final = final_g is not None
    fg = (final_g if final else jnp.ones((d,), F32)).reshape(1, d)
    out = pl.pallas_call(
        functools.partial(_combine_kernel, final=final),
        grid_spec=pltpu.PrefetchScalarGridSpec(
            num_scalar_prefetch=1,
            grid=(n_tiles,),
            in_specs=[
                smem_blk((SUBLANES, tm), lambda i, ps: (0, i)),
                smem_blk((SUBLANES, tm), lambda i, ps: (0, jnp.minimum(i + 1, n_tiles - 1))),
                pl.BlockSpec((tm, d), lambda i, ps: (i, 0)),
                pl.BlockSpec((tm, LANES), lambda i, ps: (i, 0)),
                pl.BlockSpec((1, d), lambda i, ps: (0, 0)),
                pl.BlockSpec(memory_space=pl.ANY),
            ],
            out_specs=pl.BlockSpec((tm, d), lambda i, ps: (i, 0)),
            scratch_shapes=[pltpu.VMEM((2, 2 * tm, d), F32), pltpu.SemaphoreType.DMA((2,))],
        ),
        out_shape=jax.ShapeDtypeStruct((n_tok, d), F32),
        compiler_params=pltpu.CompilerParams(
            dimension_semantics=("arbitrary",), vmem_limit_bytes=VMEM_LIMIT),
        name="moe_combine",
    )(pad_start, meta, meta, xt, wts, fg, y)
    return out.reshape(bsz, seq, d)


def kernel(x, mem, s5_norm, s5_w_in, s5_lambda_re, s5_lambda_im, s5_log_dt, s5_b_re, s5_b_im, s5_c_re, s5_c_im, s5_d, s5_w_out, gm_norm, gm_w_in, gm_v_norm, gm_w_s, gm_b_s, gm_w_out, mem_norm, xa_norm, xa_w_q, xa_w_kv, xa_w_o, moe_norm, moe_w_group, moe_b_group, moe_w_expert, moe_b_expert, moe_w_gate, moe_w_up, moe_w_down, final_norm):
    for i in range(DEPTH):
        j = i // 2
        if i % 2 == 0:
            x = _s5_layer(x, s5_norm[j], s5_w_in[j], s5_lambda_re[j], s5_lambda_im[j],
                          s5_log_dt[j], s5_b_re[j], s5_b_im[j], s5_c_re[j], s5_c_im[j],
                          s5_d[j], s5_w_out[j])
        else:
            x = _gmlp_layer(x, gm_norm[j], gm_w_in[j], gm_v_norm[j], gm_w_s[j], gm_b_s[j],
                            gm_w_out[j])
        x, meta, wts, cnt = _xattn_router_layer(
            x, mem, mem_norm, xa_norm[i], xa_w_q[i], xa_w_kv[i], xa_w_o[i], moe_norm[i],
            moe_w_group[i], moe_b_group[i], moe_w_expert[i], moe_b_expert[i])
        x = _moe_layer(x, meta, wts, cnt, moe_norm[i], moe_w_gate[i], moe_w_up[i],
                       moe_w_down[i], final_g=final_norm if i == DEPTH - 1 else None)
    return x
```

```python
import functools

import jax
import jax.numpy as jnp
from jax import lax
from jax.experimental import pallas as pl
from jax.experimental.pallas import tpu as pltpu

F32 = jnp.float32
BF16 = jnp.bfloat16

D_MODEL = 1024
DEPTH = 2
CHUNK = 64
S5_GROUP_CH = 16
S5_GROUPS = 64
S5_STATE = 64
GM_HALF = 2 * D_MODEL
GM_GROUPS = 8
GM_SPAN = 128
GM_GROUP_CH = GM_HALF // GM_GROUPS
XA_HEADS = 4
XA_HEAD_DIM = D_MODEL // XA_HEADS
MOE_GROUPS = 4
MOE_PER_GROUP = 8
MOE_EXPERTS = MOE_GROUPS * MOE_PER_GROUP
MOE_HIDDEN = D_MODEL // 2
RMS_EPS = 1e-6

LANES = 128
SUBLANES = 8
VMEM_LIMIT = 56 * 1024 * 1024

S5_CT = 256
S5_NCT = D_MODEL // S5_CT
S5_STATES_CT = (S5_CT // S5_GROUP_CH) * S5_STATE
S5_SLABS = S5_STATES_CT // LANES
S5_TC = 128
S5_PITCH = S5_TC + 4
S5_SCAN_CT = 2

MOE_BLK = 256
DISPATCH_TM = 256
COMBINE_TM = 256


def _rms(x, g):
    ms = jnp.mean(x * x, axis=-1, keepdims=True)
    return x * lax.rsqrt(ms + RMS_EPS) * g


def _const_spec(shape):
    nd = len(shape)
    return pl.BlockSpec(shape, lambda *_: (0,) * nd, pipeline_mode=pl.Buffered(1))


def _s5_kernel(x_ref, g_ref, win_ref, bblk_ref, cblk_ref, are_ref, aim_ref, d_ref,
               wout_ref, o_ref, hre_ref, him_ref, bre_ref, bim_ref, sre_ref, sim_ref):
    nb = x_ref.shape[0]
    tc = x_ref.shape[1]

    @pl.when(pl.program_id(0) == 0)
    def _():
        hre_ref[...] = jnp.zeros_like(hre_ref)
        him_ref[...] = jnp.zeros_like(him_ref)

    x = x_ref[...].reshape(nb * tc, D_MODEL)
    xn = _rms(x, g_ref[...]).astype(BF16)
    u = jnp.dot(xn, win_ref[...], preferred_element_type=F32)

    y_parts = []
    for ct0 in range(0, S5_NCT, S5_SCAN_CT):
        tiles = range(ct0, ct0 + S5_SCAN_CT)
        for c, ct in enumerate(tiles):
            u_ct = u[:, ct * S5_CT:(ct + 1) * S5_CT]
            bu = jnp.dot(u_ct.astype(BF16), bblk_ref[ct], preferred_element_type=F32)
            for b in range(nb):
                for j in range(S5_SLABS):
                    rows = slice(b * tc, (b + 1) * tc)
                    bre_ref[c, b, pl.ds(j * S5_PITCH, tc), :] = bu[rows, j * LANES:(j + 1) * LANES]
                    bim_ref[c, b, pl.ds(j * S5_PITCH, tc), :] = bu[
                        rows, S5_STATES_CT + j * LANES:S5_STATES_CT + (j + 1) * LANES]
        chains = [(c, ct, b) for c, ct in enumerate(tiles) for b in range(nb)]
        a_re = [are_ref[ct] for ct in tiles]
        a_im = [aim_ref[ct] for ct in tiles]

        def step(t, carry):
            idx = pl.ds(t, S5_SLABS, stride=S5_PITCH)
            bu_t = [(bre_ref[c, b, idx, :], bim_ref[c, b, idx, :]) for c, _, b in chains]
            new = []
            for k, (c, _, b) in enumerate(chains):
                hr, hi = carry[2 * k], carry[2 * k + 1]
                new.append(a_re[c] * hr - a_im[c] * hi + bu_t[k][0])
                new.append(a_re[c] * hi + a_im[c] * hr + bu_t[k][1])
            for k, (c, _, b) in enumerate(chains):
                sre_ref[c, b, idx, :] = new[2 * k]
                sim_ref[c, b, idx, :] = new[2 * k + 1]
            return tuple(new)

        init = []
        for _, ct, b in chains:
            init += [hre_ref[ct, b], him_ref[ct, b]]
        fin = lax.fori_loop(0, tc, step, tuple(init), unroll=2)
        for k, (_, ct, b) in enumerate(chains):
            hre_ref[ct, b] = fin[2 * k]
            him_ref[ct, b] = fin[2 * k + 1]

        for c, ct in enumerate(tiles):
            xs = []
            for b in range(nb):
                cols = [sre_ref[c, b, pl.ds(j * S5_PITCH, tc), :] for j in range(S5_SLABS)]
                cols += [sim_ref[c, b, pl.ds(j * S5_PITCH, tc), :] for j in range(S5_SLABS)]
                xs.append(jnp.concatenate(cols, axis=1))
            xst = jnp.concatenate(xs, axis=0).astype(BF16)
            y_parts.append(jnp.dot(xst, cblk_ref[ct], preferred_element_type=F32))

    y = jnp.concatenate(y_parts, axis=1) + d_ref[...] * u
    yg = jax.nn.gelu(y).astype(BF16)
    z = jnp.dot(yg, wout_ref[...], preferred_element_type=F32)
    out = x + z[:, :D_MODEL] * (1.0 / (1.0 + jnp.exp(-z[:, D_MODEL:])))
    o_ref[...] = out.reshape(nb, tc, D_MODEL)


def _s5_discretize(lam_re, lam_im, log_dt, b_re, b_im, c_re, c_im):
    lr = lam_re.astype(F32)
    li = lam_im.astype(F32)
    dt = jnp.exp(log_dt.astype(F32))[:, None]
    mag = jnp.exp(lr * dt)
    ab_re = mag * jnp.cos(li * dt)
    ab_im = mag * jnp.sin(li * dt)
    den = lr * lr + li * li
    coef_re = ((ab_re - 1.0) * lr + ab_im * li) / den
    coef_im = (ab_im * lr - (ab_re - 1.0) * li) / den
    br = b_re.astype(F32)
    bi = b_im.astype(F32)
    bb_re = coef_re[..., None] * br - coef_im[..., None] * bi
    bb_im = coef_re[..., None] * bi + coef_im[..., None] * br
    gpt = S5_CT // S5_GROUP_CH
    eye = jnp.eye(gpt, dtype=F32)

    def in_blocks(bb):
        t = bb.reshape(S5_NCT, gpt, S5_STATE, S5_GROUP_CH).transpose(0, 1, 3, 2)
        blk = t[:, :, :, None, :] * eye[None, :, None, :, None]
        return blk.reshape(S5_NCT, S5_CT, S5_STATES_CT)

    def out_blocks(c):
        t = c.reshape(S5_NCT, gpt, S5_GROUP_CH, S5_STATE).transpose(0, 1, 3, 2)
        blk = t[:, :, :, None, :] * eye[None, :, None, :, None]
        return blk.reshape(S5_NCT, S5_STATES_CT, S5_CT)

    bblk = jnp.concatenate([in_blocks(bb_re), in_blocks(bb_im)], axis=2).astype(BF16)
    cblk = jnp.concatenate([out_blocks(c_re.astype(F32)),
                            out_blocks(-c_im.astype(F32))], axis=1).astype(BF16)
    a_re = ab_re.reshape(S5_NCT, S5_SLABS, LANES)
    a_im = ab_im.reshape(S5_NCT, S5_SLABS, LANES)
    return bblk, cblk, a_re, a_im


def _s5_layer(x, norm_g, w_in, lam_re, lam_im, log_dt, b_re, b_im, c_re, c_im, d_skip, w_out):
    bsz, seq, d = x.shape
    bblk, cblk, a_re, a_im = _s5_discretize(lam_re, lam_im, log_dt, b_re, b_im, c_re, c_im)
    xspec = pl.BlockSpec((bsz, S5_TC, d), lambda k: (0, k, 0))
    return pl.pallas_call(
        _s5_kernel,
        grid=(seq // S5_TC,),
        in_specs=[
            xspec,
            _const_spec((1, d)),
            _const_spec((d, d)),
            _const_spec(bblk.shape),
            _const_spec(cblk.shape),
            _const_spec(a_re.shape),
            _const_spec(a_im.shape),
            _const_spec((1, d)),
            _const_spec((d, 2 * d)),
        ],
        out_specs=xspec,
        out_shape=jax.ShapeDtypeStruct(x.shape, F32),
        scratch_shapes=[
            pltpu.VMEM((S5_NCT, bsz, S5_SLABS, LANES), F32),
            pltpu.VMEM((S5_NCT, bsz, S5_SLABS, LANES), F32),
            pltpu.VMEM((S5_SCAN_CT, bsz, S5_SLABS * S5_PITCH, LANES), F32),
            pltpu.VMEM((S5_SCAN_CT, bsz, S5_SLABS * S5_PITCH, LANES), F32),
            pltpu.VMEM((S5_SCAN_CT, bsz, S5_SLABS * S5_PITCH, LANES), F32),
            pltpu.VMEM((S5_SCAN_CT, bsz, S5_SLABS * S5_PITCH, LANES), F32),
        ],
        compiler_params=pltpu.CompilerParams(
            dimension_semantics=("arbitrary",), vmem_limit_bytes=VMEM_LIMIT),
        name="s5_layer",
    )(x, norm_g.reshape(1, d), w_in.astype(BF16), bblk, cblk, a_re, a_im,
      d_skip.reshape(1, d).astype(F32), w_out.astype(BF16))


def _gmlp_kernel(x_ref, g_ref, win_ref, vn_ref, ws_ref, bst_ref, wout_ref, o_ref):
    tm = x_ref.shape[0]
    x = x_ref[...]
    xn = _rms(x, g_ref[...]).astype(BF16)
    u = jax.nn.gelu(jnp.dot(xn, win_ref[:, :GM_HALF], preferred_element_type=F32))
    v = jax.nn.gelu(jnp.dot(xn, win_ref[:, GM_HALF:], preferred_element_type=F32))
    vb = _rms(v, vn_ref[...]).astype(BF16)
    row = lax.broadcasted_iota(jnp.int32, (GM_SPAN, GM_SPAN), 0) // CHUNK
    col = lax.broadcasted_iota(jnp.int32, (GM_SPAN, GM_SPAN), 1) // CHUNK
    causal = row >= col
    ws = [jnp.where(causal, ws_ref[g], 0.0).astype(BF16) for g in range(GM_GROUPS)]
    spans = []
    for s in range(tm // GM_SPAN):
        parts = []
        for g in range(GM_GROUPS):
            vblk = vb[s * GM_SPAN:(s + 1) * GM_SPAN, g * GM_GROUP_CH:(g + 1) * GM_GROUP_CH]
            parts.append(jnp.dot(ws[g], vblk, preferred_element_type=F32)
                         + bst_ref[:, g:g + 1])
        spans.append(jnp.concatenate(parts, axis=1))
    mixed = jnp.concatenate(spans, axis=0)
    p = (u * mixed).astype(BF16)
    o_ref[...] = x + jnp.dot(p, wout_ref[...], preferred_element_type=F32)


def _gmlp_layer(x, norm_g, w_in, v_norm, w_s, b_s, w_out, tm=256):
    bsz, seq, d = x.shape
    n_tok = bsz * seq
    xt = x.reshape(n_tok, d)
    xspec = pl.BlockSpec((tm, d), lambda i: (i, 0))
    out = pl.pallas_call(
        _gmlp_kernel,
        grid=(n_tok // tm,),
        in_specs=[
            xspec,
            _const_spec((1, d)),
            _const_spec((d, 2 * GM_HALF)),
            _const_spec((1, GM_HALF)),
            _const_spec((GM_GROUPS, GM_SPAN, GM_SPAN)),
            _const_spec((GM_SPAN, GM_GROUPS)),
            _const_spec((GM_HALF, d)),
        ],
        out_specs=xspec,
        out_shape=jax.ShapeDtypeStruct((n_tok, d), F32),
        compiler_params=pltpu.CompilerParams(
            dimension_semantics=("parallel",), vmem_limit_bytes=VMEM_LIMIT),
        name="gmlp_layer",
    )(xt, norm_g.reshape(1, d), w_in.astype(BF16), v_norm.reshape(1, GM_HALF),
      w_s.astype(F32), b_s.T.astype(F32), w_out.astype(BF16))
    return out.reshape(bsz, seq, d)


def _norm_proj_kernel(x_ref, g_ref, w_ref, o_ref):
    xn = _rms(x_ref[...], g_ref[...]).astype(BF16)
    o_ref[...] = jnp.dot(xn, w_ref[...], preferred_element_type=F32).astype(o_ref.dtype)


def _norm_proj(x, g, w, out_dtype, tm=512, tn=1024):
    m, d = x.shape
    n = w.shape[1]
    return pl.pallas_call(
        _norm_proj_kernel,
        grid=(m // tm, n // tn),
        in_specs=[
            pl.BlockSpec((tm, d), lambda i, j: (i, 0)),
            pl.BlockSpec((1, d), lambda i, j: (0, 0)),
            pl.BlockSpec((d, tn), lambda i, j: (0, j)),
        ],
        out_specs=pl.BlockSpec((tm, tn), lambda i, j: (i, j)),
        out_shape=jax.ShapeDtypeStruct((m, n), out_dtype),
        compiler_params=pltpu.CompilerParams(
            dimension_semantics=("parallel", "parallel"), vmem_limit_bytes=VMEM_LIMIT),
        name="norm_proj",
    )(x, g.reshape(1, d), w.astype(BF16))


def _route(x, g_ref, whi_ref, wlo_ref, bias_ref, run_ref):
    tm = x.shape[0]
    xn = _rms(x, g_ref[...])
    xhi = xn.astype(BF16)
    xlo = (xn - xhi.astype(F32)).astype(BF16)
    logits = (jnp.dot(xhi, whi_ref[...], preferred_element_type=F32)
              + jnp.dot(xhi, wlo_ref[...], preferred_element_type=F32)
              + jnp.dot(xlo, whi_ref[...], preferred_element_type=F32)) + bias_ref[...]
    lane = lax.broadcasted_iota(jnp.int32, (tm, LANES), 1)
    neg = jnp.float32(-jnp.inf)

    def first_argmax(vals):
        mx = jnp.max(vals, axis=-1, keepdims=True)
        idx = jnp.min(jnp.where(vals == mx, lane, LANES), axis=-1, keepdims=True)
        return mx, idx

    gl = jnp.where(lane < MOE_GROUPS, logits, neg)
    gmax, gidx = first_argmax(gl)
    w_g = 1.0 / jnp.sum(jnp.exp(gl - gmax), axis=-1, keepdims=True)
    lo = MOE_GROUPS + MOE_PER_GROUP * gidx
    el = jnp.where((lane >= lo) & (lane < lo + MOE_PER_GROUP), logits, neg)
    m1, i1 = first_argmax(el)
    m2, i2 = first_argmax(jnp.where(lane == i1, neg, el))
    e21 = jnp.exp(m2 - m1)
    w1 = w_g / (1.0 + e21)
    w2 = w_g * e21 / (1.0 + e21)
    e1 = i1 - MOE_GROUPS
    e2 = i2 - MOE_GROUPS

    onehot = ((lane == e1) | (lane == e2)).astype(BF16)
    r = lax.broadcasted_iota(jnp.int32, (tm, tm), 0)
    c = lax.broadcasted_iota(jnp.int32, (tm, tm), 1)
    before = (c < r).astype(BF16)
    tot = run_ref[...] + jnp.dot(before, onehot, preferred_element_type=F32)
    r1 = jnp.sum(jnp.where(lane == e1, tot, 0.0), axis=-1, keepdims=True)
    r2 = jnp.sum(jnp.where(lane == e2, tot, 0.0), axis=-1, keepdims=True)
    run_ref[...] = run_ref[...] + jnp.sum(onehot.astype(F32), axis=0, keepdims=True)

    meta = jnp.where(lane == 0, e1.astype(F32), jnp.where(lane == 1, e2.astype(F32),
                     jnp.where(lane == 2, r1, jnp.where(lane == 3, r2, 0.0))))
    meta_t = jnp.transpose(meta)[:SUBLANES, :].astype(jnp.int32)
    wts = jnp.where(lane == 0, w1, jnp.where(lane == 1, w2, 0.0))
    return meta_t, wts


def _xattn_router_kernel(x_ref, g_ref, wq_ref, kt_ref, v_ref, wo_ref, mg_ref, whi_ref,
                         wlo_ref, bias_ref, o_ref, meta_ref, wts_ref, cnt_ref, run_ref):
    @pl.when((pl.program_id(0) == 0) & (pl.program_id(1) == 0))
    def _():
        run_ref[...] = jnp.zeros_like(run_ref)

    x = x_ref[0]
    xn = _rms(x, g_ref[...]).astype(BF16)
    q = jnp.dot(xn, wq_ref[...], preferred_element_type=F32) * (XA_HEAD_DIM ** -0.5)
    q = q.astype(BF16)
    heads = []
    for h in range(XA_HEADS):
        cols = slice(h * XA_HEAD_DIM, (h + 1) * XA_HEAD_DIM)
        s = jnp.dot(q[:, cols], kt_ref[0, cols, :], preferred_element_type=F32)
        e = jnp.exp(s - jnp.max(s, axis=-1, keepdims=True))
        p = (e / jnp.sum(e, axis=-1, keepdims=True)).astype(BF16)
        heads.append(jnp.dot(p, v_ref[0, :, cols], preferred_element_type=F32))
    o = jnp.concatenate(heads, axis=1).astype(BF16)
    out = x + jnp.dot(o, wo_ref[...], preferred_element_type=F32)
    o_ref[0] = out
    meta, wts = _route(out, mg_ref, whi_ref, wlo_ref, bias_ref, run_ref)
    meta_ref[...] = meta
    wts_ref[...] = wts
    cnt_ref[...] = run_ref[...]


def _xattn_router_layer(x, mem, mem_g, norm_g, w_q, w_kv, w_o, moe_g, w_group, b_group,
                        w_expert, b_expert, tm=512):
    bsz, seq, d = x.shape
    m = mem.shape[1]
    nt = seq // tm
    kv = _norm_proj(mem.reshape(bsz * m, d), mem_g, w_kv, BF16).reshape(bsz, m, 2 * d)
    kt = kv[..., :d].transpose(0, 2, 1)
    v = kv[..., d:]
    nr = MOE_GROUPS + MOE_EXPERTS
    w_r = jnp.zeros((d, LANES), F32).at[:, :MOE_GROUPS].set(w_group.astype(F32))
    w_r = w_r.at[:, MOE_GROUPS:nr].set(w_expert.astype(F32))
    bias = jnp.zeros((1, LANES), F32).at[0, :MOE_GROUPS].set(b_group.astype(F32))
    bias = bias.at[0, MOE_GROUPS:nr].set(b_expert.astype(F32))
    w_hi = w_r.astype(BF16)
    w_lo = (w_r - w_hi.astype(F32)).astype(BF16)
    xspec = pl.BlockSpec((1, tm, d), lambda b, i: (b, i, 0))
    return pl.pallas_call(
        _xattn_router_kernel,
        grid=(bsz, nt),
        in_specs=[
            xspec,
            _const_spec((1, d)),
            _const_spec((d, d)),
            pl.BlockSpec((1, d, m), lambda b, i: (b, 0, 0)),
            pl.BlockSpec((1, m, d), lambda b, i: (b, 0, 0)),
            _const_spec((d, d)),
            _const_spec((1, d)),
            _const_spec((d, LANES)),
            _const_spec((d, LANES)),
            _const_spec((1, LANES)),
        ],
        out_specs=[
            xspec,
            pl.BlockSpec((SUBLANES, tm), lambda b, i: (0, b * nt + i)),
            pl.BlockSpec((tm, LANES), lambda b, i: (b * nt + i, 0)),
            pl.BlockSpec((1, LANES), lambda b, i: (0, 0)),
        ],
        out_shape=[
            jax.ShapeDtypeStruct(x.shape, F32),
            jax.ShapeDtypeStruct((SUBLANES, bsz * seq), jnp.int32),
            jax.ShapeDtypeStruct((bsz * seq, LANES), F32),
            jax.ShapeDtypeStruct((1, LANES), F32),
        ],
        scratch_shapes=[pltpu.VMEM((1, LANES), F32)],
        compiler_params=pltpu.CompilerParams(
            dimension_semantics=("arbitrary", "arbitrary"), vmem_limit_bytes=VMEM_LIMIT),
        name="xattn_router",
    )(x, norm_g.reshape(1, d), w_q.astype(BF16), kt, v, w_o.astype(BF16),
      moe_g.reshape(1, d), w_hi, w_lo, bias)


PACKED = D_MODEL // 2
U32 = jnp.uint32


def _pack_rows(v):
    lo = lax.bitcast_convert_type(v[:, :PACKED].astype(BF16).astype(F32), U32)
    hi = lax.bitcast_convert_type(v[:, PACKED:].astype(BF16).astype(F32), U32)
    return (hi & jnp.uint32(0xFFFF0000)) | (lo >> 16)


def _unpack_rows(p):
    lo = lax.bitcast_convert_type(p << 16, F32)
    hi = lax.bitcast_convert_type(p & jnp.uint32(0xFFFF0000), F32)
    return jnp.concatenate([lo, hi], axis=1)


def _row_copy(src, src_row, dst, dst_row, sem):
    return pltpu.make_async_copy(src.at[pl.ds(src_row, 1), :], dst.at[pl.ds(dst_row, 1), :], sem)


def _dispatch_kernel(ps_ref, pe_ref, nused_ref, meta_ref, x_ref, g_ref, xd_hbm, xbuf, zbuf,
                     sem, zsem):
    i = pl.program_id(0)
    n = pl.num_programs(0)
    tm = x_ref.shape[0]
    n_blocks = xd_hbm.shape[0] // MOE_BLK
    slot = i % 2

    def zero_block(row):
        rows = pl.ds(pl.multiple_of(row, MOE_BLK), MOE_BLK)
        return pltpu.make_async_copy(zbuf, xd_hbm.at[rows, :], zsem)

    @pl.when(i == 0)
    def _():
        zbuf[...] = jnp.zeros_like(zbuf)
        for e in range(MOE_EXPERTS):
            @pl.when(pe_ref[e] > ps_ref[e])
            def _():
                zero_block(pe_ref[e] - MOE_BLK).start()

        def start_unused(b, carry):
            zero_block(b * MOE_BLK).start()
            return carry
        lax.fori_loop(nused_ref[0], n_blocks, start_unused, 0)
        for e in range(MOE_EXPERTS):
            @pl.when(pe_ref[e] > ps_ref[e])
            def _():
                zero_block(0).wait()

        def wait_unused(b, carry):
            zero_block(0).wait()
            return carry
        lax.fori_loop(nused_ref[0], n_blocks, wait_unused, 0)

    def wait_slot(s):
        for _ in range(2):
            pltpu.make_async_copy(xbuf.at[s], xd_hbm.at[pl.ds(0, tm), :], sem.at[s]).wait()

    @pl.when(i >= 2)
    def _():
        wait_slot(slot)

    xbuf[slot] = _pack_rows(_rms(x_ref[...], g_ref[...]))

    def body(r, carry):
        p1 = ps_ref[meta_ref[0, r]] + meta_ref[2, r]
        p2 = ps_ref[meta_ref[1, r]] + meta_ref[3, r]
        _row_copy(xbuf.at[slot], r, xd_hbm, p1, sem.at[slot]).start()
        _row_copy(xbuf.at[slot], r, xd_hbm, p2, sem.at[slot]).start()
        return carry
    lax.fori_loop(0, tm, body, 0, unroll=8)

    @pl.when(i == n - 1)
    def _():
        wait_slot(1 - slot)
        wait_slot(slot)


def _expert_kernel(bexp_ref, nused_ref, xd_ref, wg_ref, wu_ref, wd_ref, y_ref, wgb, wub, wdb):
    b = pl.program_id(0)
    used = b < nused_ref[0]
    e = bexp_ref[b]
    e_prev = bexp_ref[jnp.maximum(b - 1, 0)]

    @pl.when(used & ((b == 0) | (e != e_prev)))
    def _():
        wgb[...] = wg_ref[...].astype(BF16)
        wub[...] = wu_ref[...].astype(BF16)
        wdb[...] = wd_ref[...].astype(BF16)

    @pl.when(used)
    def _():
        xb = _unpack_rows(xd_ref[...]).astype(BF16)
        a = jnp.dot(xb, wgb[...], preferred_element_type=F32)
        up = jnp.dot(xb, wub[...], preferred_element_type=F32)
        h = (a * (1.0 / (1.0 + jnp.exp(-a))) * up).astype(BF16)
        y_ref[...] = _pack_rows(jnp.dot(h, wdb[...], preferred_element_type=F32))

    @pl.when(jnp.logical_not(used))
    def _():
        y_ref[...] = jnp.zeros_like(y_ref)


def _combine_start(meta_ref, ps_ref, tm, y_hbm, buf, sem, slot):
    def body(r, carry):
        p1 = ps_ref[meta_ref[0, r]] + meta_ref[2, r]
        p2 = ps_ref[meta_ref[1, r]] + meta_ref[3, r]
        _row_copy(y_hbm, p1, buf.at[slot], r, sem.at[slot]).start()
        _row_copy(y_hbm, p2, buf.at[slot], tm + r, sem.at[slot]).start()
        return carry
    lax.fori_loop(0, tm, body, 0, unroll=8)


def _combine_kernel(ps_ref, meta_ref, metan_ref, x_ref, wts_ref, fg_ref, y_hbm, o_ref, buf, sem,
                    *, final):
    i = pl.program_id(0)
    n = pl.num_programs(0)
    tm = x_ref.shape[0]
    slot = i % 2

    @pl.when(i == 0)
    def _():
        _combine_start(meta_ref, ps_ref, tm, y_hbm, buf, sem, 0)

    @pl.when(i + 1 < n)
    def _():
        _combine_start(metan_ref, ps_ref, tm, y_hbm, buf, sem, 1 - slot)

    pltpu.make_async_copy(y_hbm.at[pl.ds(0, 2 * tm), :], buf.at[slot], sem.at[slot]).wait()
    w = wts_ref[...]
    out = (x_ref[...] + w[:, 0:1] * _unpack_rows(buf[slot, :tm, :])
           + w[:, 1:2] * _unpack_rows(buf[slot, tm:, :]))
    if final:
        out = _rms(out, fg_ref[...])
    o_ref[...] = out


def _moe_layer(x, meta, wts, cnt, norm_g, layer, w_gate, w_up, w_down, final_g=None):
    bsz, seq, d = x.shape
    n_tok = bsz * seq
    xt = x.reshape(n_tok, d)

    counts = cnt[0, :MOE_EXPERTS].astype(jnp.int32)
    padded = ((counts + MOE_BLK - 1) // MOE_BLK) * MOE_BLK
    pad_end = jnp.cumsum(padded)
    pad_start = pad_end - padded
    n_blocks = (n_tok * 2) // MOE_BLK + MOE_EXPERTS
    n_pad = n_blocks * MOE_BLK
    blk_row = jnp.arange(n_blocks, dtype=jnp.int32) * MOE_BLK
    block_expert = jnp.minimum(jnp.sum(blk_row[:, None] >= pad_end[None, :], axis=1),
                               MOE_EXPERTS - 1).astype(jnp.int32)
    n_used = (pad_end[-1:] // MOE_BLK).astype(jnp.int32)

    smem_blk = functools.partial(pl.BlockSpec, memory_space=pltpu.SMEM)
    tm = DISPATCH_TM
    n_tiles = n_tok // tm
    xd = pl.pallas_call(
        _dispatch_kernel,
        grid_spec=pltpu.PrefetchScalarGridSpec(
            num_scalar_prefetch=3,
            grid=(n_tiles,),
            in_specs=[
                smem_blk((SUBLANES, tm), lambda i, *_: (0, i)),
                pl.BlockSpec((tm, d), lambda i, *_: (i, 0)),
                pl.BlockSpec((1, d), lambda i, *_: (0, 0)),
            ],
            out_specs=pl.BlockSpec(memory_space=pl.ANY),
            scratch_shapes=[pltpu.VMEM((2, tm, PACKED), U32), pltpu.VMEM((MOE_BLK, PACKED), U32),
                            pltpu.SemaphoreType.DMA((2,)), pltpu.SemaphoreType.DMA(())],
        ),
        out_shape=jax.ShapeDtypeStruct((n_pad, PACKED), U32),
        compiler_params=pltpu.CompilerParams(
            dimension_semantics=("arbitrary",), vmem_limit_bytes=VMEM_LIMIT),
        name="moe_dispatch",
    )(pad_start, pad_end, n_used, meta, xt, norm_g.reshape(1, d))

    def used_blk(b, be, nu):
        return (jnp.maximum(jnp.minimum(b, nu[0] - 1), 0), 0)

    def expert_blk(b, be, nu):
        return (layer, be[b], 0, 0)

    y = pl.pallas_call(
        _expert_kernel,
        grid_spec=pltpu.PrefetchScalarGridSpec(
            num_scalar_prefetch=2,
            grid=(n_blocks,),
            in_specs=[
                pl.BlockSpec((MOE_BLK, PACKED), used_blk),
                pl.BlockSpec((None, None, d, MOE_HIDDEN), expert_blk),
                pl.BlockSpec((None, None, d, MOE_HIDDEN), expert_blk),
                pl.BlockSpec((None, None, MOE_HIDDEN, d), expert_blk),
            ],
            out_specs=pl.BlockSpec((MOE_BLK, PACKED), lambda b, be, nu: (b, 0)),
            scratch_shapes=[pltpu.VMEM((d, MOE_HIDDEN), BF16), pltpu.VMEM((d, MOE_HIDDEN), BF16),
                            pltpu.VMEM((MOE_HIDDEN, d), BF16)],
        ),
        out_shape=jax.ShapeDtypeStruct((n_pad, PACKED), U32),
        compiler_params=pltpu.CompilerParams(
            dimension_semantics=("arbitrary",), vmem_limit_bytes=VMEM_LIMIT),
        name="moe_experts",
    )(block_expert, n_used, xd, w_gate, w_up, w_down)

    tm = COMBINE_TM
    n_tiles = n_tok // tm
    final = final_g is not None
    fg = (final_g if final else jnp.ones((d,), F32)).reshape(1, d)
    out = pl.pallas_call(
        functools.partial(_combine_kernel, final=final),
        grid_spec=pltpu.PrefetchScalarGridSpec(
            num_scalar_prefetch=1,
            grid=(n_tiles,),
            in_specs=[
                smem_blk((SUBLANES, tm), lambda i, ps: (0, i)),
                smem_blk((SUBLANES, tm), lambda i, ps: (0, jnp.minimum(i + 1, n_tiles - 1))),
                pl.BlockSpec((tm, d), lambda i, ps: (i, 0)),
                pl.BlockSpec((tm, LANES), lambda i, ps: (i, 0)),
                pl.BlockSpec((1, d), lambda i, ps: (0, 0)),
                pl.BlockSpec(memory_space=pl.ANY),
            ],
            out_specs=pl.BlockSpec((tm, d), lambda i, ps: (i, 0)),
            scratch_shapes=[pltpu.VMEM((2, 2 * tm, PACKED), U32), pltpu.SemaphoreType.DMA((2,))],
        ),
        out_shape=jax.ShapeDtypeStruct((n_tok, d), F32),
        compiler_params=pltpu.CompilerParams(
            dimension_semantics=("arbitrary",), vmem_limit_bytes=VMEM_LIMIT),
        name="moe_combine",
    )(pad_start, meta, meta, xt, wts, fg, y)
    return out.reshape(bsz, seq, d)


def kernel(x, mem, s5_norm, s5_w_in, s5_lambda_re, s5_lambda_im, s5_log_dt, s5_b_re, s5_b_im, s5_c_re, s5_c_im, s5_d, s5_w_out, gm_norm, gm_w_in, gm_v_norm, gm_w_s, gm_b_s, gm_w_out, mem_norm, xa_norm, xa_w_q, xa_w_kv, xa_w_o, moe_norm, moe_w_group, moe_b_group, moe_w_expert, moe_b_expert, moe_w_gate, moe_w_up, moe_w_down, final_norm):
    for i in range(DEPTH):
        j = i // 2
        if i % 2 == 0:
            x = _s5_layer(x, s5_norm[j], s5_w_in[j], s5_lambda_re[j], s5_lambda_im[j],
                          s5_log_dt[j], s5_b_re[j], s5_b_im[j], s5_c_re[j], s5_c_im[j],
                          s5_d[j], s5_w_out[j])
        else:
            x = _gmlp_layer(x, gm_norm[j], gm_w_in[j], gm_v_norm[j], gm_w_s[j], gm_b_s[j],
                            gm_w_out[j])
        x, meta, wts, cnt = _xattn_router_layer(
            x, mem, mem_norm, xa_norm[i], xa_w_q[i], xa_w_kv[i], xa_w_o[i], moe_norm[i],
            moe_w_group[i], moe_b_group[i], moe_w_expert[i], moe_b_expert[i])
        x = _moe_layer(x, meta, wts, cnt, moe_norm[i], i, moe_w_gate, moe_w_up, moe_w_down,
                       final_g=final_norm if i == DEPTH - 1 else None)
    return x
```

```python
import functools

import jax
import jax.numpy as jnp
from jax import lax
from jax.experimental import pallas as pl
from jax.experimental.pallas import tpu as pltpu
from jax.experimental.pallas import tpu_sc as plsc

F32 = jnp.float32
BF16 = jnp.bfloat16

D_MODEL = 1024
DEPTH = 2
CHUNK = 64
S5_GROUP_CH = 16
S5_GROUPS = 64
S5_STATE = 64
GM_HALF = 2 * D_MODEL
GM_GROUPS = 8
GM_SPAN = 128
GM_GROUP_CH = GM_HALF // GM_GROUPS
XA_HEADS = 4
XA_HEAD_DIM = D_MODEL // XA_HEADS
MOE_GROUPS = 4
MOE_PER_GROUP = 8
MOE_EXPERTS = MOE_GROUPS * MOE_PER_GROUP
MOE_HIDDEN = D_MODEL // 2
RMS_EPS = 1e-6

LANES = 128
SUBLANES = 8
VMEM_LIMIT = 56 * 1024 * 1024

S5_CT = 256
S5_NCT = D_MODEL // S5_CT
S5_STATES_CT = (S5_CT // S5_GROUP_CH) * S5_STATE
S5_SLABS = S5_STATES_CT // LANES
S5_TC = 128
S5_PITCH = S5_TC + 4
S5_SCAN_CT = 2

MOE_BLK = 256
DISPATCH_TM = 256
COMBINE_TM = 256


def _rms(x, g):
    ms = jnp.mean(x * x, axis=-1, keepdims=True)
    return x * lax.rsqrt(ms + RMS_EPS) * g


def _const_spec(shape):
    nd = len(shape)
    return pl.BlockSpec(shape, lambda *_: (0,) * nd, pipeline_mode=pl.Buffered(1))


def _s5_kernel(x_ref, g_ref, win_ref, bblk_ref, cblk_ref, are_ref, aim_ref, d_ref,
               wout_ref, o_ref, hre_ref, him_ref, bre_ref, bim_ref, sre_ref, sim_ref):
    nb = x_ref.shape[0]
    tc = x_ref.shape[1]

    @pl.when(pl.program_id(0) == 0)
    def _():
        hre_ref[...] = jnp.zeros_like(hre_ref)
        him_ref[...] = jnp.zeros_like(him_ref)

    x = x_ref[...].reshape(nb * tc, D_MODEL)
    xn = _rms(x, g_ref[...]).astype(BF16)
    u = jnp.dot(xn, win_ref[...], preferred_element_type=F32)

    y_parts = []
    for ct0 in range(0, S5_NCT, S5_SCAN_CT):
        tiles = range(ct0, ct0 + S5_SCAN_CT)
        for c, ct in enumerate(tiles):
            u_ct = u[:, ct * S5_CT:(ct + 1) * S5_CT]
            bu = jnp.dot(u_ct.astype(BF16), bblk_ref[ct], preferred_element_type=F32)
            for b in range(nb):
                for j in range(S5_SLABS):
                    rows = slice(b * tc, (b + 1) * tc)
                    bre_ref[c, b, pl.ds(j * S5_PITCH, tc), :] = bu[rows, j * LANES:(j + 1) * LANES]
                    bim_ref[c, b, pl.ds(j * S5_PITCH, tc), :] = bu[
                        rows, S5_STATES_CT + j * LANES:S5_STATES_CT + (j + 1) * LANES]
        chains = [(c, ct, b) for c, ct in enumerate(tiles) for b in range(nb)]
        a_re = [are_ref[ct] for ct in tiles]
        a_im = [aim_ref[ct] for ct in tiles]

        def step(t, carry):
            idx = pl.ds(t, S5_SLABS, stride=S5_PITCH)
            bu_t = [(bre_ref[c, b, idx, :], bim_ref[c, b, idx, :]) for c, _, b in chains]
            new = []
            for k, (c, _, b) in enumerate(chains):
                hr, hi = carry[2 * k], carry[2 * k + 1]
                new.append(a_re[c] * hr - a_im[c] * hi + bu_t[k][0])
                new.append(a_re[c] * hi + a_im[c] * hr + bu_t[k][1])
            for k, (c, _, b) in enumerate(chains):
                sre_ref[c, b, idx, :] = new[2 * k]
                sim_ref[c, b, idx, :] = new[2 * k + 1]
            return tuple(new)

        init = []
        for _, ct, b in chains:
            init += [hre_ref[ct, b], him_ref[ct, b]]
        fin = lax.fori_loop(0, tc, step, tuple(init), unroll=2)
        for k, (_, ct, b) in enumerate(chains):
            hre_ref[ct, b] = fin[2 * k]
            him_ref[ct, b] = fin[2 * k + 1]

        for c, ct in enumerate(tiles):
            xs = []
            for b in range(nb):
                cols = [sre_ref[c, b, pl.ds(j * S5_PITCH, tc), :] for j in range(S5_SLABS)]
                cols += [sim_ref[c, b, pl.ds(j * S5_PITCH, tc), :] for j in range(S5_SLABS)]
                xs.append(jnp.concatenate(cols, axis=1))
            xst = jnp.concatenate(xs, axis=0).astype(BF16)
            y_parts.append(jnp.dot(xst, cblk_ref[ct], preferred_element_type=F32))

    y = jnp.concatenate(y_parts, axis=1) + d_ref[...] * u
    yg = jax.nn.gelu(y).astype(BF16)
    z = jnp.dot(yg, wout_ref[...], preferred_element_type=F32)
    out = x + z[:, :D_MODEL] * (1.0 / (1.0 + jnp.exp(-z[:, D_MODEL:])))
    o_ref[...] = out.reshape(nb, tc, D_MODEL)


def _s5_discretize(lam_re, lam_im, log_dt, b_re, b_im, c_re, c_im):
    lr = lam_re.astype(F32)
    li = lam_im.astype(F32)
    dt = jnp.exp(log_dt.astype(F32))[:, None]
    mag = jnp.exp(lr * dt)
    ab_re = mag * jnp.cos(li * dt)
    ab_im = mag * jnp.sin(li * dt)
    den = lr * lr + li * li
    coef_re = ((ab_re - 1.0) * lr + ab_im * li) / den
    coef_im = (ab_im * lr - (ab_re - 1.0) * li) / den
    br = b_re.astype(F32)
    bi = b_im.astype(F32)
    bb_re = coef_re[..., None] * br - coef_im[..., None] * bi
    bb_im = coef_re[..., None] * bi + coef_im[..., None] * br
    gpt = S5_CT // S5_GROUP_CH
    eye = jnp.eye(gpt, dtype=F32)

    def in_blocks(bb):
        t = bb.reshape(S5_NCT, gpt, S5_STATE, S5_GROUP_CH).transpose(0, 1, 3, 2)
        blk = t[:, :, :, None, :] * eye[None, :, None, :, None]
        return blk.reshape(S5_NCT, S5_CT, S5_STATES_CT)

    def out_blocks(c):
        t = c.reshape(S5_NCT, gpt, S5_GROUP_CH, S5_STATE).transpose(0, 1, 3, 2)
        blk = t[:, :, :, None, :] * eye[None, :, None, :, None]
        return blk.reshape(S5_NCT, S5_STATES_CT, S5_CT)

    bblk = jnp.concatenate([in_blocks(bb_re), in_blocks(bb_im)], axis=2).astype(BF16)
    cblk = jnp.concatenate([out_blocks(c_re.astype(F32)),
                            out_blocks(-c_im.astype(F32))], axis=1).astype(BF16)
    a_re = ab_re.reshape(S5_NCT, S5_SLABS, LANES)
    a_im = ab_im.reshape(S5_NCT, S5_SLABS, LANES)
    return bblk, cblk, a_re, a_im


def _s5_layer(x, norm_g, w_in, lam_re, lam_im, log_dt, b_re, b_im, c_re, c_im, d_skip, w_out):
    bsz, seq, d = x.shape
    bblk, cblk, a_re, a_im = _s5_discretize(lam_re, lam_im, log_dt, b_re, b_im, c_re, c_im)
    xspec = pl.BlockSpec((bsz, S5_TC, d), lambda k: (0, k, 0))
    return pl.pallas_call(
        _s5_kernel,
        grid=(seq // S5_TC,),
        in_specs=[
            xspec,
            _const_spec((1, d)),
            _const_spec((d, d)),
            _const_spec(bblk.shape),
            _const_spec(cblk.shape),
            _const_spec(a_re.shape),
            _const_spec(a_im.shape),
            _const_spec((1, d)),
            _const_spec((d, 2 * d)),
        ],
        out_specs=xspec,
        out_shape=jax.ShapeDtypeStruct(x.shape, F32),
        scratch_shapes=[
            pltpu.VMEM((S5_NCT, bsz, S5_SLABS, LANES), F32),
            pltpu.VMEM((S5_NCT, bsz, S5_SLABS, LANES), F32),
            pltpu.VMEM((S5_SCAN_CT, bsz, S5_SLABS * S5_PITCH, LANES), F32),
            pltpu.VMEM((S5_SCAN_CT, bsz, S5_SLABS * S5_PITCH, LANES), F32),
            pltpu.VMEM((S5_SCAN_CT, bsz, S5_SLABS * S5_PITCH, LANES), F32),
            pltpu.VMEM((S5_SCAN_CT, bsz, S5_SLABS * S5_PITCH, LANES), F32),
        ],
        compiler_params=pltpu.CompilerParams(
            dimension_semantics=("arbitrary",), vmem_limit_bytes=VMEM_LIMIT),
        name="s5_layer",
    )(x, norm_g.reshape(1, d), w_in.astype(BF16), bblk, cblk, a_re, a_im,
      d_skip.reshape(1, d).astype(F32), w_out.astype(BF16))


def _gmlp_kernel(x_ref, g_ref, win_ref, vn_ref, ws_ref, bst_ref, wout_ref, o_ref):
    tm = x_ref.shape[0]
    x = x_ref[...]
    xn = _rms(x, g_ref[...]).astype(BF16)
    u = jax.nn.gelu(jnp.dot(xn, win_ref[:, :GM_HALF], preferred_element_type=F32))
    v = jax.nn.gelu(jnp.dot(xn, win_ref[:, GM_HALF:], preferred_element_type=F32))
    vb = _rms(v, vn_ref[...]).astype(BF16)
    row = lax.broadcasted_iota(jnp.int32, (GM_SPAN, GM_SPAN), 0) // CHUNK
    col = lax.broadcasted_iota(jnp.int32, (GM_SPAN, GM_SPAN), 1) // CHUNK
    causal = row >= col
    ws = [jnp.where(causal, ws_ref[g], 0.0).astype(BF16) for g in range(GM_GROUPS)]
    spans = []
    for s in range(tm // GM_SPAN):
        parts = []
        for g in range(GM_GROUPS):
            vblk = vb[s * GM_SPAN:(s + 1) * GM_SPAN, g * GM_GROUP_CH:(g + 1) * GM_GROUP_CH]
            parts.append(jnp.dot(ws[g], vblk, preferred_element_type=F32)
                         + bst_ref[:, g:g + 1])
        spans.append(jnp.concatenate(parts, axis=1))
    mixed = jnp.concatenate(spans, axis=0)
    p = (u * mixed).astype(BF16)
    o_ref[...] = x + jnp.dot(p, wout_ref[...], preferred_element_type=F32)


def _gmlp_layer(x, norm_g, w_in, v_norm, w_s, b_s, w_out, tm=256):
    bsz, seq, d = x.shape
    n_tok = bsz * seq
    xt = x.reshape(n_tok, d)
    xspec = pl.BlockSpec((tm, d), lambda i: (i, 0))
    out = pl.pallas_call(
        _gmlp_kernel,
        grid=(n_tok // tm,),
        in_specs=[
            xspec,
            _const_spec((1, d)),
            _const_spec((d, 2 * GM_HALF)),
            _const_spec((1, GM_HALF)),
            _const_spec((GM_GROUPS, GM_SPAN, GM_SPAN)),
            _const_spec((GM_SPAN, GM_GROUPS)),
            _const_spec((GM_HALF, d)),
        ],
        out_specs=xspec,
        out_shape=jax.ShapeDtypeStruct((n_tok, d), F32),
        compiler_params=pltpu.CompilerParams(
            dimension_semantics=("parallel",), vmem_limit_bytes=VMEM_LIMIT),
        name="gmlp_layer",
    )(xt, norm_g.reshape(1, d), w_in.astype(BF16), v_norm.reshape(1, GM_HALF),
      w_s.astype(F32), b_s.T.astype(F32), w_out.astype(BF16))
    return out.reshape(bsz, seq, d)


def _norm_proj_kernel(x_ref, g_ref, w_ref, o_ref):
    xn = _rms(x_ref[...], g_ref[...]).astype(BF16)
    o_ref[...] = jnp.dot(xn, w_ref[...], preferred_element_type=F32).astype(o_ref.dtype)


def _norm_proj(x, g, w, out_dtype, tm=512, tn=1024):
    m, d = x.shape
    n = w.shape[1]
    return pl.pallas_call(
        _norm_proj_kernel,
        grid=(m // tm, n // tn),
        in_specs=[
            pl.BlockSpec((tm, d), lambda i, j: (i, 0)),
            pl.BlockSpec((1, d), lambda i, j: (0, 0)),
            pl.BlockSpec((d, tn), lambda i, j: (0, j)),
        ],
        out_specs=pl.BlockSpec((tm, tn), lambda i, j: (i, j)),
        out_shape=jax.ShapeDtypeStruct((m, n), out_dtype),
        compiler_params=pltpu.CompilerParams(
            dimension_semantics=("parallel", "parallel"), vmem_limit_bytes=VMEM_LIMIT),
        name="norm_proj",
    )(x, g.reshape(1, d), w.astype(BF16))


def _route(x, g_ref, whi_ref, wlo_ref, bias_ref, run_ref):
    tm = x.shape[0]
    xn = _rms(x, g_ref[...])
    xhi = xn.astype(BF16)
    xlo = (xn - xhi.astype(F32)).astype(BF16)
    logits = (jnp.dot(xhi, whi_ref[...], preferred_element_type=F32)
              + jnp.dot(xhi, wlo_ref[...], preferred_element_type=F32)
              + jnp.dot(xlo, whi_ref[...], preferred_element_type=F32)) + bias_ref[...]
    lane = lax.broadcasted_iota(jnp.int32, (tm, LANES), 1)
    neg = jnp.float32(-jnp.inf)

    def first_argmax(vals):
        mx = jnp.max(vals, axis=-1, keepdims=True)
        idx = jnp.min(jnp.where(vals == mx, lane, LANES), axis=-1, keepdims=True)
        return mx, idx

    gl = jnp.where(lane < MOE_GROUPS, logits, neg)
    gmax, gidx = first_argmax(gl)
    w_g = 1.0 / jnp.sum(jnp.exp(gl - gmax), axis=-1, keepdims=True)
    lo = MOE_GROUPS + MOE_PER_GROUP * gidx
    el = jnp.where((lane >= lo) & (lane < lo + MOE_PER_GROUP), logits, neg)
    m1, i1 = first_argmax(el)
    m2, i2 = first_argmax(jnp.where(lane == i1, neg, el))
    e21 = jnp.exp(m2 - m1)
    w1 = w_g / (1.0 + e21)
    w2 = w_g * e21 / (1.0 + e21)
    e1 = i1 - MOE_GROUPS
    e2 = i2 - MOE_GROUPS

    onehot = ((lane == e1) | (lane == e2)).astype(BF16)
    r = lax.broadcasted_iota(jnp.int32, (tm, tm), 0)
    c = lax.broadcasted_iota(jnp.int32, (tm, tm), 1)
    before = (c < r).astype(BF16)
    tot = run_ref[...] + jnp.dot(before, onehot, preferred_element_type=F32)
    r1 = jnp.sum(jnp.where(lane == e1, tot, 0.0), axis=-1, keepdims=True)
    r2 = jnp.sum(jnp.where(lane == e2, tot, 0.0), axis=-1, keepdims=True)
    run_ref[...] = run_ref[...] + jnp.sum(onehot.astype(F32), axis=0, keepdims=True)

    meta = jnp.where(lane == 0, e1.astype(F32), jnp.where(lane == 1, e2.astype(F32),
                     jnp.where(lane == 2, r1, jnp.where(lane == 3, r2, 0.0))))
    meta_t = jnp.transpose(meta)[:SUBLANES, :].astype(jnp.int32)
    wts = jnp.where(lane == 0, w1, jnp.where(lane == 1, w2, 0.0))
    return meta_t, wts


def _xattn_router_kernel(x_ref, g_ref, wq_ref, kt_ref, v_ref, wo_ref, mg_ref, whi_ref,
                         wlo_ref, bias_ref, o_ref, meta_ref, wts_ref, cnt_ref, run_ref):
    @pl.when((pl.program_id(0) == 0) & (pl.program_id(1) == 0))
    def _():
        run_ref[...] = jnp.zeros_like(run_ref)

    x = x_ref[0]
    xn = _rms(x, g_ref[...]).astype(BF16)
    q = jnp.dot(xn, wq_ref[...], preferred_element_type=F32) * (XA_HEAD_DIM ** -0.5)
    q = q.astype(BF16)
    heads = []
    for h in range(XA_HEADS):
        cols = slice(h * XA_HEAD_DIM, (h + 1) * XA_HEAD_DIM)
        s = jnp.dot(q[:, cols], kt_ref[0, cols, :], preferred_element_type=F32)
        e = jnp.exp(s - jnp.max(s, axis=-1, keepdims=True))
        p = (e / jnp.sum(e, axis=-1, keepdims=True)).astype(BF16)
        heads.append(jnp.dot(p, v_ref[0, :, cols], preferred_element_type=F32))
    o = jnp.concatenate(heads, axis=1).astype(BF16)
    out = x + jnp.dot(o, wo_ref[...], preferred_element_type=F32)
    o_ref[0] = out
    meta, wts = _route(out, mg_ref, whi_ref, wlo_ref, bias_ref, run_ref)
    meta_ref[...] = meta
    wts_ref[...] = wts
    cnt_ref[...] = run_ref[...]


def _xattn_router_layer(x, mem, mem_g, norm_g, w_q, w_kv, w_o, moe_g, w_group, b_group,
                        w_expert, b_expert, tm=512):
    bsz, seq, d = x.shape
    m = mem.shape[1]
    nt = seq // tm
    kv = _norm_proj(mem.reshape(bsz * m, d), mem_g, w_kv, BF16).reshape(bsz, m, 2 * d)
    kt = kv[..., :d].transpose(0, 2, 1)
    v = kv[..., d:]
    nr = MOE_GROUPS + MOE_EXPERTS
    w_r = jnp.zeros((d, LANES), F32).at[:, :MOE_GROUPS].set(w_group.astype(F32))
    w_r = w_r.at[:, MOE_GROUPS:nr].set(w_expert.astype(F32))
    bias = jnp.zeros((1, LANES), F32).at[0, :MOE_GROUPS].set(b_group.astype(F32))
    bias = bias.at[0, MOE_GROUPS:nr].set(b_expert.astype(F32))
    w_hi = w_r.astype(BF16)
    w_lo = (w_r - w_hi.astype(F32)).astype(BF16)
    xspec = pl.BlockSpec((1, tm, d), lambda b, i: (b, i, 0))
    return pl.pallas_call(
        _xattn_router_kernel,
        grid=(bsz, nt),
        in_specs=[
            xspec,
            _const_spec((1, d)),
            _const_spec((d, d)),
            pl.BlockSpec((1, d, m), lambda b, i: (b, 0, 0)),
            pl.BlockSpec((1, m, d), lambda b, i: (b, 0, 0)),
            _const_spec((d, d)),
            _const_spec((1, d)),
            _const_spec((d, LANES)),
            _const_spec((d, LANES)),
            _const_spec((1, LANES)),
        ],
        out_specs=[
            xspec,
            pl.BlockSpec((SUBLANES, tm), lambda b, i: (0, b * nt + i)),
            pl.BlockSpec((tm, LANES), lambda b, i: (b * nt + i, 0)),
            pl.BlockSpec((1, LANES), lambda b, i: (0, 0)),
        ],
        out_shape=[
            jax.ShapeDtypeStruct(x.shape, F32),
            jax.ShapeDtypeStruct((SUBLANES, bsz * seq), jnp.int32),
            jax.ShapeDtypeStruct((bsz * seq, LANES), F32),
            jax.ShapeDtypeStruct((1, LANES), F32),
        ],
        scratch_shapes=[pltpu.VMEM((1, LANES), F32)],
        compiler_params=pltpu.CompilerParams(
            dimension_semantics=("arbitrary", "arbitrary"), vmem_limit_bytes=VMEM_LIMIT),
        name="xattn_router",
    )(x, norm_g.reshape(1, d), w_q.astype(BF16), kt, v, w_o.astype(BF16),
      moe_g.reshape(1, d), w_hi, w_lo, bias)


PACKED = D_MODEL // 2
U32 = jnp.uint32


def _pack_rows(v):
    lo = lax.bitcast_convert_type(v[:, :PACKED].astype(BF16).astype(F32), U32)
    hi = lax.bitcast_convert_type(v[:, PACKED:].astype(BF16).astype(F32), U32)
    return (hi & jnp.uint32(0xFFFF0000)) | (lo >> 16)


def _unpack_rows(p):
    lo = lax.bitcast_convert_type(p << 16, F32)
    hi = lax.bitcast_convert_type(p & jnp.uint32(0xFFFF0000), F32)
    return jnp.concatenate([lo, hi], axis=1)


def _row_copy(src, src_row, dst, dst_row, sem):
    return pltpu.make_async_copy(src.at[pl.ds(src_row, 1), :], dst.at[pl.ds(dst_row, 1), :], sem)


def _dispatch_kernel(ps_ref, pe_ref, nused_ref, pos_ref, x_ref, g_ref, xd_hbm, xbuf, zbuf,
                     sem, zsem):
    i = pl.program_id(0)
    n = pl.num_programs(0)
    tm = x_ref.shape[0]
    n_blocks = xd_hbm.shape[0] // MOE_BLK
    slot = i % 2

    def zero_block(row):
        rows = pl.ds(pl.multiple_of(row, MOE_BLK), MOE_BLK)
        return pltpu.make_async_copy(zbuf, xd_hbm.at[rows, :], zsem)

    @pl.when(i == 0)
    def _():
        zbuf[...] = jnp.zeros_like(zbuf)
        for e in range(MOE_EXPERTS):
            @pl.when(pe_ref[e] > ps_ref[e])
            def _():
                zero_block(pe_ref[e] - MOE_BLK).start()

        def start_unused(b, carry):
            zero_block(b * MOE_BLK).start()
            return carry
        lax.fori_loop(nused_ref[0], n_blocks, start_unused, 0)
        for e in range(MOE_EXPERTS):
            @pl.when(pe_ref[e] > ps_ref[e])
            def _():
                zero_block(0).wait()

        def wait_unused(b, carry):
            zero_block(0).wait()
            return carry
        lax.fori_loop(nused_ref[0], n_blocks, wait_unused, 0)

    def wait_slot(s):
        for _ in range(2):
            pltpu.make_async_copy(xbuf.at[s], xd_hbm.at[pl.ds(0, tm), :], sem.at[s]).wait()

    @pl.when(i >= 2)
    def _():
        wait_slot(slot)

    xbuf[slot] = _pack_rows(_rms(x_ref[...], g_ref[...]))

    def body(r, carry):
        _row_copy(xbuf.at[slot], r, xd_hbm, pos_ref[0, r], sem.at[slot]).start()
        _row_copy(xbuf.at[slot], r, xd_hbm, pos_ref[1, r], sem.at[slot]).start()
        return carry
    lax.fori_loop(0, tm, body, 0, unroll=8)

    @pl.when(i == n - 1)
    def _():
        wait_slot(1 - slot)
        wait_slot(slot)


def _expert_kernel(bexp_ref, nused_ref, xd_ref, wg_ref, wu_ref, wd_ref, y_ref, wgb, wub, wdb):
    b = pl.program_id(0)
    used = b < nused_ref[0]
    e = bexp_ref[b]
    e_prev = bexp_ref[jnp.maximum(b - 1, 0)]

    @pl.when(used & ((b == 0) | (e != e_prev)))
    def _():
        wgb[...] = wg_ref[...].astype(BF16)
        wub[...] = wu_ref[...].astype(BF16)
        wdb[...] = wd_ref[...].astype(BF16)

    @pl.when(used)
    def _():
        xb = _unpack_rows(xd_ref[...]).astype(BF16)
        a = jnp.dot(xb, wgb[...], preferred_element_type=F32)
        up = jnp.dot(xb, wub[...], preferred_element_type=F32)
        h = (a * (1.0 / (1.0 + jnp.exp(-a))) * up).astype(BF16)
        y_ref[...] = _pack_rows(jnp.dot(h, wdb[...], preferred_element_type=F32))

    @pl.when(jnp.logical_not(used))
    def _():
        y_ref[...] = jnp.zeros_like(y_ref)


def _pos_kernel(ps_ref, meta_ref, pos_ref):
    m = meta_ref[...]
    e = m[0:2, :]
    base = jnp.zeros_like(e)
    for k in range(MOE_EXPERTS):
        base = jnp.where(e == k, ps_ref[k], base)
    pos_ref[...] = base + m[2:4, :]


def _moe_positions(pad_start, meta, tm=2048):
    n_tok = meta.shape[1]
    return pl.pallas_call(
        _pos_kernel,
        grid_spec=pltpu.PrefetchScalarGridSpec(
            num_scalar_prefetch=1,
            grid=(n_tok // tm,),
            in_specs=[pl.BlockSpec((SUBLANES, tm), lambda i, ps: (0, i))],
            out_specs=pl.BlockSpec((2, tm), lambda i, ps: (0, i)),
        ),
        out_shape=jax.ShapeDtypeStruct((2, n_tok), jnp.int32),
        compiler_params=pltpu.CompilerParams(dimension_semantics=("parallel",)),
        name="moe_positions",
    )(pad_start, meta)


SC_CORES = 2
SC_SUBCORES = 16
SC_CHUNK = 64


def _sc_gather_rows(table, idx):
    n_rows = idx.shape[0]
    width = table.shape[1]
    workers = SC_CORES * SC_SUBCORES
    per_worker = n_rows // workers
    n_chunks = per_worker // SC_CHUNK
    mesh = plsc.VectorSubcoreMesh(core_axis_name="c", subcore_axis_name="s")

    def body(table_hbm, idx_hbm, out_hbm, idx_v, rows_v, sem):
        wid = lax.axis_index("s") * SC_CORES + lax.axis_index("c")
        base = wid * per_worker

        @pl.loop(0, n_chunks)
        def _(j):
            off = pl.multiple_of(base + j * SC_CHUNK, SC_CHUNK)
            pltpu.sync_copy(idx_hbm.at[pl.ds(off, SC_CHUNK)], idx_v)
            pltpu.async_copy(table_hbm.at[idx_v], rows_v, sem).wait()
            pltpu.sync_copy(rows_v, out_hbm.at[pl.ds(off, SC_CHUNK)])

    return pl.kernel(
        body,
        out_type=jax.ShapeDtypeStruct((n_rows, width), table.dtype),
        mesh=mesh,
        scratch_types=[pltpu.VMEM((SC_CHUNK,), jnp.int32),
                       pltpu.VMEM((SC_CHUNK, width), table.dtype),
                       pltpu.SemaphoreType.DMA],
        name="sc_gather_rows",
    )(table, idx)


def _combine_kernel(x_ref, wts_ref, g1_ref, g2_ref, fg_ref, o_ref, *, final):
    w = wts_ref[...]
    out = (x_ref[...] + w[:, 0:1] * _unpack_rows(g1_ref[...])
           + w[:, 1:2] * _unpack_rows(g2_ref[...]))
    if final:
        out = _rms(out, fg_ref[...])
    o_ref[...] = out


def _moe_layer(x, meta, wts, cnt, norm_g, layer, w_gate, w_up, w_down, final_g=None):
    bsz, seq, d = x.shape
    n_tok = bsz * seq
    xt = x.reshape(n_tok, d)

    counts = cnt[0, :MOE_EXPERTS].astype(jnp.int32)
    padded = ((counts + MOE_BLK - 1) // MOE_BLK) * MOE_BLK
    pad_end = jnp.cumsum(padded)
    pad_start = pad_end - padded
    n_blocks = (n_tok * 2) // MOE_BLK + MOE_EXPERTS
    n_pad = n_blocks * MOE_BLK
    blk_row = jnp.arange(n_blocks, dtype=jnp.int32) * MOE_BLK
    block_expert = jnp.minimum(jnp.sum(blk_row[:, None] >= pad_end[None, :], axis=1),
                               MOE_EXPERTS - 1).astype(jnp.int32)
    n_used = (pad_end[-1:] // MOE_BLK).astype(jnp.int32)

    pos = _moe_positions(pad_start, meta)

    smem_blk = functools.partial(pl.BlockSpec, memory_space=pltpu.SMEM)
    tm = DISPATCH_TM
    n_tiles = n_tok // tm
    xd = pl.pallas_call(
        _dispatch_kernel,
        grid_spec=pltpu.PrefetchScalarGridSpec(
            num_scalar_prefetch=3,
            grid=(n_tiles,),
            in_specs=[
                smem_blk((2, tm), lambda i, *_: (0, i)),
                pl.BlockSpec((tm, d), lambda i, *_: (i, 0)),
                pl.BlockSpec((1, d), lambda i, *_: (0, 0)),
            ],
            out_specs=pl.BlockSpec(memory_space=pl.ANY),
            scratch_shapes=[pltpu.VMEM((2, tm, PACKED), U32), pltpu.VMEM((MOE_BLK, PACKED), U32),
                            pltpu.SemaphoreType.DMA((2,)), pltpu.SemaphoreType.DMA(())],
        ),
        out_shape=jax.ShapeDtypeStruct((n_pad, PACKED), U32),
        compiler_params=pltpu.CompilerParams(
            dimension_semantics=("arbitrary",), vmem_limit_bytes=VMEM_LIMIT),
        name="moe_dispatch",
    )(pad_start, pad_end, n_used, pos, xt, norm_g.reshape(1, d))

    def used_blk(b, be, nu):
        return (jnp.maximum(jnp.minimum(b, nu[0] - 1), 0), 0)

    def expert_blk(b, be, nu):
        return (layer, be[b], 0, 0)

    y = pl.pallas_call(
        _expert_kernel,
        grid_spec=pltpu.PrefetchScalarGridSpec(
            num_scalar_prefetch=2,
            grid=(n_blocks,),
            in_specs=[
                pl.BlockSpec((MOE_BLK, PACKED), used_blk),
                pl.BlockSpec((None, None, d, MOE_HIDDEN), expert_blk),
                pl.BlockSpec((None, None, d, MOE_HIDDEN), expert_blk),
                pl.BlockSpec((None, None, MOE_HIDDEN, d), expert_blk),
            ],
            out_specs=pl.BlockSpec((MOE_BLK, PACKED), lambda b, be, nu: (b, 0)),
            scratch_shapes=[pltpu.VMEM((d, MOE_HIDDEN), BF16), pltpu.VMEM((d, MOE_HIDDEN), BF16),
                            pltpu.VMEM((MOE_HIDDEN, d), BF16)],
        ),
        out_shape=jax.ShapeDtypeStruct((n_pad, PACKED), U32),
        compiler_params=pltpu.CompilerParams(
            dimension_semantics=("arbitrary",), vmem_limit_bytes=VMEM_LIMIT),
        name="moe_experts",
    )(block_expert, n_used, xd, w_gate, w_up, w_down)

    gathered = _sc_gather_rows(lax.bitcast_convert_type(y, jnp.int32), pos.reshape(2 * n_tok))
    gathered = lax.bitcast_convert_type(gathered, U32)
    tm = COMBINE_TM
    n_tiles = n_tok // tm
    final = final_g is not None
    fg = (final_g if final else jnp.ones((d,), F32)).reshape(1, d)
    out = pl.pallas_call(
        functools.partial(_combine_kernel, final=final),
        grid=(n_tiles,),
        in_specs=[
            pl.BlockSpec((tm, d), lambda i: (i, 0)),
            pl.BlockSpec((tm, LANES), lambda i: (i, 0)),
            pl.BlockSpec((tm, PACKED), lambda i: (i, 0)),
            pl.BlockSpec((tm, PACKED), lambda i: (i + n_tiles, 0)),
            pl.BlockSpec((1, d), lambda i: (0, 0)),
        ],
        out_specs=pl.BlockSpec((tm, d), lambda i: (i, 0)),
        out_shape=jax.ShapeDtypeStruct((n_tok, d), F32),
        compiler_params=pltpu.CompilerParams(
            dimension_semantics=("parallel",), vmem_limit_bytes=VMEM_LIMIT),
        name="moe_combine",
    )(xt, wts, gathered, gathered, fg)
    return out.reshape(bsz, seq, d)


def kernel(x, mem, s5_norm, s5_w_in, s5_lambda_re, s5_lambda_im, s5_log_dt, s5_b_re, s5_b_im, s5_c_re, s5_c_im, s5_d, s5_w_out, gm_norm, gm_w_in, gm_v_norm, gm_w_s, gm_b_s, gm_w_out, mem_norm, xa_norm, xa_w_q, xa_w_kv, xa_w_o, moe_norm, moe_w_group, moe_b_group, moe_w_expert, moe_b_expert, moe_w_gate, moe_w_up, moe_w_down, final_norm):
    for i in range(DEPTH):
        j = i // 2
        if i % 2 == 0:
            x = _s5_layer(x, s5_norm[j], s5_w_in[j], s5_lambda_re[j], s5_lambda_im[j],
                          s5_log_dt[j], s5_b_re[j], s5_b_im[j], s5_c_re[j], s5_c_im[j],
                          s5_d[j], s5_w_out[j])
        else:
            x = _gmlp_layer(x, gm_norm[j], gm_w_in[j], gm_v_norm[j], gm_w_s[j], gm_b_s[j],
                            gm_w_out[j])
        x, meta, wts, cnt = _xattn_router_layer(
            x, mem, mem_norm, xa_norm[i], xa_w_q[i], xa_w_kv[i], xa_w_o[i], moe_norm[i],
            moe_w_group[i], moe_b_group[i], moe_w_expert[i], moe_b_expert[i])
        x = _moe_layer(x, meta, wts, cnt, moe_norm[i], i, moe_w_gate, moe_w_up, moe_w_down,
                       final_g=final_norm if i == DEPTH - 1 else None)
    return x
```

```python
import functools

import jax
import jax.numpy as jnp
from jax import lax
from jax.experimental import pallas as pl
from jax.experimental.pallas import tpu as pltpu
from jax.experimental.pallas import tpu_sc as plsc

F32 = jnp.float32
BF16 = jnp.bfloat16

D_MODEL = 1024
DEPTH = 2
CHUNK = 64
S5_GROUP_CH = 16
S5_GROUPS = 64
S5_STATE = 64
GM_HALF = 2 * D_MODEL
GM_GROUPS = 8
GM_SPAN = 128
GM_GROUP_CH = GM_HALF // GM_GROUPS
XA_HEADS = 4
XA_HEAD_DIM = D_MODEL // XA_HEADS
MOE_GROUPS = 4
MOE_PER_GROUP = 8
MOE_EXPERTS = MOE_GROUPS * MOE_PER_GROUP
MOE_HIDDEN = D_MODEL // 2
RMS_EPS = 1e-6

LANES = 128
SUBLANES = 8
VMEM_LIMIT = 56 * 1024 * 1024

S5_CT = 256
S5_NCT = D_MODEL // S5_CT
S5_STATES_CT = (S5_CT // S5_GROUP_CH) * S5_STATE
S5_SLABS = S5_STATES_CT // LANES
S5_TC = 128
S5_PITCH = S5_TC + 4
S5_SCAN_CT = 2

MOE_BLK = 256
DISPATCH_TM = 256
COMBINE_TM = 256


def _rms(x, g):
    ms = jnp.mean(x * x, axis=-1, keepdims=True)
    return x * lax.rsqrt(ms + RMS_EPS) * g


def _const_spec(shape):
    nd = len(shape)
    return pl.BlockSpec(shape, lambda *_: (0,) * nd, pipeline_mode=pl.Buffered(1))


def _s5_kernel(x_ref, g_ref, win_ref, bblk_ref, cblk_ref, are_ref, aim_ref, d_ref,
               wout_ref, o_ref, hre_ref, him_ref, bre_ref, bim_ref, sre_ref, sim_ref):
    nb = x_ref.shape[0]
    tc = x_ref.shape[1]

    @pl.when(pl.program_id(0) == 0)
    def _():
        hre_ref[...] = jnp.zeros_like(hre_ref)
        him_ref[...] = jnp.zeros_like(him_ref)

    x = x_ref[...].reshape(nb * tc, D_MODEL)
    xn = _rms(x, g_ref[...]).astype(BF16)
    u = jnp.dot(xn, win_ref[...], preferred_element_type=F32)

    y_parts = []
    for ct0 in range(0, S5_NCT, S5_SCAN_CT):
        tiles = range(ct0, ct0 + S5_SCAN_CT)
        for c, ct in enumerate(tiles):
            u_ct = u[:, ct * S5_CT:(ct + 1) * S5_CT]
            bu = jnp.dot(u_ct.astype(BF16), bblk_ref[ct], preferred_element_type=F32)
            for b in range(nb):
                for j in range(S5_SLABS):
                    rows = slice(b * tc, (b + 1) * tc)
                    bre_ref[c, b, pl.ds(j * S5_PITCH, tc), :] = bu[rows, j * LANES:(j + 1) * LANES]
                    bim_ref[c, b, pl.ds(j * S5_PITCH, tc), :] = bu[
                        rows, S5_STATES_CT + j * LANES:S5_STATES_CT + (j + 1) * LANES]
        chains = [(c, ct, b) for c, ct in enumerate(tiles) for b in range(nb)]
        a_re = [are_ref[ct] for ct in tiles]
        a_im = [aim_ref[ct] for ct in tiles]

        def step(t, carry):
            idx = pl.ds(t, S5_SLABS, stride=S5_PITCH)
            bu_t = [(bre_ref[c, b, idx, :], bim_ref[c, b, idx, :]) for c, _, b in chains]
            new = []
            for k, (c, _, b) in enumerate(chains):
                hr, hi = carry[2 * k], carry[2 * k + 1]
                new.append(a_re[c] * hr - a_im[c] * hi + bu_t[k][0])
                new.append(a_re[c] * hi + a_im[c] * hr + bu_t[k][1])
            for k, (c, _, b) in enumerate(chains):
                sre_ref[c, b, idx, :] = new[2 * k]
                sim_ref[c, b, idx, :] = new[2 * k + 1]
            return tuple(new)

        init = []
        for _, ct, b in chains:
            init += [hre_ref[ct, b], him_ref[ct, b]]
        fin = lax.fori_loop(0, tc, step, tuple(init), unroll=2)
        for k, (_, ct, b) in enumerate(chains):
            hre_ref[ct, b] = fin[2 * k]
            him_ref[ct, b] = fin[2 * k + 1]

        for c, ct in enumerate(tiles):
            xs = []
            for b in range(nb):
                cols = [sre_ref[c, b, pl.ds(j * S5_PITCH, tc), :] for j in range(S5_SLABS)]
                cols += [sim_ref[c, b, pl.ds(j * S5_PITCH, tc), :] for j in range(S5_SLABS)]
                xs.append(jnp.concatenate(cols, axis=1))
            xst = jnp.concatenate(xs, axis=0).astype(BF16)
            y_parts.append(jnp.dot(xst, cblk_ref[ct], preferred_element_type=F32))

    y = jnp.concatenate(y_parts, axis=1) + d_ref[...] * u
    yg = jax.nn.gelu(y).astype(BF16)
    z = jnp.dot(yg, wout_ref[...], preferred_element_type=F32)
    out = x + z[:, :D_MODEL] * (1.0 / (1.0 + jnp.exp(-z[:, D_MODEL:])))
    o_ref[...] = out.reshape(nb, tc, D_MODEL)


def _s5_discretize(lam_re, lam_im, log_dt, b_re, b_im, c_re, c_im):
    lr = lam_re.astype(F32)
    li = lam_im.astype(F32)
    dt = jnp.exp(log_dt.astype(F32))[:, None]
    mag = jnp.exp(lr * dt)
    ab_re = mag * jnp.cos(li * dt)
    ab_im = mag * jnp.sin(li * dt)
    den = lr * lr + li * li
    coef_re = ((ab_re - 1.0) * lr + ab_im * li) / den
    coef_im = (ab_im * lr - (ab_re - 1.0) * li) / den
    br = b_re.astype(F32)
    bi = b_im.astype(F32)
    bb_re = coef_re[..., None] * br - coef_im[..., None] * bi
    bb_im = coef_re[..., None] * bi + coef_im[..., None] * br
    gpt = S5_CT // S5_GROUP_CH
    eye = jnp.eye(gpt, dtype=F32)

    def in_blocks(bb):
        t = bb.reshape(S5_NCT, gpt, S5_STATE, S5_GROUP_CH).transpose(0, 1, 3, 2)
        blk = t[:, :, :, None, :] * eye[None, :, None, :, None]
        return blk.reshape(S5_NCT, S5_CT, S5_STATES_CT)

    def out_blocks(c):
        t = c.reshape(S5_NCT, gpt, S5_GROUP_CH, S5_STATE).transpose(0, 1, 3, 2)
        blk = t[:, :, :, None, :] * eye[None, :, None, :, None]
        return blk.reshape(S5_NCT, S5_STATES_CT, S5_CT)

    bblk = jnp.concatenate([in_blocks(bb_re), in_blocks(bb_im)], axis=2).astype(BF16)
    cblk = jnp.concatenate([out_blocks(c_re.astype(F32)),
                            out_blocks(-c_im.astype(F32))], axis=1).astype(BF16)
    a_re = ab_re.reshape(S5_NCT, S5_SLABS, LANES)
    a_im = ab_im.reshape(S5_NCT, S5_SLABS, LANES)
    return bblk, cblk, a_re, a_im


def _s5_layer(x, norm_g, w_in, lam_re, lam_im, log_dt, b_re, b_im, c_re, c_im, d_skip, w_out):
    bsz, seq, d = x.shape
    bblk, cblk, a_re, a_im = _s5_discretize(lam_re, lam_im, log_dt, b_re, b_im, c_re, c_im)
    xspec = pl.BlockSpec((bsz, S5_TC, d), lambda k: (0, k, 0))
    return pl.pallas_call(
        _s5_kernel,
        grid=(seq // S5_TC,),
        in_specs=[
            xspec,
            _const_spec((1, d)),
            _const_spec((d, d)),
            _const_spec(bblk.shape),
            _const_spec(cblk.shape),
            _const_spec(a_re.shape),
            _const_spec(a_im.shape),
            _const_spec((1, d)),
            _const_spec((d, 2 * d)),
        ],
        out_specs=xspec,
        out_shape=jax.ShapeDtypeStruct(x.shape, F32),
        scratch_shapes=[
            pltpu.VMEM((S5_NCT, bsz, S5_SLABS, LANES), F32),
            pltpu.VMEM((S5_NCT, bsz, S5_SLABS, LANES), F32),
            pltpu.VMEM((S5_SCAN_CT, bsz, S5_SLABS * S5_PITCH, LANES), F32),
            pltpu.VMEM((S5_SCAN_CT, bsz, S5_SLABS * S5_PITCH, LANES), F32),
            pltpu.VMEM((S5_SCAN_CT, bsz, S5_SLABS * S5_PITCH, LANES), F32),
            pltpu.VMEM((S5_SCAN_CT, bsz, S5_SLABS * S5_PITCH, LANES), F32),
        ],
        compiler_params=pltpu.CompilerParams(
            dimension_semantics=("arbitrary",), vmem_limit_bytes=VMEM_LIMIT),
        name="s5_layer",
    )(x, norm_g.reshape(1, d), w_in.astype(BF16), bblk, cblk, a_re, a_im,
      d_skip.reshape(1, d).astype(F32), w_out.astype(BF16))


def _gmlp_kernel(x_ref, g_ref, win_ref, vn_ref, ws_ref, bst_ref, wout_ref, o_ref):
    tm = x_ref.shape[0]
    x = x_ref[...]
    xn = _rms(x, g_ref[...]).astype(BF16)
    u = jax.nn.gelu(jnp.dot(xn, win_ref[:, :GM_HALF], preferred_element_type=F32))
    v = jax.nn.gelu(jnp.dot(xn, win_ref[:, GM_HALF:], preferred_element_type=F32))
    vb = _rms(v, vn_ref[...]).astype(BF16)
    row = lax.broadcasted_iota(jnp.int32, (GM_SPAN, GM_SPAN), 0) // CHUNK
    col = lax.broadcasted_iota(jnp.int32, (GM_SPAN, GM_SPAN), 1) // CHUNK
    causal = row >= col
    ws = [jnp.where(causal, ws_ref[g], 0.0).astype(BF16) for g in range(GM_GROUPS)]
    spans = []
    for s in range(tm // GM_SPAN):
        parts = []
        for g in range(GM_GROUPS):
            vblk = vb[s * GM_SPAN:(s + 1) * GM_SPAN, g * GM_GROUP_CH:(g + 1) * GM_GROUP_CH]
            parts.append(jnp.dot(ws[g], vblk, preferred_element_type=F32)
                         + bst_ref[:, g:g + 1])
        spans.append(jnp.concatenate(parts, axis=1))
    mixed = jnp.concatenate(spans, axis=0)
    p = (u * mixed).astype(BF16)
    o_ref[...] = x + jnp.dot(p, wout_ref[...], preferred_element_type=F32)


def _gmlp_layer(x, norm_g, w_in, v_norm, w_s, b_s, w_out, tm=256):
    bsz, seq, d = x.shape
    n_tok = bsz * seq
    xt = x.reshape(n_tok, d)
    xspec = pl.BlockSpec((tm, d), lambda i: (i, 0))
    out = pl.pallas_call(
        _gmlp_kernel,
        grid=(n_tok // tm,),
        in_specs=[
            xspec,
            _const_spec((1, d)),
            _const_spec((d, 2 * GM_HALF)),
            _const_spec((1, GM_HALF)),
            _const_spec((GM_GROUPS, GM_SPAN, GM_SPAN)),
            _const_spec((GM_SPAN, GM_GROUPS)),
            _const_spec((GM_HALF, d)),
        ],
        out_specs=xspec,
        out_shape=jax.ShapeDtypeStruct((n_tok, d), F32),
        compiler_params=pltpu.CompilerParams(
            dimension_semantics=("parallel",), vmem_limit_bytes=VMEM_LIMIT),
        name="gmlp_layer",
    )(xt, norm_g.reshape(1, d), w_in.astype(BF16), v_norm.reshape(1, GM_HALF),
      w_s.astype(F32), b_s.T.astype(F32), w_out.astype(BF16))
    return out.reshape(bsz, seq, d)


def _norm_proj_kernel(x_ref, g_ref, w_ref, o_ref):
    xn = _rms(x_ref[...], g_ref[...]).astype(BF16)
    o_ref[...] = jnp.dot(xn, w_ref[...], preferred_element_type=F32).astype(o_ref.dtype)


def _norm_proj(x, g, w, out_dtype, tm=512, tn=1024):
    m, d = x.shape
    n = w.shape[1]
    return pl.pallas_call(
        _norm_proj_kernel,
        grid=(m // tm, n // tn),
        in_specs=[
            pl.BlockSpec((tm, d), lambda i, j: (i, 0)),
            pl.BlockSpec((1, d), lambda i, j: (0, 0)),
            pl.BlockSpec((d, tn), lambda i, j: (0, j)),
        ],
        out_specs=pl.BlockSpec((tm, tn), lambda i, j: (i, j)),
        out_shape=jax.ShapeDtypeStruct((m, n), out_dtype),
        compiler_params=pltpu.CompilerParams(
            dimension_semantics=("parallel", "parallel"), vmem_limit_bytes=VMEM_LIMIT),
        name="norm_proj",
    )(x, g.reshape(1, d), w.astype(BF16))


def _route(x, g_ref, whi_ref, wlo_ref, bias_ref, run_ref):
    tm = x.shape[0]
    xn = _rms(x, g_ref[...])
    xhi = xn.astype(BF16)
    xlo = (xn - xhi.astype(F32)).astype(BF16)
    logits = (jnp.dot(xhi, whi_ref[...], preferred_element_type=F32)
              + jnp.dot(xhi, wlo_ref[...], preferred_element_type=F32)
              + jnp.dot(xlo, whi_ref[...], preferred_element_type=F32)) + bias_ref[...]
    lane = lax.broadcasted_iota(jnp.int32, (tm, LANES), 1)
    neg = jnp.float32(-jnp.inf)

    def first_argmax(vals):
        mx = jnp.max(vals, axis=-1, keepdims=True)
        idx = jnp.min(jnp.where(vals == mx, lane, LANES), axis=-1, keepdims=True)
        return mx, idx

    gl = jnp.where(lane < MOE_GROUPS, logits, neg)
    gmax, gidx = first_argmax(gl)
    w_g = 1.0 / jnp.sum(jnp.exp(gl - gmax), axis=-1, keepdims=True)
    lo = MOE_GROUPS + MOE_PER_GROUP * gidx
    el = jnp.where((lane >= lo) & (lane < lo + MOE_PER_GROUP), logits, neg)
    m1, i1 = first_argmax(el)
    m2, i2 = first_argmax(jnp.where(lane == i1, neg, el))
    e21 = jnp.exp(m2 - m1)
    w1 = w_g / (1.0 + e21)
    w2 = w_g * e21 / (1.0 + e21)
    e1 = i1 - MOE_GROUPS
    e2 = i2 - MOE_GROUPS

    onehot = ((lane == e1) | (lane == e2)).astype(BF16)
    r = lax.broadcasted_iota(jnp.int32, (tm, tm), 0)
    c = lax.broadcasted_iota(jnp.int32, (tm, tm), 1)
    before = (c < r).astype(BF16)
    tot = run_ref[...] + jnp.dot(before, onehot, preferred_element_type=F32)
    r1 = jnp.sum(jnp.where(lane == e1, tot, 0.0), axis=-1, keepdims=True)
    r2 = jnp.sum(jnp.where(lane == e2, tot, 0.0), axis=-1, keepdims=True)
    run_ref[...] = run_ref[...] + jnp.sum(onehot.astype(F32), axis=0, keepdims=True)

    meta = jnp.where(lane == 0, e1.astype(F32), jnp.where(lane == 1, e2.astype(F32),
                     jnp.where(lane == 2, r1, jnp.where(lane == 3, r2, 0.0))))
    meta_t = jnp.transpose(meta)[:SUBLANES, :].astype(jnp.int32)
    wts = jnp.where(lane == 0, w1, jnp.where(lane == 1, w2, 0.0))
    return meta_t, wts, xn


def _xattn_router_kernel(x_ref, g_ref, wq_ref, kt_ref, v_ref, wo_ref, mg_ref, whi_ref,
                         wlo_ref, bias_ref, o_ref, meta_ref, wts_ref, cnt_ref, xp_ref, run_ref):
    @pl.when((pl.program_id(0) == 0) & (pl.program_id(1) == 0))
    def _():
        run_ref[...] = jnp.zeros_like(run_ref)

    x = x_ref[0]
    xn = _rms(x, g_ref[...]).astype(BF16)
    q = jnp.dot(xn, wq_ref[...], preferred_element_type=F32) * (XA_HEAD_DIM ** -0.5)
    q = q.astype(BF16)
    heads = []
    for h in range(XA_HEADS):
        cols = slice(h * XA_HEAD_DIM, (h + 1) * XA_HEAD_DIM)
        s = jnp.dot(q[:, cols], kt_ref[0, cols, :], preferred_element_type=F32)
        e = jnp.exp(s - jnp.max(s, axis=-1, keepdims=True))
        p = (e / jnp.sum(e, axis=-1, keepdims=True)).astype(BF16)
        heads.append(jnp.dot(p, v_ref[0, :, cols], preferred_element_type=F32))
    o = jnp.concatenate(heads, axis=1).astype(BF16)
    out = x + jnp.dot(o, wo_ref[...], preferred_element_type=F32)
    o_ref[0] = out
    meta, wts, xn_moe = _route(out, mg_ref, whi_ref, wlo_ref, bias_ref, run_ref)
    meta_ref[...] = meta
    wts_ref[...] = wts
    cnt_ref[...] = run_ref[...]
    xp_ref[...] = _pack_rows(xn_moe)


def _xattn_router_layer(x, mem, mem_g, norm_g, w_q, w_kv, w_o, moe_g, w_group, b_group,
                        w_expert, b_expert, tm=512):
    bsz, seq, d = x.shape
    m = mem.shape[1]
    nt = seq // tm
    kv = _norm_proj(mem.reshape(bsz * m, d), mem_g, w_kv, BF16).reshape(bsz, m, 2 * d)
    kt = kv[..., :d].transpose(0, 2, 1)
    v = kv[..., d:]
    nr = MOE_GROUPS + MOE_EXPERTS
    w_r = jnp.zeros((d, LANES), F32).at[:, :MOE_GROUPS].set(w_group.astype(F32))
    w_r = w_r.at[:, MOE_GROUPS:nr].set(w_expert.astype(F32))
    bias = jnp.zeros((1, LANES), F32).at[0, :MOE_GROUPS].set(b_group.astype(F32))
    bias = bias.at[0, MOE_GROUPS:nr].set(b_expert.astype(F32))
    w_hi = w_r.astype(BF16)
    w_lo = (w_r - w_hi.astype(F32)).astype(BF16)
    xspec = pl.BlockSpec((1, tm, d), lambda b, i: (b, i, 0))
    return pl.pallas_call(
        _xattn_router_kernel,
        grid=(bsz, nt),
        in_specs=[
            xspec,
            _const_spec((1, d)),
            _const_spec((d, d)),
            pl.BlockSpec((1, d, m), lambda b, i: (b, 0, 0)),
            pl.BlockSpec((1, m, d), lambda b, i: (b, 0, 0)),
            _const_spec((d, d)),
            _const_spec((1, d)),
            _const_spec((d, LANES)),
            _const_spec((d, LANES)),
            _const_spec((1, LANES)),
        ],
        out_specs=[
            xspec,
            pl.BlockSpec((SUBLANES, tm), lambda b, i: (0, b * nt + i)),
            pl.BlockSpec((tm, LANES), lambda b, i: (b * nt + i, 0)),
            pl.BlockSpec((1, LANES), lambda b, i: (0, 0)),
            pl.BlockSpec((tm, d // 2), lambda b, i: (b * nt + i, 0)),
        ],
        out_shape=[
            jax.ShapeDtypeStruct(x.shape, F32),
            jax.ShapeDtypeStruct((SUBLANES, bsz * seq), jnp.int32),
            jax.ShapeDtypeStruct((bsz * seq, LANES), F32),
            jax.ShapeDtypeStruct((1, LANES), F32),
            jax.ShapeDtypeStruct((bsz * seq, d // 2), jnp.int32),
        ],
        scratch_shapes=[pltpu.VMEM((1, LANES), F32)],
        compiler_params=pltpu.CompilerParams(
            dimension_semantics=("arbitrary", "arbitrary"), vmem_limit_bytes=VMEM_LIMIT),
        name="xattn_router",
    )(x, norm_g.reshape(1, d), w_q.astype(BF16), kt, v, w_o.astype(BF16),
      moe_g.reshape(1, d), w_hi, w_lo, bias)


PACKED = D_MODEL // 2
I32 = jnp.int32
HI_HALF = -65536
LO_HALF = 65535


def _pack_rows(v):
    lo = lax.bitcast_convert_type(v[:, :PACKED].astype(BF16).astype(F32), I32)
    hi = lax.bitcast_convert_type(v[:, PACKED:].astype(BF16).astype(F32), I32)
    return (hi & HI_HALF) | ((lo >> 16) & LO_HALF)


def _unpack_rows(p):
    lo = lax.bitcast_convert_type(p << 16, F32)
    hi = lax.bitcast_convert_type(p & HI_HALF, F32)
    return jnp.concatenate([lo, hi], axis=1)


def _expert_kernel(bexp_ref, nused_ref, valid_ref, xd_ref, wg_ref, wu_ref, wd_ref, y_ref,
                   wgb, wub, wdb):
    b = pl.program_id(0)
    used = b < nused_ref[0]
    e = bexp_ref[b]
    e_prev = bexp_ref[jnp.maximum(b - 1, 0)]

    @pl.when(used & ((b == 0) | (e != e_prev)))
    def _():
        wgb[...] = wg_ref[...].astype(BF16)
        wub[...] = wu_ref[...].astype(BF16)
        wdb[...] = wd_ref[...].astype(BF16)

    @pl.when(used)
    def _():
        row = lax.broadcasted_iota(jnp.int32, xd_ref.shape, 0)
        xd = jnp.where(row < valid_ref[b], xd_ref[...], 0)
        xb = _unpack_rows(xd).astype(BF16)
        a = jnp.dot(xb, wgb[...], preferred_element_type=F32)
        up = jnp.dot(xb, wub[...], preferred_element_type=F32)
        h = (a * (1.0 / (1.0 + jnp.exp(-a))) * up).astype(BF16)
        y_ref[...] = _pack_rows(jnp.dot(h, wdb[...], preferred_element_type=F32))

    @pl.when(jnp.logical_not(used))
    def _():
        y_ref[...] = jnp.zeros_like(y_ref)


def _pos_kernel(ps_ref, meta_ref, pos_ref):
    m = meta_ref[...]
    e = m[0:2, :]
    base = jnp.zeros_like(e)
    for k in range(MOE_EXPERTS):
        base = jnp.where(e == k, ps_ref[k], base)
    pos_ref[...] = base + m[2:4, :]


def _moe_positions(pad_start, meta, tm=2048):
    n_tok = meta.shape[1]
    return pl.pallas_call(
        _pos_kernel,
        grid_spec=pltpu.PrefetchScalarGridSpec(
            num_scalar_prefetch=1,
            grid=(n_tok // tm,),
            in_specs=[pl.BlockSpec((SUBLANES, tm), lambda i, ps: (0, i))],
            out_specs=pl.BlockSpec((2, tm), lambda i, ps: (0, i)),
        ),
        out_shape=jax.ShapeDtypeStruct((2, n_tok), jnp.int32),
        compiler_params=pltpu.CompilerParams(dimension_semantics=("parallel",)),
        name="moe_positions",
    )(pad_start, meta)


SC_CORES = 2
SC_SUBCORES = 16
SC_CHUNK = 64


def _sc_gather_rows(table, idx):
    n_rows = idx.shape[0]
    width = table.shape[1]
    workers = SC_CORES * SC_SUBCORES
    per_worker = n_rows // workers
    n_chunks = per_worker // SC_CHUNK
    mesh = plsc.VectorSubcoreMesh(core_axis_name="c", subcore_axis_name="s")

    def body(table_hbm, idx_hbm, out_hbm, idx_v, rows_v, sem):
        wid = lax.axis_index("s") * SC_CORES + lax.axis_index("c")
        base = wid * per_worker

        @pl.loop(0, n_chunks)
        def _(j):
            off = pl.multiple_of(base + j * SC_CHUNK, SC_CHUNK)
            pltpu.sync_copy(idx_hbm.at[pl.ds(off, SC_CHUNK)], idx_v)
            pltpu.async_copy(table_hbm.at[idx_v], rows_v, sem).wait()
            pltpu.sync_copy(rows_v, out_hbm.at[pl.ds(off, SC_CHUNK)])

    return pl.kernel(
        body,
        out_type=jax.ShapeDtypeStruct((n_rows, width), table.dtype),
        mesh=mesh,
        scratch_types=[pltpu.VMEM((SC_CHUNK,), jnp.int32),
                       pltpu.VMEM((SC_CHUNK, width), table.dtype),
                       pltpu.SemaphoreType.DMA],
        name="sc_gather_rows",
    )(table, idx)


def _sc_scatter_rows(rows, idx, n_out):
    n_rows, width = rows.shape
    workers = SC_CORES * SC_SUBCORES
    per_worker = n_rows // workers
    n_chunks = per_worker // SC_CHUNK
    mesh = plsc.VectorSubcoreMesh(core_axis_name="c", subcore_axis_name="s")

    def body(rows_hbm, idx_hbm, out_hbm, idx_v, rows_v, sem):
        wid = lax.axis_index("s") * SC_CORES + lax.axis_index("c")
        base = wid * per_worker

        @pl.loop(0, n_chunks)
        def _(j):
            off = pl.multiple_of(base + j * SC_CHUNK, SC_CHUNK)
            pltpu.sync_copy(rows_hbm.at[pl.ds(off, SC_CHUNK)], rows_v)
            for k in range(2):
                koff = pl.multiple_of(k * n_rows + off, SC_CHUNK)
                pltpu.sync_copy(idx_hbm.at[pl.ds(koff, SC_CHUNK)], idx_v)
                pltpu.async_copy(rows_v, out_hbm.at[idx_v], sem).wait()

    return pl.kernel(
        body,
        out_type=jax.ShapeDtypeStruct((n_out, width), rows.dtype),
        mesh=mesh,
        scratch_types=[pltpu.VMEM((SC_CHUNK,), jnp.int32),
                       pltpu.VMEM((SC_CHUNK, width), rows.dtype),
                       pltpu.SemaphoreType.DMA],
        name="sc_scatter_rows",
    )(rows, idx)


def _combine_kernel(x_ref, wts_ref, g1_ref, g2_ref, fg_ref, o_ref, *, final):
    w = wts_ref[...]
    out = (x_ref[...] + w[:, 0:1] * _unpack_rows(g1_ref[...])
           + w[:, 1:2] * _unpack_rows(g2_ref[...]))
    if final:
        out = _rms(out, fg_ref[...])
    o_ref[...] = out


def _moe_layer(x, meta, wts, cnt, xp, layer, w_gate, w_up, w_down, final_g=None):
    bsz, seq, d = x.shape
    n_tok = bsz * seq
    xt = x.reshape(n_tok, d)

    counts = cnt[0, :MOE_EXPERTS].astype(jnp.int32)
    padded = ((counts + MOE_BLK - 1) // MOE_BLK) * MOE_BLK
    pad_end = jnp.cumsum(padded)
    pad_start = pad_end - padded
    n_blocks = (n_tok * 2) // MOE_BLK + MOE_EXPERTS
    n_pad = n_blocks * MOE_BLK
    blk_row = jnp.arange(n_blocks, dtype=jnp.int32) * MOE_BLK
    block_expert = jnp.minimum(jnp.sum(blk_row[:, None] >= pad_end[None, :], axis=1),
                               MOE_EXPERTS - 1).astype(jnp.int32)
    n_used = (pad_end[-1:] // MOE_BLK).astype(jnp.int32)
    seg_row = blk_row - pad_start[block_expert]
    valid = jnp.clip(counts[block_expert] - seg_row, 0, MOE_BLK).astype(jnp.int32)

    pos = _moe_positions(pad_start, meta).reshape(2 * n_tok)
    xd = _sc_scatter_rows(xp, pos, n_pad)

    def used_blk(b, be, nu, va):
        return (jnp.maximum(jnp.minimum(b, nu[0] - 1), 0), 0)

    def expert_blk(b, be, nu, va):
        return (layer, be[b], 0, 0)

    y = pl.pallas_call(
        _expert_kernel,
        grid_spec=pltpu.PrefetchScalarGridSpec(
            num_scalar_prefetch=3,
            grid=(n_blocks,),
            in_specs=[
                pl.BlockSpec((MOE_BLK, PACKED), used_blk),
                pl.BlockSpec((None, None, d, MOE_HIDDEN), expert_blk),
                pl.BlockSpec((None, None, d, MOE_HIDDEN), expert_blk),
                pl.BlockSpec((None, None, MOE_HIDDEN, d), expert_blk),
            ],
            out_specs=pl.BlockSpec((MOE_BLK, PACKED), lambda b, be, nu, va: (b, 0)),
            scratch_shapes=[pltpu.VMEM((d, MOE_HIDDEN), BF16), pltpu.VMEM((d, MOE_HIDDEN), BF16),
                            pltpu.VMEM((MOE_HIDDEN, d), BF16)],
        ),
        out_shape=jax.ShapeDtypeStruct((n_pad, PACKED), I32),
        compiler_params=pltpu.CompilerParams(
            dimension_semantics=("arbitrary",), vmem_limit_bytes=VMEM_LIMIT),
        name="moe_experts",
    )(block_expert, n_used, valid, xd, w_gate, w_up, w_down)

    gathered = _sc_gather_rows(y, pos)
    tm = COMBINE_TM
    n_tiles = n_tok // tm
    final = final_g is not None
    fg = (final_g if final else jnp.ones((d,), F32)).reshape(1, d)
    out = pl.pallas_call(
        functools.partial(_combine_kernel, final=final),
        grid=(n_tiles,),
        in_specs=[
            pl.BlockSpec((tm, d), lambda i: (i, 0)),
            pl.BlockSpec((tm, LANES), lambda i: (i, 0)),
            pl.BlockSpec((tm, PACKED), lambda i: (i, 0)),
            pl.BlockSpec((tm, PACKED), lambda i: (i + n_tiles, 0)),
            pl.BlockSpec((1, d), lambda i: (0, 0)),
        ],
        out_specs=pl.BlockSpec((tm, d), lambda i: (i, 0)),
        out_shape=jax.ShapeDtypeStruct((n_tok, d), F32),
        compiler_params=pltpu.CompilerParams(
            dimension_semantics=("parallel",), vmem_limit_bytes=VMEM_LIMIT),
        name="moe_combine",
    )(xt, wts, gathered, gathered, fg)
    return out.reshape(bsz, seq, d)


def kernel(x, mem, s5_norm, s5_w_in, s5_lambda_re, s5_lambda_im, s5_log_dt, s5_b_re, s5_b_im, s5_c_re, s5_c_im, s5_d, s5_w_out, gm_norm, gm_w_in, gm_v_norm, gm_w_s, gm_b_s, gm_w_out, mem_norm, xa_norm, xa_w_q, xa_w_kv, xa_w_o, moe_norm, moe_w_group, moe_b_group, moe_w_expert, moe_b_expert, moe_w_gate, moe_w_up, moe_w_down, final_norm):
    for i in range(DEPTH):
        j = i // 2
        if i % 2 == 0:
            x = _s5_layer(x, s5_norm[j], s5_w_in[j], s5_lambda_re[j], s5_lambda_im[j],
                          s5_log_dt[j], s5_b_re[j], s5_b_im[j], s5_c_re[j], s5_c_im[j],
                          s5_d[j], s5_w_out[j])
        else:
            x = _gmlp_layer(x, gm_norm[j], gm_w_in[j], gm_v_norm[j], gm_w_s[j], gm_b_s[j],
                            gm_w_out[j])
        x, meta, wts, cnt, xp = _xattn_router_layer(
            x, mem, mem_norm, xa_norm[i], xa_w_q[i], xa_w_kv[i], xa_w_o[i], moe_norm[i],
            moe_w_group[i], moe_b_group[i], moe_w_expert[i], moe_b_expert[i])
        x = _moe_layer(x, meta, wts, cnt, xp, i, moe_w_gate, moe_w_up, moe_w_down,
                       final_g=final_norm if i == DEPTH - 1 else None)
    return x
```

```python
import functools

import jax
import jax.numpy as jnp
from jax import lax
from jax.experimental import pallas as pl
from jax.experimental.pallas import tpu as pltpu
from jax.experimental.pallas import tpu_sc as plsc

F32 = jnp.float32
BF16 = jnp.bfloat16

D_MODEL = 1024
DEPTH = 2
CHUNK = 64
S5_GROUP_CH = 16
S5_GROUPS = 64
S5_STATE = 64
GM_HALF = 2 * D_MODEL
GM_GROUPS = 8
GM_SPAN = 128
GM_GROUP_CH = GM_HALF // GM_GROUPS
XA_HEADS = 4
XA_HEAD_DIM = D_MODEL // XA_HEADS
MOE_GROUPS = 4
MOE_PER_GROUP = 8
MOE_EXPERTS = MOE_GROUPS * MOE_PER_GROUP
MOE_HIDDEN = D_MODEL // 2
RMS_EPS = 1e-6

LANES = 128
SUBLANES = 8
VMEM_LIMIT = 56 * 1024 * 1024

S5_CT = 256
S5_NCT = D_MODEL // S5_CT
S5_STATES_CT = (S5_CT // S5_GROUP_CH) * S5_STATE
S5_SLABS = S5_STATES_CT // LANES
S5_TC = 128
S5_PITCH = S5_TC + 4
S5_SCAN_CT = 2

MOE_BLK = 256
COMBINE_TM = 512


def _rms(x, g):
    ms = jnp.mean(x * x, axis=-1, keepdims=True)
    return x * lax.rsqrt(ms + RMS_EPS) * g


def _const_spec(shape):
    nd = len(shape)
    return pl.BlockSpec(shape, lambda *_: (0,) * nd, pipeline_mode=pl.Buffered(1))


def _s5_kernel(x_ref, g_ref, win_ref, bblk_ref, cblk_ref, are_ref, aim_ref, d_ref,
               wout_ref, o_ref, hre_ref, him_ref, bre_ref, bim_ref, sre_ref, sim_ref):
    nb = x_ref.shape[0]
    tc = x_ref.shape[1]

    @pl.when(pl.program_id(0) == 0)
    def _():
        hre_ref[...] = jnp.zeros_like(hre_ref)
        him_ref[...] = jnp.zeros_like(him_ref)

    x = x_ref[...].reshape(nb * tc, D_MODEL)
    xn = _rms(x, g_ref[...]).astype(BF16)
    u = jnp.dot(xn, win_ref[...], preferred_element_type=F32)

    y_parts = []
    for ct0 in range(0, S5_NCT, S5_SCAN_CT):
        tiles = range(ct0, ct0 + S5_SCAN_CT)
        for c, ct in enumerate(tiles):
            u_ct = u[:, ct * S5_CT:(ct + 1) * S5_CT]
            bu = jnp.dot(u_ct.astype(BF16), bblk_ref[ct], preferred_element_type=F32)
            for b in range(nb):
                for j in range(S5_SLABS):
                    rows = slice(b * tc, (b + 1) * tc)
                    bre_ref[c, b, pl.ds(j * S5_PITCH, tc), :] = bu[rows, j * LANES:(j + 1) * LANES]
                    bim_ref[c, b, pl.ds(j * S5_PITCH, tc), :] = bu[
                        rows, S5_STATES_CT + j * LANES:S5_STATES_CT + (j + 1) * LANES]
        chains = [(c, ct, b) for c, ct in enumerate(tiles) for b in range(nb)]
        a_re = [are_ref[ct] for ct in tiles]
        a_im = [aim_ref[ct] for ct in tiles]

        def step(t, carry):
            idx = pl.ds(t, S5_SLABS, stride=S5_PITCH)
            bu_t = [(bre_ref[c, b, idx, :], bim_ref[c, b, idx, :]) for c, _, b in chains]
            new = []
            for k, (c, _, b) in enumerate(chains):
                hr, hi = carry[2 * k], carry[2 * k + 1]
                new.append(a_re[c] * hr - a_im[c] * hi + bu_t[k][0])
                new.append(a_re[c] * hi + a_im[c] * hr + bu_t[k][1])
            for k, (c, _, b) in enumerate(chains):
                sre_ref[c, b, idx, :] = new[2 * k]
                sim_ref[c, b, idx, :] = new[2 * k + 1]
            return tuple(new)

        init = []
        for _, ct, b in chains:
            init += [hre_ref[ct, b], him_ref[ct, b]]
        fin = lax.fori_loop(0, tc, step, tuple(init), unroll=2)
        for k, (_, ct, b) in enumerate(chains):
            hre_ref[ct, b] = fin[2 * k]
            him_ref[ct, b] = fin[2 * k + 1]

        for c, ct in enumerate(tiles):
            xs = []
            for b in range(nb):
                cols = [sre_ref[c, b, pl.ds(j * S5_PITCH, tc), :] for j in range(S5_SLABS)]
                cols += [sim_ref[c, b, pl.ds(j * S5_PITCH, tc), :] for j in range(S5_SLABS)]
                xs.append(jnp.concatenate(cols, axis=1))
            xst = jnp.concatenate(xs, axis=0).astype(BF16)
            y_parts.append(jnp.dot(xst, cblk_ref[ct], preferred_element_type=F32))

    y = jnp.concatenate(y_parts, axis=1) + d_ref[...] * u
    yg = jax.nn.gelu(y).astype(BF16)
    z = jnp.dot(yg, wout_ref[...], preferred_element_type=F32)
    out = x + z[:, :D_MODEL] * (1.0 / (1.0 + jnp.exp(-z[:, D_MODEL:])))
    o_ref[...] = out.reshape(nb, tc, D_MODEL)


def _s5_discretize(lam_re, lam_im, log_dt, b_re, b_im, c_re, c_im):
    lr = lam_re.astype(F32)
    li = lam_im.astype(F32)
    dt = jnp.exp(log_dt.astype(F32))[:, None]
    mag = jnp.exp(lr * dt)
    ab_re = mag * jnp.cos(li * dt)
    ab_im = mag * jnp.sin(li * dt)
    den = lr * lr + li * li
    coef_re = ((ab_re - 1.0) * lr + ab_im * li) / den
    coef_im = (ab_im * lr - (ab_re - 1.0) * li) / den
    br = b_re.astype(F32)
    bi = b_im.astype(F32)
    bb_re = coef_re[..., None] * br - coef_im[..., None] * bi
    bb_im = coef_re[..., None] * bi + coef_im[..., None] * br
    gpt = S5_CT // S5_GROUP_CH
    eye = jnp.eye(gpt, dtype=F32)

    def in_blocks(bb):
        t = bb.reshape(S5_NCT, gpt, S5_STATE, S5_GROUP_CH).transpose(0, 1, 3, 2)
        blk = t[:, :, :, None, :] * eye[None, :, None, :, None]
        return blk.reshape(S5_NCT, S5_CT, S5_STATES_CT)

    def out_blocks(c):
        t = c.reshape(S5_NCT, gpt, S5_GROUP_CH, S5_STATE).transpose(0, 1, 3, 2)
        blk = t[:, :, :, None, :] * eye[None, :, None, :, None]
        return blk.reshape(S5_NCT, S5_STATES_CT, S5_CT)

    bblk = jnp.concatenate([in_blocks(bb_re), in_blocks(bb_im)], axis=2).astype(BF16)
    cblk = jnp.concatenate([out_blocks(c_re.astype(F32)),
                            out_blocks(-c_im.astype(F32))], axis=1).astype(BF16)
    a_re = ab_re.reshape(S5_NCT, S5_SLABS, LANES)
    a_im = ab_im.reshape(S5_NCT, S5_SLABS, LANES)
    return bblk, cblk, a_re, a_im


def _s5_layer(x, norm_g, w_in, lam_re, lam_im, log_dt, b_re, b_im, c_re, c_im, d_skip, w_out):
    bsz, seq, d = x.shape
    bblk, cblk, a_re, a_im = _s5_discretize(lam_re, lam_im, log_dt, b_re, b_im, c_re, c_im)
    xspec = pl.BlockSpec((bsz, S5_TC, d), lambda k: (0, k, 0))
    return pl.pallas_call(
        _s5_kernel,
        grid=(seq // S5_TC,),
        in_specs=[
            xspec,
            _const_spec((1, d)),
            _const_spec((d, d)),
            _const_spec(bblk.shape),
            _const_spec(cblk.shape),
            _const_spec(a_re.shape),
            _const_spec(a_im.shape),
            _const_spec((1, d)),
            _const_spec((d, 2 * d)),
        ],
        out_specs=xspec,
        out_shape=jax.ShapeDtypeStruct(x.shape, F32),
        scratch_shapes=[
            pltpu.VMEM((S5_NCT, bsz, S5_SLABS, LANES), F32),
            pltpu.VMEM((S5_NCT, bsz, S5_SLABS, LANES), F32),
            pltpu.VMEM((S5_SCAN_CT, bsz, S5_SLABS * S5_PITCH, LANES), F32),
            pltpu.VMEM((S5_SCAN_CT, bsz, S5_SLABS * S5_PITCH, LANES), F32),
            pltpu.VMEM((S5_SCAN_CT, bsz, S5_SLABS * S5_PITCH, LANES), F32),
            pltpu.VMEM((S5_SCAN_CT, bsz, S5_SLABS * S5_PITCH, LANES), F32),
        ],
        compiler_params=pltpu.CompilerParams(
            dimension_semantics=("arbitrary",), vmem_limit_bytes=VMEM_LIMIT),
        name="s5_layer",
    )(x, norm_g.reshape(1, d), w_in.astype(BF16), bblk, cblk, a_re, a_im,
      d_skip.reshape(1, d).astype(F32), w_out.astype(BF16))


def _gmlp_kernel(x_ref, wts_ref, g1_ref, g2_ref, g_ref, win_ref, vn_ref, ws_ref, bst_ref,
                 wout_ref, o_ref):
    tm = x_ref.shape[0]
    w = wts_ref[...]
    x = (x_ref[...] + w[:, 0:1] * _unpack_rows(g1_ref[...])
         + w[:, 1:2] * _unpack_rows(g2_ref[...]))
    xn = _rms(x, g_ref[...]).astype(BF16)
    u = jax.nn.gelu(jnp.dot(xn, win_ref[:, :GM_HALF], preferred_element_type=F32))
    v = jax.nn.gelu(jnp.dot(xn, win_ref[:, GM_HALF:], preferred_element_type=F32))
    vb = _rms(v, vn_ref[...]).astype(BF16)
    row = lax.broadcasted_iota(jnp.int32, (GM_SPAN, GM_SPAN), 0) // CHUNK
    col = lax.broadcasted_iota(jnp.int32, (GM_SPAN, GM_SPAN), 1) // CHUNK
    causal = row >= col
    ws = [jnp.where(causal, ws_ref[g], 0.0).astype(BF16) for g in range(GM_GROUPS)]
    spans = []
    for s in range(tm // GM_SPAN):
        parts = []
        for g in range(GM_GROUPS):
            vblk = vb[s * GM_SPAN:(s + 1) * GM_SPAN, g * GM_GROUP_CH:(g + 1) * GM_GROUP_CH]
            parts.append(jnp.dot(ws[g], vblk, preferred_element_type=F32)
                         + bst_ref[:, g:g + 1])
        spans.append(jnp.concatenate(parts, axis=1))
    mixed = jnp.concatenate(spans, axis=0)
    p = (u * mixed).astype(BF16)
    o_ref[...] = x + jnp.dot(p, wout_ref[...], preferred_element_type=F32)


def _gmlp_layer(x, moe_out, norm_g, w_in, v_norm, w_s, b_s, w_out, tm=256):
    bsz, seq, d = x.shape
    n_tok = bsz * seq
    xt = x.reshape(n_tok, d)
    wts, gathered = moe_out
    n_tiles = n_tok // tm
    xspec = pl.BlockSpec((tm, d), lambda i: (i, 0))
    out = pl.pallas_call(
        _gmlp_kernel,
        grid=(n_tiles,),
        in_specs=[
            xspec,
            pl.BlockSpec((tm, LANES), lambda i: (i, 0)),
            pl.BlockSpec((tm, d // 2), lambda i: (i, 0)),
            pl.BlockSpec((tm, d // 2), lambda i: (i + n_tiles, 0)),
            _const_spec((1, d)),
            _const_spec((d, 2 * GM_HALF)),
            _const_spec((1, GM_HALF)),
            _const_spec((GM_GROUPS, GM_SPAN, GM_SPAN)),
            _const_spec((GM_SPAN, GM_GROUPS)),
            _const_spec((GM_HALF, d)),
        ],
        out_specs=xspec,
        out_shape=jax.ShapeDtypeStruct((n_tok, d), F32),
        compiler_params=pltpu.CompilerParams(
            dimension_semantics=("parallel",), vmem_limit_bytes=VMEM_LIMIT),
        name="gmlp_layer",
    )(xt, wts, gathered, gathered, norm_g.reshape(1, d), w_in.astype(BF16),
      v_norm.reshape(1, GM_HALF),
      w_s.astype(F32), b_s.T.astype(F32), w_out.astype(BF16))
    return out.reshape(bsz, seq, d)


def _norm_proj_kernel(x_ref, g_ref, w_ref, o_ref):
    xn = _rms(x_ref[...], g_ref[...]).astype(BF16)
    o_ref[...] = jnp.dot(xn, w_ref[...], preferred_element_type=F32).astype(o_ref.dtype)


def _norm_proj(x, g, w, out_dtype, tm=512, tn=1024):
    m, d = x.shape
    n = w.shape[1]
    return pl.pallas_call(
        _norm_proj_kernel,
        grid=(m // tm, n // tn),
        in_specs=[
            pl.BlockSpec((tm, d), lambda i, j: (i, 0)),
            pl.BlockSpec((1, d), lambda i, j: (0, 0)),
            pl.BlockSpec((d, tn), lambda i, j: (0, j)),
        ],
        out_specs=pl.BlockSpec((tm, tn), lambda i, j: (i, j)),
        out_shape=jax.ShapeDtypeStruct((m, n), out_dtype),
        compiler_params=pltpu.CompilerParams(
            dimension_semantics=("parallel", "parallel"), vmem_limit_bytes=VMEM_LIMIT),
        name="norm_proj",
    )(x, g.reshape(1, d), w.astype(BF16))


def _route(x, g_ref, whi_ref, wlo_ref, bias_ref, run_ref):
    tm = x.shape[0]
    xn = _rms(x, g_ref[...])
    xhi = xn.astype(BF16)
    xlo = (xn - xhi.astype(F32)).astype(BF16)
    logits = (jnp.dot(xhi, whi_ref[...], preferred_element_type=F32)
              + jnp.dot(xhi, wlo_ref[...], preferred_element_type=F32)
              + jnp.dot(xlo, whi_ref[...], preferred_element_type=F32)) + bias_ref[...]
    lane = lax.broadcasted_iota(jnp.int32, (tm, LANES), 1)
    neg = jnp.float32(-jnp.inf)

    def first_argmax(vals):
        mx = jnp.max(vals, axis=-1, keepdims=True)
        idx = jnp.min(jnp.where(vals == mx, lane, LANES), axis=-1, keepdims=True)
        return mx, idx

    gl = jnp.where(lane < MOE_GROUPS, logits, neg)
    gmax, gidx = first_argmax(gl)
    w_g = 1.0 / jnp.sum(jnp.exp(gl - gmax), axis=-1, keepdims=True)
    lo = MOE_GROUPS + MOE_PER_GROUP * gidx
    el = jnp.where((lane >= lo) & (lane < lo + MOE_PER_GROUP), logits, neg)
    m1, i1 = first_argmax(el)
    m2, i2 = first_argmax(jnp.where(lane == i1, neg, el))
    e21 = jnp.exp(m2 - m1)
    w1 = w_g / (1.0 + e21)
    w2 = w_g * e21 / (1.0 + e21)
    e1 = i1 - MOE_GROUPS
    e2 = i2 - MOE_GROUPS

    onehot = ((lane == e1) | (lane == e2)).astype(BF16)
    r = lax.broadcasted_iota(jnp.int32, (tm, tm), 0)
    c = lax.broadcasted_iota(jnp.int32, (tm, tm), 1)
    before = (c < r).astype(BF16)
    tot = run_ref[...] + jnp.dot(before, onehot, preferred_element_type=F32)
    r1 = jnp.sum(jnp.where(lane == e1, tot, 0.0), axis=-1, keepdims=True)
    r2 = jnp.sum(jnp.where(lane == e2, tot, 0.0), axis=-1, keepdims=True)
    run_ref[...] = run_ref[...] + jnp.sum(onehot.astype(F32), axis=0, keepdims=True)

    meta = jnp.where(lane == 0, e1.astype(F32), jnp.where(lane == 1, e2.astype(F32),
                     jnp.where(lane == 2, r1, jnp.where(lane == 3, r2, 0.0))))
    meta_t = jnp.transpose(meta)[:SUBLANES, :].astype(jnp.int32)
    wts = jnp.where(lane == 0, w1, jnp.where(lane == 1, w2, 0.0))
    return meta_t, wts, xn


def _xattn_router_kernel(x_ref, g_ref, wq_ref, kt_ref, v_ref, wo_ref, mg_ref, whi_ref,
                         wlo_ref, bias_ref, o_ref, meta_ref, wts_ref, cnt_ref, xp_ref, run_ref):
    @pl.when((pl.program_id(0) == 0) & (pl.program_id(1) == 0))
    def _():
        run_ref[...] = jnp.zeros_like(run_ref)

    x = x_ref[0]
    xn = _rms(x, g_ref[...]).astype(BF16)
    q = jnp.dot(xn, wq_ref[...], preferred_element_type=F32) * (XA_HEAD_DIM ** -0.5)
    q = q.astype(BF16)
    heads = []
    for h in range(XA_HEADS):
        cols = slice(h * XA_HEAD_DIM, (h + 1) * XA_HEAD_DIM)
        s = jnp.dot(q[:, cols], kt_ref[0, cols, :], preferred_element_type=F32)
        e = jnp.exp(s - jnp.max(s, axis=-1, keepdims=True))
        p = (e / jnp.sum(e, axis=-1, keepdims=True)).astype(BF16)
        heads.append(jnp.dot(p, v_ref[0, :, cols], preferred_element_type=F32))
    o = jnp.concatenate(heads, axis=1).astype(BF16)
    out = x + jnp.dot(o, wo_ref[...], preferred_element_type=F32)
    o_ref[0] = out
    meta, wts, xn_moe = _route(out, mg_ref, whi_ref, wlo_ref, bias_ref, run_ref)
    meta_ref[...] = meta
    wts_ref[...] = wts
    cnt_ref[...] = run_ref[...]
    xp_ref[...] = _pack_rows(xn_moe)


def _xattn_router_layer(x, mem, mem_g, norm_g, w_q, w_kv, w_o, moe_g, w_group, b_group,
                        w_expert, b_expert, tm=512):
    bsz, seq, d = x.shape
    m = mem.shape[1]
    nt = seq // tm
    kv = _norm_proj(mem.reshape(bsz * m, d), mem_g, w_kv, BF16).reshape(bsz, m, 2 * d)
    kt = kv[..., :d].transpose(0, 2, 1)
    v = kv[..., d:]
    nr = MOE_GROUPS + MOE_EXPERTS
    w_r = jnp.zeros((d, LANES), F32).at[:, :MOE_GROUPS].set(w_group.astype(F32))
    w_r = w_r.at[:, MOE_GROUPS:nr].set(w_expert.astype(F32))
    bias = jnp.zeros((1, LANES), F32).at[0, :MOE_GROUPS].set(b_group.astype(F32))
    bias = bias.at[0, MOE_GROUPS:nr].set(b_expert.astype(F32))
    w_hi = w_r.astype(BF16)
    w_lo = (w_r - w_hi.astype(F32)).astype(BF16)
    xspec = pl.BlockSpec((1, tm, d), lambda b, i: (b, i, 0))
    return pl.pallas_call(
        _xattn_router_kernel,
        grid=(bsz, nt),
        in_specs=[
            xspec,
            _const_spec((1, d)),
            _const_spec((d, d)),
            pl.BlockSpec((1, d, m), lambda b, i: (b, 0, 0)),
            pl.BlockSpec((1, m, d), lambda b, i: (b, 0, 0)),
            _const_spec((d, d)),
            _const_spec((1, d)),
            _const_spec((d, LANES)),
            _const_spec((d, LANES)),
            _const_spec((1, LANES)),
        ],
        out_specs=[
            xspec,
            pl.BlockSpec((SUBLANES, tm), lambda b, i: (0, b * nt + i)),
            pl.BlockSpec((tm, LANES), lambda b, i: (b * nt + i, 0)),
            pl.BlockSpec((1, LANES), lambda b, i: (0, 0)),
            pl.BlockSpec((tm, d // 2), lambda b, i: (b * nt + i, 0)),
        ],
        out_shape=[
            jax.ShapeDtypeStruct(x.shape, F32),
            jax.ShapeDtypeStruct((SUBLANES, bsz * seq), jnp.int32),
            jax.ShapeDtypeStruct((bsz * seq, LANES), F32),
            jax.ShapeDtypeStruct((1, LANES), F32),
            jax.ShapeDtypeStruct((bsz * seq, d // 2), jnp.int32),
        ],
        scratch_shapes=[pltpu.VMEM((1, LANES), F32)],
        compiler_params=pltpu.CompilerParams(
            dimension_semantics=("arbitrary", "arbitrary"), vmem_limit_bytes=VMEM_LIMIT),
        name="xattn_router",
    )(x, norm_g.reshape(1, d), w_q.astype(BF16), kt, v, w_o.astype(BF16),
      moe_g.reshape(1, d), w_hi, w_lo, bias)


PACKED = D_MODEL // 2
I32 = jnp.int32
HI_HALF = -65536
LO_HALF = 65535


def _pack_rows(v):
    lo = lax.bitcast_convert_type(v[:, :PACKED].astype(BF16).astype(F32), I32)
    hi = lax.bitcast_convert_type(v[:, PACKED:].astype(BF16).astype(F32), I32)
    return (hi & HI_HALF) | ((lo >> 16) & LO_HALF)


def _unpack_rows(p):
    lo = lax.bitcast_convert_type(p << 16, F32)
    hi = lax.bitcast_convert_type(p & HI_HALF, F32)
    return jnp.concatenate([lo, hi], axis=1)


def _expert_kernel(bstart_ref, nblk_ref, nused_ref, valid_ref, xd_hbm, wg_ref, wu_ref, wd_ref,
                   y_hbm, wgb, wub, wdb, xbuf, ybuf, lsem, ssem):
    e = pl.program_id(0)
    n_used = nused_ref[0]
    b0 = bstart_ref[e]
    nb = nblk_ref[e]

    def rows(b):
        return pl.ds(pl.multiple_of(b * MOE_BLK, MOE_BLK), MOE_BLK)

    def load(b, slot):
        return pltpu.make_async_copy(xd_hbm.at[rows(b), :], xbuf.at[slot], lsem.at[slot])

    def store(b, slot):
        return pltpu.make_async_copy(ybuf.at[slot], y_hbm.at[rows(b), :], ssem.at[slot])

    @pl.when((e == 0) & (n_used > 0))
    def _():
        load(0, 0).start()

    @pl.when(nb > 0)
    def _():
        wgb[...] = wg_ref[...].astype(BF16)
        wub[...] = wu_ref[...].astype(BF16)
        wdb[...] = wd_ref[...].astype(BF16)

    def block(k, carry):
        b = b0 + k
        slot = b % 2
        load(b, slot).wait()

        @pl.when(b + 1 < n_used)
        def _():
            load(b + 1, 1 - slot).start()

        @pl.when(b >= 2)
        def _():
            store(b - 2, slot).wait()

        row = lax.broadcasted_iota(jnp.int32, (MOE_BLK, PACKED), 0)
        xd = jnp.where(row < valid_ref[b], xbuf[slot], 0)
        xb = _unpack_rows(xd).astype(BF16)
        a = jnp.dot(xb, wgb[...], preferred_element_type=F32)
        up = jnp.dot(xb, wub[...], preferred_element_type=F32)
        h = (a * (1.0 / (1.0 + jnp.exp(-a))) * up).astype(BF16)
        ybuf[slot] = _pack_rows(jnp.dot(h, wdb[...], preferred_element_type=F32))
        store(b, slot).start()
        return carry

    lax.fori_loop(0, nb, block, 0)

    @pl.when(e == pl.num_programs(0) - 1)
    def _():
        for back in (1, 2):
            @pl.when(n_used >= back)
            def _():
                last = n_used - back
                store(last, last % 2).wait()


def _moe_experts(xd, blk_start, blk_count, n_used, valid, layer, w_gate, w_up, w_down):
    n_pad = xd.shape[0]
    d = D_MODEL

    def expert_blk(e, *_):
        return (layer, e, 0, 0)

    return pl.pallas_call(
        _expert_kernel,
        grid_spec=pltpu.PrefetchScalarGridSpec(
            num_scalar_prefetch=4,
            grid=(MOE_EXPERTS,),
            in_specs=[
                pl.BlockSpec(memory_space=pl.ANY),
                pl.BlockSpec((None, None, d, MOE_HIDDEN), expert_blk),
                pl.BlockSpec((None, None, d, MOE_HIDDEN), expert_blk),
                pl.BlockSpec((None, None, MOE_HIDDEN, d), expert_blk),
            ],
            out_specs=pl.BlockSpec(memory_space=pl.ANY),
            scratch_shapes=[pltpu.VMEM((d, MOE_HIDDEN), BF16), pltpu.VMEM((d, MOE_HIDDEN), BF16),
                            pltpu.VMEM((MOE_HIDDEN, d), BF16),
                            pltpu.VMEM((2, MOE_BLK, PACKED), I32),
                            pltpu.VMEM((2, MOE_BLK, PACKED), I32),
                            pltpu.SemaphoreType.DMA((2,)), pltpu.SemaphoreType.DMA((2,))],
        ),
        out_shape=jax.ShapeDtypeStruct((n_pad, PACKED), I32),
        compiler_params=pltpu.CompilerParams(
            dimension_semantics=("arbitrary",), vmem_limit_bytes=VMEM_LIMIT),
        name="moe_experts",
    )(blk_start.astype(jnp.int32), blk_count.astype(jnp.int32), n_used, valid, xd,
      w_gate, w_up, w_down)


def _pos_kernel(ps_ref, meta_ref, pos_ref):
    m = meta_ref[...]
    e = m[0:2, :]
    base = jnp.zeros_like(e)
    for k in range(MOE_EXPERTS):
        base = jnp.where(e == k, ps_ref[k], base)
    pos_ref[...] = base + m[2:4, :]


def _moe_positions(pad_start, meta, tm=2048):
    n_tok = meta.shape[1]
    return pl.pallas_call(
        _pos_kernel,
        grid_spec=pltpu.PrefetchScalarGridSpec(
            num_scalar_prefetch=1,
            grid=(n_tok // tm,),
            in_specs=[pl.BlockSpec((SUBLANES, tm), lambda i, ps: (0, i))],
            out_specs=pl.BlockSpec((2, tm), lambda i, ps: (0, i)),
        ),
        out_shape=jax.ShapeDtypeStruct((2, n_tok), jnp.int32),
        compiler_params=pltpu.CompilerParams(dimension_semantics=("parallel",)),
        name="moe_positions",
    )(pad_start, meta)


SC_CORES = 2
SC_SUBCORES = 16
SC_CHUNK = 64


def _sc_gather_rows(table, idx):
    n_rows = idx.shape[0]
    width = table.shape[1]
    workers = SC_CORES * SC_SUBCORES
    per_worker = n_rows // workers
    n_chunks = per_worker // SC_CHUNK
    mesh = plsc.VectorSubcoreMesh(core_axis_name="c", subcore_axis_name="s")

    def body(table_hbm, idx_hbm, out_hbm, idx_v, rows_v, sem):
        wid = lax.axis_index("s") * SC_CORES + lax.axis_index("c")
        base = wid * per_worker

        @pl.loop(0, n_chunks)
        def _(j):
            off = pl.multiple_of(base + j * SC_CHUNK, SC_CHUNK)
            pltpu.sync_copy(idx_hbm.at[pl.ds(off, SC_CHUNK)], idx_v)
            pltpu.async_copy(table_hbm.at[idx_v], rows_v, sem).wait()
            pltpu.sync_copy(rows_v, out_hbm.at[pl.ds(off, SC_CHUNK)])

    return pl.kernel(
        body,
        out_type=jax.ShapeDtypeStruct((n_rows, width), table.dtype),
        mesh=mesh,
        scratch_types=[pltpu.VMEM((SC_CHUNK,), jnp.int32),
                       pltpu.VMEM((SC_CHUNK, width), table.dtype),
                       pltpu.SemaphoreType.DMA],
        name="sc_gather_rows",
    )(table, idx)


def _sc_scatter_rows(rows, idx, n_out):
    n_rows, width = rows.shape
    workers = SC_CORES * SC_SUBCORES
    per_worker = n_rows // workers
    n_chunks = per_worker // SC_CHUNK
    mesh = plsc.VectorSubcoreMesh(core_axis_name="c", subcore_axis_name="s")

    def body(rows_hbm, idx_hbm, out_hbm, idx_v, rows_v, sem):
        wid = lax.axis_index("s") * SC_CORES + lax.axis_index("c")
        base = wid * per_worker

        @pl.loop(0, n_chunks)
        def _(j):
            off = pl.multiple_of(base + j * SC_CHUNK, SC_CHUNK)
            pltpu.sync_copy(rows_hbm.at[pl.ds(off, SC_CHUNK)], rows_v)
            for k in range(2):
                koff = pl.multiple_of(k * n_rows + off, SC_CHUNK)
                pltpu.sync_copy(idx_hbm.at[pl.ds(koff, SC_CHUNK)], idx_v)
                pltpu.async_copy(rows_v, out_hbm.at[idx_v], sem).wait()

    return pl.kernel(
        body,
        out_type=jax.ShapeDtypeStruct((n_out, width), rows.dtype),
        mesh=mesh,
        scratch_types=[pltpu.VMEM((SC_CHUNK,), jnp.int32),
                       pltpu.VMEM((SC_CHUNK, width), rows.dtype),
                       pltpu.SemaphoreType.DMA],
        name="sc_scatter_rows",
    )(rows, idx)


def _combine_kernel(x_ref, wts_ref, g1_ref, g2_ref, fg_ref, o_ref, *, final):
    w = wts_ref[...]
    out = (x_ref[...] + w[:, 0:1] * _unpack_rows(g1_ref[...])
           + w[:, 1:2] * _unpack_rows(g2_ref[...]))
    if final:
        out = _rms(out, fg_ref[...])
    o_ref[...] = out


def _moe_layer(x, meta, cnt, xp, layer, w_gate, w_up, w_down):
    n_tok = x.shape[0] * x.shape[1]

    counts = cnt[0, :MOE_EXPERTS].astype(jnp.int32)
    padded = ((counts + MOE_BLK - 1) // MOE_BLK) * MOE_BLK
    pad_end = jnp.cumsum(padded)
    pad_start = pad_end - padded
    n_blocks = (n_tok * 2) // MOE_BLK + MOE_EXPERTS
    n_pad = n_blocks * MOE_BLK
    blk_row = jnp.arange(n_blocks, dtype=jnp.int32) * MOE_BLK
    block_expert = jnp.minimum(jnp.sum(blk_row[:, None] >= pad_end[None, :], axis=1),
                               MOE_EXPERTS - 1).astype(jnp.int32)
    n_used = (pad_end[-1:] // MOE_BLK).astype(jnp.int32)
    seg_row = blk_row - pad_start[block_expert]
    valid = jnp.clip(counts[block_expert] - seg_row, 0, MOE_BLK).astype(jnp.int32)

    pos = _moe_positions(pad_start, meta).reshape(2 * n_tok)
    xd = _sc_scatter_rows(xp, pos, n_pad)
    y = _moe_experts(xd, pad_start // MOE_BLK, padded // MOE_BLK, n_used, valid, layer,
                     w_gate, w_up, w_down)

    return _sc_gather_rows(y, pos)


def _moe_combine(x, moe_out, final_g=None):
    bsz, seq, d = x.shape
    n_tok = bsz * seq
    xt = x.reshape(n_tok, d)
    wts, gathered = moe_out
    tm = COMBINE_TM
    n_tiles = n_tok // tm
    final = final_g is not None
    fg = (final_g if final else jnp.ones((d,), F32)).reshape(1, d)
    out = pl.pallas_call(
        functools.partial(_combine_kernel, final=final),
        grid=(n_tiles,),
        in_specs=[
            pl.BlockSpec((tm, d), lambda i: (i, 0)),
            pl.BlockSpec((tm, LANES), lambda i: (i, 0)),
            pl.BlockSpec((tm, PACKED), lambda i: (i, 0)),
            pl.BlockSpec((tm, PACKED), lambda i: (i + n_tiles, 0)),
            pl.BlockSpec((1, d), lambda i: (0, 0)),
        ],
        out_specs=pl.BlockSpec((tm, d), lambda i: (i, 0)),
        out_shape=jax.ShapeDtypeStruct((n_tok, d), F32),
        compiler_params=pltpu.CompilerParams(
            dimension_semantics=("parallel",), vmem_limit_bytes=VMEM_LIMIT),
        name="moe_combine",
    )(xt, wts, gathered, gathered, fg)
    return out.reshape(bsz, seq, d)


def kernel(x, mem, s5_norm, s5_w_in, s5_lambda_re, s5_lambda_im, s5_log_dt, s5_b_re, s5_b_im, s5_c_re, s5_c_im, s5_d, s5_w_out, gm_norm, gm_w_in, gm_v_norm, gm_w_s, gm_b_s, gm_w_out, mem_norm, xa_norm, xa_w_q, xa_w_kv, xa_w_o, moe_norm, moe_w_group, moe_b_group, moe_w_expert, moe_b_expert, moe_w_gate, moe_w_up, moe_w_down, final_norm):
    moe_out = None
    for i in range(DEPTH):
        j = i // 2
        if i % 2 == 0:
            if moe_out is not None:
                x = _moe_combine(x, moe_out)
            x = _s5_layer(x, s5_norm[j], s5_w_in[j], s5_lambda_re[j], s5_lambda_im[j],
                          s5_log_dt[j], s5_b_re[j], s5_b_im[j], s5_c_re[j], s5_c_im[j],
                          s5_d[j], s5_w_out[j])
        else:
            x = _gmlp_layer(x, moe_out, gm_norm[j], gm_w_in[j], gm_v_norm[j], gm_w_s[j],
                            gm_b_s[j], gm_w_out[j])
        x, meta, wts, cnt, xp = _xattn_router_layer(
            x, mem, mem_norm, xa_norm[i], xa_w_q[i], xa_w_kv[i], xa_w_o[i], moe_norm[i],
            moe_w_group[i], moe_b_group[i], moe_w_expert[i], moe_b_expert[i])
        moe_out = (wts, _moe_layer(x, meta, cnt, xp, i, moe_w_gate, moe_w_up, moe_w_down))
    return _moe_combine(x, moe_out, final_g=final_norm)
```

```python
import functools

import jax
import jax.numpy as jnp
from jax import lax
from jax.experimental import pallas as pl
from jax.experimental.pallas import tpu as pltpu
from jax.experimental.pallas import tpu_sc as plsc

F32 = jnp.float32
BF16 = jnp.bfloat16

D_MODEL = 1024
DEPTH = 2
CHUNK = 64
S5_GROUP_CH = 16
S5_GROUPS = 64
S5_STATE = 64
GM_HALF = 2 * D_MODEL
GM_GROUPS = 8
GM_SPAN = 128
GM_GROUP_CH = GM_HALF // GM_GROUPS
XA_HEADS = 4
XA_HEAD_DIM = D_MODEL // XA_HEADS
MOE_GROUPS = 4
MOE_PER_GROUP = 8
MOE_EXPERTS = MOE_GROUPS * MOE_PER_GROUP
MOE_HIDDEN = D_MODEL // 2
RMS_EPS = 1e-6

LANES = 128
SUBLANES = 8
VMEM_LIMIT = 56 * 1024 * 1024

S5_CT = 256
S5_NCT = D_MODEL // S5_CT
S5_STATES_CT = (S5_CT // S5_GROUP_CH) * S5_STATE
S5_SLABS = S5_STATES_CT // LANES
S5_TC = 128
S5_PITCH = S5_TC + 4
S5_SCAN_CT = 2

MOE_BLK = 512
COMBINE_TM = 512


def _rms(x, g):
    ms = jnp.mean(x * x, axis=-1, keepdims=True)
    return x * lax.rsqrt(ms + RMS_EPS) * g


def _const_spec(shape):
    nd = len(shape)
    return pl.BlockSpec(shape, lambda *_: (0,) * nd, pipeline_mode=pl.Buffered(1))


def _s5_kernel(x_ref, g_ref, win_ref, bblk_ref, cblk_ref, are_ref, aim_ref, d_ref,
               wout_ref, o_ref, hre_ref, him_ref, bre_ref, bim_ref, sre_ref, sim_ref):
    nb = x_ref.shape[0]
    tc = x_ref.shape[1]

    @pl.when(pl.program_id(0) == 0)
    def _():
        hre_ref[...] = jnp.zeros_like(hre_ref)
        him_ref[...] = jnp.zeros_like(him_ref)

    x = x_ref[...].reshape(nb * tc, D_MODEL)
    xn = _rms(x, g_ref[...]).astype(BF16)
    u = jnp.dot(xn, win_ref[...], preferred_element_type=F32)

    y_parts = []
    for ct0 in range(0, S5_NCT, S5_SCAN_CT):
        tiles = range(ct0, ct0 + S5_SCAN_CT)
        for c, ct in enumerate(tiles):
            u_ct = u[:, ct * S5_CT:(ct + 1) * S5_CT]
            bu = jnp.dot(u_ct.astype(BF16), bblk_ref[ct], preferred_element_type=F32)
            for b in range(nb):
                for j in range(S5_SLABS):
                    rows = slice(b * tc, (b + 1) * tc)
                    bre_ref[c, b, pl.ds(j * S5_PITCH, tc), :] = bu[rows, j * LANES:(j + 1) * LANES]
                    bim_ref[c, b, pl.ds(j * S5_PITCH, tc), :] = bu[
                        rows, S5_STATES_CT + j * LANES:S5_STATES_CT + (j + 1) * LANES]
        chains = [(c, ct, b) for c, ct in enumerate(tiles) for b in range(nb)]
        a_re = [are_ref[ct] for ct in tiles]
        a_im = [aim_ref[ct] for ct in tiles]

        def step(t, carry):
            idx = pl.ds(t, S5_SLABS, stride=S5_PITCH)
            bu_t = [(bre_ref[c, b, idx, :], bim_ref[c, b, idx, :]) for c, _, b in chains]
            new = []
            for k, (c, _, b) in enumerate(chains):
                hr, hi = carry[2 * k], carry[2 * k + 1]
                new.append(a_re[c] * hr - a_im[c] * hi + bu_t[k][0])
                new.append(a_re[c] * hi + a_im[c] * hr + bu_t[k][1])
            for k, (c, _, b) in enumerate(chains):
                sre_ref[c, b, idx, :] = new[2 * k]
                sim_ref[c, b, idx, :] = new[2 * k + 1]
            return tuple(new)

        init = []
        for _, ct, b in chains:
            init += [hre_ref[ct, b], him_ref[ct, b]]
        fin = lax.fori_loop(0, tc, step, tuple(init), unroll=2)
        for k, (_, ct, b) in enumerate(chains):
            hre_ref[ct, b] = fin[2 * k]
            him_ref[ct, b] = fin[2 * k + 1]

        for c, ct in enumerate(tiles):
            xs = []
            for b in range(nb):
                cols = [sre_ref[c, b, pl.ds(j * S5_PITCH, tc), :] for j in range(S5_SLABS)]
                cols += [sim_ref[c, b, pl.ds(j * S5_PITCH, tc), :] for j in range(S5_SLABS)]
                xs.append(jnp.concatenate(cols, axis=1))
            xst = jnp.concatenate(xs, axis=0).astype(BF16)
            y_parts.append(jnp.dot(xst, cblk_ref[ct], preferred_element_type=F32))

    y = jnp.concatenate(y_parts, axis=1) + d_ref[...] * u
    yg = jax.nn.gelu(y).astype(BF16)
    z = jnp.dot(yg, wout_ref[...], preferred_element_type=F32)
    out = x + z[:, :D_MODEL] * (1.0 / (1.0 + jnp.exp(-z[:, D_MODEL:])))
    o_ref[...] = out.reshape(nb, tc, D_MODEL)


def _s5_discretize(lam_re, lam_im, log_dt, b_re, b_im, c_re, c_im):
    lr = lam_re.astype(F32)
    li = lam_im.astype(F32)
    dt = jnp.exp(log_dt.astype(F32))[:, None]
    mag = jnp.exp(lr * dt)
    ab_re = mag * jnp.cos(li * dt)
    ab_im = mag * jnp.sin(li * dt)
    den = lr * lr + li * li
    coef_re = ((ab_re - 1.0) * lr + ab_im * li) / den
    coef_im = (ab_im * lr - (ab_re - 1.0) * li) / den
    br = b_re.astype(F32)
    bi = b_im.astype(F32)
    bb_re = coef_re[..., None] * br - coef_im[..., None] * bi
    bb_im = coef_re[..., None] * bi + coef_im[..., None] * br
    gpt = S5_CT // S5_GROUP_CH
    eye = jnp.eye(gpt, dtype=F32)

    def in_blocks(bb):
        t = bb.reshape(S5_NCT, gpt, S5_STATE, S5_GROUP_CH).transpose(0, 1, 3, 2)
        blk = t[:, :, :, None, :] * eye[None, :, None, :, None]
        return blk.reshape(S5_NCT, S5_CT, S5_STATES_CT)

    def out_blocks(c):
        t = c.reshape(S5_NCT, gpt, S5_GROUP_CH, S5_STATE).transpose(0, 1, 3, 2)
        blk = t[:, :, :, None, :] * eye[None, :, None, :, None]
        return blk.reshape(S5_NCT, S5_STATES_CT, S5_CT)

    bblk = jnp.concatenate([in_blocks(bb_re), in_blocks(bb_im)], axis=2).astype(BF16)
    cblk = jnp.concatenate([out_blocks(c_re.astype(F32)),
                            out_blocks(-c_im.astype(F32))], axis=1).astype(BF16)
    a_re = ab_re.reshape(S5_NCT, S5_SLABS, LANES)
    a_im = ab_im.reshape(S5_NCT, S5_SLABS, LANES)
    return bblk, cblk, a_re, a_im


def _s5_layer(x, norm_g, w_in, lam_re, lam_im, log_dt, b_re, b_im, c_re, c_im, d_skip, w_out):
    bsz, seq, d = x.shape
    bblk, cblk, a_re, a_im = _s5_discretize(lam_re, lam_im, log_dt, b_re, b_im, c_re, c_im)
    xspec = pl.BlockSpec((bsz, S5_TC, d), lambda k: (0, k, 0))
    return pl.pallas_call(
        _s5_kernel,
        grid=(seq // S5_TC,),
        in_specs=[
            xspec,
            _const_spec((1, d)),
            _const_spec((d, d)),
            _const_spec(bblk.shape),
            _const_spec(cblk.shape),
            _const_spec(a_re.shape),
            _const_spec(a_im.shape),
            _const_spec((1, d)),
            _const_spec((d, 2 * d)),
        ],
        out_specs=xspec,
        out_shape=jax.ShapeDtypeStruct(x.shape, F32),
        scratch_shapes=[
            pltpu.VMEM((S5_NCT, bsz, S5_SLABS, LANES), F32),
            pltpu.VMEM((S5_NCT, bsz, S5_SLABS, LANES), F32),
            pltpu.VMEM((S5_SCAN_CT, bsz, S5_SLABS * S5_PITCH, LANES), F32),
            pltpu.VMEM((S5_SCAN_CT, bsz, S5_SLABS * S5_PITCH, LANES), F32),
            pltpu.VMEM((S5_SCAN_CT, bsz, S5_SLABS * S5_PITCH, LANES), F32),
            pltpu.VMEM((S5_SCAN_CT, bsz, S5_SLABS * S5_PITCH, LANES), F32),
        ],
        compiler_params=pltpu.CompilerParams(
            dimension_semantics=("arbitrary",), vmem_limit_bytes=VMEM_LIMIT),
        name="s5_layer",
    )(x, norm_g.reshape(1, d), w_in.astype(BF16), bblk, cblk, a_re, a_im,
      d_skip.reshape(1, d).astype(F32), w_out.astype(BF16))


def _gmlp_kernel(x_ref, wts_ref, g1_ref, g2_ref, g_ref, win_ref, vn_ref, ws_ref, bst_ref,
                 wout_ref, o_ref):
    tm = x_ref.shape[0]
    w = wts_ref[...]
    x = (x_ref[...] + w[:, 0:1] * _unpack_rows(g1_ref[...])
         + w[:, 1:2] * _unpack_rows(g2_ref[...]))
    xn = _rms(x, g_ref[...]).astype(BF16)
    u = jax.nn.gelu(jnp.dot(xn, win_ref[:, :GM_HALF], preferred_element_type=F32))
    v = jax.nn.gelu(jnp.dot(xn, win_ref[:, GM_HALF:], preferred_element_type=F32))
    vb = _rms(v, vn_ref[...]).astype(BF16)
    row = lax.broadcasted_iota(jnp.int32, (GM_SPAN, GM_SPAN), 0) // CHUNK
    col = lax.broadcasted_iota(jnp.int32, (GM_SPAN, GM_SPAN), 1) // CHUNK
    causal = row >= col
    ws = [jnp.where(causal, ws_ref[g], 0.0).astype(BF16) for g in range(GM_GROUPS)]
    spans = []
    for s in range(tm // GM_SPAN):
        parts = []
        for g in range(GM_GROUPS):
            vblk = vb[s * GM_SPAN:(s + 1) * GM_SPAN, g * GM_GROUP_CH:(g + 1) * GM_GROUP_CH]
            parts.append(jnp.dot(ws[g], vblk, preferred_element_type=F32)
                         + bst_ref[:, g:g + 1])
        spans.append(jnp.concatenate(parts, axis=1))
    mixed = jnp.concatenate(spans, axis=0)
    p = (u * mixed).astype(BF16)
    o_ref[...] = x + jnp.dot(p, wout_ref[...], preferred_element_type=F32)


def _gmlp_layer(x, moe_out, norm_g, w_in, v_norm, w_s, b_s, w_out, tm=512):
    bsz, seq, d = x.shape
    n_tok = bsz * seq
    xt = x.reshape(n_tok, d)
    wts, gathered = moe_out
    n_tiles = n_tok // tm
    xspec = pl.BlockSpec((tm, d), lambda i: (i, 0))
    out = pl.pallas_call(
        _gmlp_kernel,
        grid=(n_tiles,),
        in_specs=[
            xspec,
            pl.BlockSpec((tm, LANES), lambda i: (i, 0)),
            pl.BlockSpec((tm, d // 2), lambda i: (i, 0)),
            pl.BlockSpec((tm, d // 2), lambda i: (i + n_tiles, 0)),
            _const_spec((1, d)),
            _const_spec((d, 2 * GM_HALF)),
            _const_spec((1, GM_HALF)),
            _const_spec((GM_GROUPS, GM_SPAN, GM_SPAN)),
            _const_spec((GM_SPAN, GM_GROUPS)),
            _const_spec((GM_HALF, d)),
        ],
        out_specs=xspec,
        out_shape=jax.ShapeDtypeStruct((n_tok, d), F32),
        compiler_params=pltpu.CompilerParams(
            dimension_semantics=("parallel",), vmem_limit_bytes=VMEM_LIMIT),
        name="gmlp_layer",
    )(xt, wts, gathered, gathered, norm_g.reshape(1, d), w_in.astype(BF16),
      v_norm.reshape(1, GM_HALF),
      w_s.astype(F32), b_s.T.astype(F32), w_out.astype(BF16))
    return out.reshape(bsz, seq, d)


def _norm_proj_kernel(x_ref, g_ref, w_ref, o_ref):
    xn = _rms(x_ref[...], g_ref[...]).astype(BF16)
    o_ref[...] = jnp.dot(xn, w_ref[...], preferred_element_type=F32).astype(o_ref.dtype)


def _norm_proj(x, g, w, out_dtype, tm=512, tn=1024):
    m, d = x.shape
    n = w.shape[1]
    return pl.pallas_call(
        _norm_proj_kernel,
        grid=(m // tm, n // tn),
        in_specs=[
            pl.BlockSpec((tm, d), lambda i, j: (i, 0)),
            pl.BlockSpec((1, d), lambda i, j: (0, 0)),
            pl.BlockSpec((d, tn), lambda i, j: (0, j)),
        ],
        out_specs=pl.BlockSpec((tm, tn), lambda i, j: (i, j)),
        out_shape=jax.ShapeDtypeStruct((m, n), out_dtype),
        compiler_params=pltpu.CompilerParams(
            dimension_semantics=("parallel", "parallel"), vmem_limit_bytes=VMEM_LIMIT),
        name="norm_proj",
    )(x, g.reshape(1, d), w.astype(BF16))


def _route(x, g_ref, whi_ref, wlo_ref, bias_ref, run_ref):
    tm = x.shape[0]
    xn = _rms(x, g_ref[...])
    xhi = xn.astype(BF16)
    xlo = (xn - xhi.astype(F32)).astype(BF16)
    logits = (jnp.dot(xhi, whi_ref[...], preferred_element_type=F32)
              + jnp.dot(xhi, wlo_ref[...], preferred_element_type=F32)
              + jnp.dot(xlo, whi_ref[...], preferred_element_type=F32)) + bias_ref[...]
    lane = lax.broadcasted_iota(jnp.int32, (tm, LANES), 1)
    neg = jnp.float32(-jnp.inf)

    def first_argmax(vals):
        mx = jnp.max(vals, axis=-1, keepdims=True)
        idx = jnp.min(jnp.where(vals == mx, lane, LANES), axis=-1, keepdims=True)
        return mx, idx

    gl = jnp.where(lane < MOE_GROUPS, logits, neg)
    gmax, gidx = first_argmax(gl)
    w_g = 1.0 / jnp.sum(jnp.exp(gl - gmax), axis=-1, keepdims=True)
    lo = MOE_GROUPS + MOE_PER_GROUP * gidx
    el = jnp.where((lane >= lo) & (lane < lo + MOE_PER_GROUP), logits, neg)
    m1, i1 = first_argmax(el)
    m2, i2 = first_argmax(jnp.where(lane == i1, neg, el))
    e21 = jnp.exp(m2 - m1)
    w1 = w_g / (1.0 + e21)
    w2 = w_g * e21 / (1.0 + e21)
    e1 = i1 - MOE_GROUPS
    e2 = i2 - MOE_GROUPS

    onehot = ((lane == e1) | (lane == e2)).astype(BF16)
    r = lax.broadcasted_iota(jnp.int32, (tm, tm), 0)
    c = lax.broadcasted_iota(jnp.int32, (tm, tm), 1)
    before = (c < r).astype(BF16)
    tot = run_ref[...] + jnp.dot(before, onehot, preferred_element_type=F32)
    r1 = jnp.sum(jnp.where(lane == e1, tot, 0.0), axis=-1, keepdims=True)
    r2 = jnp.sum(jnp.where(lane == e2, tot, 0.0), axis=-1, keepdims=True)
    run_ref[...] = run_ref[...] + jnp.sum(onehot.astype(F32), axis=0, keepdims=True)

    meta = jnp.where(lane == 0, e1.astype(F32), jnp.where(lane == 1, e2.astype(F32),
                     jnp.where(lane == 2, r1, jnp.where(lane == 3, r2, 0.0))))
    meta_t = jnp.transpose(meta)[:SUBLANES, :].astype(jnp.int32)
    wts = jnp.where(lane == 0, w1, jnp.where(lane == 1, w2, 0.0))
    return meta_t, wts, xn


def _xattn_router_kernel(x_ref, g_ref, wq_ref, kt_ref, v_ref, wo_ref, mg_ref, whi_ref,
                         wlo_ref, bias_ref, o_ref, meta_ref, wts_ref, cnt_ref, xp_ref, run_ref):
    @pl.when((pl.program_id(0) == 0) & (pl.program_id(1) == 0))
    def _():
        run_ref[...] = jnp.zeros_like(run_ref)

    x = x_ref[0]
    xn = _rms(x, g_ref[...]).astype(BF16)
    q = jnp.dot(xn, wq_ref[...], preferred_element_type=F32) * (XA_HEAD_DIM ** -0.5)
    q = q.astype(BF16)
    heads = []
    for h in range(XA_HEADS):
        cols = slice(h * XA_HEAD_DIM, (h + 1) * XA_HEAD_DIM)
        s = jnp.dot(q[:, cols], kt_ref[0, cols, :], preferred_element_type=F32)
        e = jnp.exp(s - jnp.max(s, axis=-1, keepdims=True))
        p = (e / jnp.sum(e, axis=-1, keepdims=True)).astype(BF16)
        heads.append(jnp.dot(p, v_ref[0, :, cols], preferred_element_type=F32))
    o = jnp.concatenate(heads, axis=1).astype(BF16)
    out = x + jnp.dot(o, wo_ref[...], preferred_element_type=F32)
    o_ref[0] = out
    meta, wts, xn_moe = _route(out, mg_ref, whi_ref, wlo_ref, bias_ref, run_ref)
    meta_ref[...] = meta
    wts_ref[...] = wts
    cnt_ref[...] = run_ref[...]
    xp_ref[...] = _pack_rows(xn_moe)


def _xattn_router_layer(x, mem, mem_g, norm_g, w_q, w_kv, w_o, moe_g, w_group, b_group,
                        w_expert, b_expert, tm=512):
    bsz, seq, d = x.shape
    m = mem.shape[1]
    nt = seq // tm
    kv = _norm_proj(mem.reshape(bsz * m, d), mem_g, w_kv, BF16).reshape(bsz, m, 2 * d)
    kt = kv[..., :d].transpose(0, 2, 1)
    v = kv[..., d:]
    nr = MOE_GROUPS + MOE_EXPERTS
    w_r = jnp.zeros((d, LANES), F32).at[:, :MOE_GROUPS].set(w_group.astype(F32))
    w_r = w_r.at[:, MOE_GROUPS:nr].set(w_expert.astype(F32))
    bias = jnp.zeros((1, LANES), F32).at[0, :MOE_GROUPS].set(b_group.astype(F32))
    bias = bias.at[0, MOE_GROUPS:nr].set(b_expert.astype(F32))
    w_hi = w_r.astype(BF16)
    w_lo = (w_r - w_hi.astype(F32)).astype(BF16)
    xspec = pl.BlockSpec((1, tm, d), lambda b, i: (b, i, 0))
    return pl.pallas_call(
        _xattn_router_kernel,
        grid=(bsz, nt),
        in_specs=[
            xspec,
            _const_spec((1, d)),
            _const_spec((d, d)),
            pl.BlockSpec((1, d, m), lambda b, i: (b, 0, 0)),
            pl.BlockSpec((1, m, d), lambda b, i: (b, 0, 0)),
            _const_spec((d, d)),
            _const_spec((1, d)),
            _const_spec((d, LANES)),
            _const_spec((d, LANES)),
            _const_spec((1, LANES)),
        ],
        out_specs=[
            xspec,
            pl.BlockSpec((SUBLANES, tm), lambda b, i: (0, b * nt + i)),
            pl.BlockSpec((tm, LANES), lambda b, i: (b * nt + i, 0)),
            pl.BlockSpec((1, LANES), lambda b, i: (0, 0)),
            pl.BlockSpec((tm, d // 2), lambda b, i: (b * nt + i, 0)),
        ],
        out_shape=[
            jax.ShapeDtypeStruct(x.shape, F32),
            jax.ShapeDtypeStruct((SUBLANES, bsz * seq), jnp.int32),
            jax.ShapeDtypeStruct((bsz * seq, LANES), F32),
            jax.ShapeDtypeStruct((1, LANES), F32),
            jax.ShapeDtypeStruct((bsz * seq, d // 2), jnp.int32),
        ],
        scratch_shapes=[pltpu.VMEM((1, LANES), F32)],
        compiler_params=pltpu.CompilerParams(
            dimension_semantics=("arbitrary", "arbitrary"), vmem_limit_bytes=VMEM_LIMIT),
        name="xattn_router",
    )(x, norm_g.reshape(1, d), w_q.astype(BF16), kt, v, w_o.astype(BF16),
      moe_g.reshape(1, d), w_hi, w_lo, bias)


PACKED = D_MODEL // 2
I32 = jnp.int32
HI_HALF = -65536
LO_HALF = 65535


def _pack_rows(v):
    lo = lax.bitcast_convert_type(v[:, :PACKED].astype(BF16).astype(F32), I32)
    hi = lax.bitcast_convert_type(v[:, PACKED:].astype(BF16).astype(F32), I32)
    return (hi & HI_HALF) | ((lo >> 16) & LO_HALF)


def _unpack_rows(p):
    lo = lax.bitcast_convert_type(p << 16, F32)
    hi = lax.bitcast_convert_type(p & HI_HALF, F32)
    return jnp.concatenate([lo, hi], axis=1)


def _expert_kernel(bstart_ref, nblk_ref, nused_ref, valid_ref, xd_hbm, wg_ref, wu_ref, wd_ref,
                   y_hbm, wgb, wub, wdb, xbuf, ybuf, lsem, ssem):
    e = pl.program_id(0)
    n_used = nused_ref[0]
    b0 = bstart_ref[e]
    nb = nblk_ref[e]

    def rows(b):
        return pl.ds(pl.multiple_of(b * MOE_BLK, MOE_BLK), MOE_BLK)

    def load(b, slot):
        return pltpu.make_async_copy(xd_hbm.at[rows(b), :], xbuf.at[slot], lsem.at[slot])

    def store(b, slot):
        return pltpu.make_async_copy(ybuf.at[slot], y_hbm.at[rows(b), :], ssem.at[slot])

    @pl.when((e == 0) & (n_used > 0))
    def _():
        load(0, 0).start()

    @pl.when(nb > 0)
    def _():
        wgb[...] = wg_ref[...].astype(BF16)
        wub[...] = wu_ref[...].astype(BF16)
        wdb[...] = wd_ref[...].astype(BF16)

    def block(k, carry):
        b = b0 + k
        slot = b % 2
        load(b, slot).wait()

        @pl.when(b + 1 < n_used)
        def _():
            load(b + 1, 1 - slot).start()

        @pl.when(b >= 2)
        def _():
            store(b - 2, slot).wait()

        row = lax.broadcasted_iota(jnp.int32, (MOE_BLK, PACKED), 0)
        xd = jnp.where(row < valid_ref[b], xbuf[slot], 0)
        xb = _unpack_rows(xd).astype(BF16)
        a = jnp.dot(xb, wgb[...], preferred_element_type=F32)
        up = jnp.dot(xb, wub[...], preferred_element_type=F32)
        h = (a * (1.0 / (1.0 + jnp.exp(-a))) * up).astype(BF16)
        ybuf[slot] = _pack_rows(jnp.dot(h, wdb[...], preferred_element_type=F32))
        store(b, slot).start()
        return carry

    lax.fori_loop(0, nb, block, 0)

    @pl.when(e == pl.num_programs(0) - 1)
    def _():
        for back in (1, 2):
            @pl.when(n_used >= back)
            def _():
                last = n_used - back
                store(last, last % 2).wait()


def _moe_experts(xd, blk_start, blk_count, n_used, valid, layer, w_gate, w_up, w_down):
    n_pad = xd.shape[0]
    d = D_MODEL

    def expert_blk(e, *_):
        return (layer, e, 0, 0)

    return pl.pallas_call(
        _expert_kernel,
        grid_spec=pltpu.PrefetchScalarGridSpec(
            num_scalar_prefetch=4,
            grid=(MOE_EXPERTS,),
            in_specs=[
                pl.BlockSpec(memory_space=pl.ANY),
                pl.BlockSpec((None, None, d, MOE_HIDDEN), expert_blk),
                pl.BlockSpec((None, None, d, MOE_HIDDEN), expert_blk),
                pl.BlockSpec((None, None, MOE_HIDDEN, d), expert_blk),
            ],
            out_specs=pl.BlockSpec(memory_space=pl.ANY),
            scratch_shapes=[pltpu.VMEM((d, MOE_HIDDEN), BF16), pltpu.VMEM((d, MOE_HIDDEN), BF16),
                            pltpu.VMEM((MOE_HIDDEN, d), BF16),
                            pltpu.VMEM((2, MOE_BLK, PACKED), I32),
                            pltpu.VMEM((2, MOE_BLK, PACKED), I32),
                            pltpu.SemaphoreType.DMA((2,)), pltpu.SemaphoreType.DMA((2,))],
        ),
        out_shape=jax.ShapeDtypeStruct((n_pad, PACKED), I32),
        compiler_params=pltpu.CompilerParams(
            dimension_semantics=("arbitrary",), vmem_limit_bytes=VMEM_LIMIT),
        name="moe_experts",
    )(blk_start.astype(jnp.int32), blk_count.astype(jnp.int32), n_used, valid, xd,
      w_gate, w_up, w_down)


def _pos_kernel(ps_ref, meta_ref, pos_ref):
    m = meta_ref[...]
    e = m[0:2, :]
    base = jnp.zeros_like(e)
    for k in range(MOE_EXPERTS):
        base = jnp.where(e == k, ps_ref[k], base)
    pos_ref[...] = base + m[2:4, :]


def _moe_positions(pad_start, meta, tm=2048):
    n_tok = meta.shape[1]
    return pl.pallas_call(
        _pos_kernel,
        grid_spec=pltpu.PrefetchScalarGridSpec(
            num_scalar_prefetch=1,
            grid=(n_tok // tm,),
            in_specs=[pl.BlockSpec((SUBLANES, tm), lambda i, ps: (0, i))],
            out_specs=pl.BlockSpec((2, tm), lambda i, ps: (0, i)),
        ),
        out_shape=jax.ShapeDtypeStruct((2, n_tok), jnp.int32),
        compiler_params=pltpu.CompilerParams(dimension_semantics=("parallel",)),
        name="moe_positions",
    )(pad_start, meta)


SC_CORES = 2
SC_SUBCORES = 16
SC_CHUNK = 64


def _sc_gather_rows(table, idx):
    n_rows = idx.shape[0]
    width = table.shape[1]
    workers = SC_CORES * SC_SUBCORES
    per_worker = n_rows // workers
    n_chunks = per_worker // SC_CHUNK
    mesh = plsc.VectorSubcoreMesh(core_axis_name="c", subcore_axis_name="s")

    def body(table_hbm, idx_hbm, out_hbm, idx_v, rows_v, sem):
        wid = lax.axis_index("s") * SC_CORES + lax.axis_index("c")
        base = wid * per_worker

        @pl.loop(0, n_chunks)
        def _(j):
            off = pl.multiple_of(base + j * SC_CHUNK, SC_CHUNK)
            pltpu.sync_copy(idx_hbm.at[pl.ds(off, SC_CHUNK)], idx_v)
            pltpu.async_copy(table_hbm.at[idx_v], rows_v, sem).wait()
            pltpu.sync_copy(rows_v, out_hbm.at[pl.ds(off, SC_CHUNK)])

    return pl.kernel(
        body,
        out_type=jax.ShapeDtypeStruct((n_rows, width), table.dtype),
        mesh=mesh,
        scratch_types=[pltpu.VMEM((SC_CHUNK,), jnp.int32),
                       pltpu.VMEM((SC_CHUNK, width), table.dtype),
                       pltpu.SemaphoreType.DMA],
        name="sc_gather_rows",
    )(table, idx)


def _sc_scatter_rows(rows, idx, n_out):
    n_rows, width = rows.shape
    workers = SC_CORES * SC_SUBCORES
    per_worker = n_rows // workers
    n_chunks = per_worker // SC_CHUNK
    mesh = plsc.VectorSubcoreMesh(core_axis_name="c", subcore_axis_name="s")

    def body(rows_hbm, idx_hbm, out_hbm, idx_v, rows_v, sem):
        wid = lax.axis_index("s") * SC_CORES + lax.axis_index("c")
        base = wid * per_worker

        @pl.loop(0, n_chunks)
        def _(j):
            off = pl.multiple_of(base + j * SC_CHUNK, SC_CHUNK)
            pltpu.sync_copy(rows_hbm.at[pl.ds(off, SC_CHUNK)], rows_v)
            for k in range(2):
                koff = pl.multiple_of(k * n_rows + off, SC_CHUNK)
                pltpu.sync_copy(idx_hbm.at[pl.ds(koff, SC_CHUNK)], idx_v)
                pltpu.async_copy(rows_v, out_hbm.at[idx_v], sem).wait()

    return pl.kernel(
        body,
        out_type=jax.ShapeDtypeStruct((n_out, width), rows.dtype),
        mesh=mesh,
        scratch_types=[pltpu.VMEM((SC_CHUNK,), jnp.int32),
                       pltpu.VMEM((SC_CHUNK, width), rows.dtype),
                       pltpu.SemaphoreType.DMA],
        name="sc_scatter_rows",
    )(rows, idx)


def _combine_kernel(x_ref, wts_ref, g1_ref, g2_ref, fg_ref, o_ref, *, final):
    w = wts_ref[...]
    out = (x_ref[...] + w[:, 0:1] * _unpack_rows(g1_ref[...])
           + w[:, 1:2] * _unpack_rows(g2_ref[...]))
    if final:
        out = _rms(out, fg_ref[...])
    o_ref[...] = out


def _moe_layer(x, meta, cnt, xp, layer, w_gate, w_up, w_down):
    n_tok = x.shape[0] * x.shape[1]

    counts = cnt[0, :MOE_EXPERTS].astype(jnp.int32)
    padded = ((counts + MOE_BLK - 1) // MOE_BLK) * MOE_BLK
    pad_end = jnp.cumsum(padded)
    pad_start = pad_end - padded
    n_blocks = (n_tok * 2) // MOE_BLK + MOE_EXPERTS
    n_pad = n_blocks * MOE_BLK
    blk_row = jnp.arange(n_blocks, dtype=jnp.int32) * MOE_BLK
    block_expert = jnp.minimum(jnp.sum(blk_row[:, None] >= pad_end[None, :], axis=1),
                               MOE_EXPERTS - 1).astype(jnp.int32)
    n_used = (pad_end[-1:] // MOE_BLK).astype(jnp.int32)
    seg_row = blk_row - pad_start[block_expert]
    valid = jnp.clip(counts[block_expert] - seg_row, 0, MOE_BLK).astype(jnp.int32)

    pos = _moe_positions(pad_start, meta).reshape(2 * n_tok)
    xd = _sc_scatter_rows(xp, pos, n_pad)
    y = _moe_experts(xd, pad_start // MOE_BLK, padded // MOE_BLK, n_used, valid, layer,
                     w_gate, w_up, w_down)

    return _sc_gather_rows(y, pos)


def _moe_combine(x, moe_out, final_g=None):
    bsz, seq, d = x.shape
    n_tok = bsz * seq
    xt = x.reshape(n_tok, d)
    wts, gathered = moe_out
    tm = COMBINE_TM
    n_tiles = n_tok // tm
    final = final_g is not None
    fg = (final_g if final else jnp.ones((d,), F32)).reshape(1, d)
    out = pl.pallas_call(
        functools.partial(_combine_kernel, final=final),
        grid=(n_tiles,),
        in_specs=[
            pl.BlockSpec((tm, d), lambda i: (i, 0)),
            pl.BlockSpec((tm, LANES), lambda i: (i, 0)),
            pl.BlockSpec((tm, PACKED), lambda i: (i, 0)),
            pl.BlockSpec((tm, PACKED), lambda i: (i + n_tiles, 0)),
            pl.BlockSpec((1, d), lambda i: (0, 0)),
        ],
        out_specs=pl.BlockSpec((tm, d), lambda i: (i, 0)),
        out_shape=jax.ShapeDtypeStruct((n_tok, d), F32),
        compiler_params=pltpu.CompilerParams(
            dimension_semantics=("parallel",), vmem_limit_bytes=VMEM_LIMIT),
        name="moe_combine",
    )(xt, wts, gathered, gathered, fg)
    return out.reshape(bsz, seq, d)


def kernel(x, mem, s5_norm, s5_w_in, s5_lambda_re, s5_lambda_im, s5_log_dt, s5_b_re, s5_b_im, s5_c_re, s5_c_im, s5_d, s5_w_out, gm_norm, gm_w_in, gm_v_norm, gm_w_s, gm_b_s, gm_w_out, mem_norm, xa_norm, xa_w_q, xa_w_kv, xa_w_o, moe_norm, moe_w_group, moe_b_group, moe_w_expert, moe_b_expert, moe_w_gate, moe_w_up, moe_w_down, final_norm):
    moe_out = None
    for i in range(DEPTH):
        j = i // 2
        if i % 2 == 0:
            if moe_out is not None:
                x = _moe_combine(x, moe_out)
            x = _s5_layer(x, s5_norm[j], s5_w_in[j], s5_lambda_re[j], s5_lambda_im[j],
                          s5_log_dt[j], s5_b_re[j], s5_b_im[j], s5_c_re[j], s5_c_im[j],
                          s5_d[j], s5_w_out[j])
        else:
            x = _gmlp_layer(x, moe_out, gm_norm[j], gm_w_in[j], gm_v_norm[j], gm_w_s[j],
                            gm_b_s[j], gm_w_out[j])
        x, meta, wts, cnt, xp = _xattn_router_layer(
            x, mem, mem_norm, xa_norm[i], xa_w_q[i], xa_w_kv[i], xa_w_o[i], moe_norm[i],
            moe_w_group[i], moe_b_group[i], moe_w_expert[i], moe_b_expert[i])
        moe_out = (wts, _moe_layer(x, meta, cnt, xp, i, moe_w_gate, moe_w_up, moe_w_down))
    return _moe_combine(x, moe_out, final_g=final_norm)
```

```python
import functools

import jax
import jax.numpy as jnp
from jax import lax
from jax.experimental import pallas as pl
from jax.experimental.pallas import tpu as pltpu
from jax.experimental.pallas import tpu_sc as plsc

F32 = jnp.float32
BF16 = jnp.bfloat16

D_MODEL = 1024
DEPTH = 2
CHUNK = 64
S5_GROUP_CH = 16
S5_GROUPS = 64
S5_STATE = 64
GM_HALF = 2 * D_MODEL
GM_GROUPS = 8
GM_SPAN = 128
GM_GROUP_CH = GM_HALF // GM_GROUPS
XA_HEADS = 4
XA_HEAD_DIM = D_MODEL // XA_HEADS
MOE_GROUPS = 4
MOE_PER_GROUP = 8
MOE_EXPERTS = MOE_GROUPS * MOE_PER_GROUP
MOE_HIDDEN = D_MODEL // 2
RMS_EPS = 1e-6

LANES = 128
SUBLANES = 8
VMEM_LIMIT = 56 * 1024 * 1024

S5_CT = 256
S5_NCT = D_MODEL // S5_CT
S5_STATES_CT = (S5_CT // S5_GROUP_CH) * S5_STATE
S5_SLABS = S5_STATES_CT // LANES
S5_TC = 128
S5_PITCH = S5_TC + 4
S5_SCAN_CT = 2

MOE_BLK = 512
COMBINE_TM = 512


def _rms(x, g):
    ms = jnp.mean(x * x, axis=-1, keepdims=True)
    return x * lax.rsqrt(ms + RMS_EPS) * g


def _const_spec(shape):
    nd = len(shape)
    return pl.BlockSpec(shape, lambda *_: (0,) * nd, pipeline_mode=pl.Buffered(1))


def _s5_kernel(x_ref, g_ref, win_ref, bblk_ref, cblk_ref, are_ref, aim_ref, d_ref,
               wout_ref, o_ref, hre_ref, him_ref, bre_ref, bim_ref, sre_ref, sim_ref):
    nb = x_ref.shape[0]
    tc = x_ref.shape[1]

    @pl.when(pl.program_id(0) == 0)
    def _():
        hre_ref[...] = jnp.zeros_like(hre_ref)
        him_ref[...] = jnp.zeros_like(him_ref)

    x = x_ref[...].reshape(nb * tc, D_MODEL)
    xn = _rms(x, g_ref[...]).astype(BF16)
    u = jnp.dot(xn, win_ref[...], preferred_element_type=F32)

    y_parts = []
    for ct0 in range(0, S5_NCT, S5_SCAN_CT):
        tiles = range(ct0, ct0 + S5_SCAN_CT)
        for c, ct in enumerate(tiles):
            u_ct = u[:, ct * S5_CT:(ct + 1) * S5_CT]
            bu = jnp.dot(u_ct.astype(BF16), bblk_ref[ct], preferred_element_type=F32)
            for b in range(nb):
                for j in range(S5_SLABS):
                    rows = slice(b * tc, (b + 1) * tc)
                    bre_ref[c, b, pl.ds(j * S5_PITCH, tc), :] = bu[rows, j * LANES:(j + 1) * LANES]
                    bim_ref[c, b, pl.ds(j * S5_PITCH, tc), :] = bu[
                        rows, S5_STATES_CT + j * LANES:S5_STATES_CT + (j + 1) * LANES]
        chains = [(c, ct, b) for c, ct in enumerate(tiles) for b in range(nb)]
        a_re = [are_ref[ct] for ct in tiles]
        a_im = [aim_ref[ct] for ct in tiles]

        def step(t, carry):
            idx = pl.ds(t, S5_SLABS, stride=S5_PITCH)
            bu_t = [(bre_ref[c, b, idx, :], bim_ref[c, b, idx, :]) for c, _, b in chains]
            new = []
            for k, (c, _, b) in enumerate(chains):
                hr, hi = carry[2 * k], carry[2 * k + 1]
                new.append(a_re[c] * hr - a_im[c] * hi + bu_t[k][0])
                new.append(a_re[c] * hi + a_im[c] * hr + bu_t[k][1])
            for k, (c, _, b) in enumerate(chains):
                sre_ref[c, b, idx, :] = new[2 * k]
                sim_ref[c, b, idx, :] = new[2 * k + 1]
            return tuple(new)

        init = []
        for _, ct, b in chains:
            init += [hre_ref[ct, b], him_ref[ct, b]]
        fin = lax.fori_loop(0, tc, step, tuple(init), unroll=2)
        for k, (_, ct, b) in enumerate(chains):
            hre_ref[ct, b] = fin[2 * k]
            him_ref[ct, b] = fin[2 * k + 1]

        for c, ct in enumerate(tiles):
            xs = []
            for b in range(nb):
                cols = [sre_ref[c, b, pl.ds(j * S5_PITCH, tc), :] for j in range(S5_SLABS)]
                cols += [sim_ref[c, b, pl.ds(j * S5_PITCH, tc), :] for j in range(S5_SLABS)]
                xs.append(jnp.concatenate(cols, axis=1))
            xst = jnp.concatenate(xs, axis=0).astype(BF16)
            y_parts.append(jnp.dot(xst, cblk_ref[ct], preferred_element_type=F32))

    y = jnp.concatenate(y_parts, axis=1) + d_ref[...] * u
    yg = jax.nn.gelu(y).astype(BF16)
    z = jnp.dot(yg, wout_ref[...], preferred_element_type=F32)
    out = x + z[:, :D_MODEL] * (1.0 / (1.0 + jnp.exp(-z[:, D_MODEL:])))
    o_ref[...] = out.reshape(nb, tc, D_MODEL)


def _s5_discretize(lam_re, lam_im, log_dt, b_re, b_im, c_re, c_im):
    lr = lam_re.astype(F32)
    li = lam_im.astype(F32)
    dt = jnp.exp(log_dt.astype(F32))[:, None]
    mag = jnp.exp(lr * dt)
    ab_re = mag * jnp.cos(li * dt)
    ab_im = mag * jnp.sin(li * dt)
    den = lr * lr + li * li
    coef_re = ((ab_re - 1.0) * lr + ab_im * li) / den
    coef_im = (ab_im * lr - (ab_re - 1.0) * li) / den
    br = b_re.astype(F32)
    bi = b_im.astype(F32)
    bb_re = coef_re[..., None] * br - coef_im[..., None] * bi
    bb_im = coef_re[..., None] * bi + coef_im[..., None] * br
    gpt = S5_CT // S5_GROUP_CH
    eye = jnp.eye(gpt, dtype=F32)

    def in_blocks(bb):
        t = bb.reshape(S5_NCT, gpt, S5_STATE, S5_GROUP_CH).transpose(0, 1, 3, 2)
        blk = t[:, :, :, None, :] * eye[None, :, None, :, None]
        return blk.reshape(S5_NCT, S5_CT, S5_STATES_CT)

    def out_blocks(c):
        t = c.reshape(S5_NCT, gpt, S5_GROUP_CH, S5_STATE).transpose(0, 1, 3, 2)
        blk = t[:, :, :, None, :] * eye[None, :, None, :, None]
        return blk.reshape(S5_NCT, S5_STATES_CT, S5_CT)

    bblk = jnp.concatenate([in_blocks(bb_re), in_blocks(bb_im)], axis=2).astype(BF16)
    cblk = jnp.concatenate([out_blocks(c_re.astype(F32)),
                            out_blocks(-c_im.astype(F32))], axis=1).astype(BF16)
    a_re = ab_re.reshape(S5_NCT, S5_SLABS, LANES)
    a_im = ab_im.reshape(S5_NCT, S5_SLABS, LANES)
    return bblk, cblk, a_re, a_im


def _s5_layer(x, norm_g, w_in, lam_re, lam_im, log_dt, b_re, b_im, c_re, c_im, d_skip, w_out):
    bsz, seq, d = x.shape
    bblk, cblk, a_re, a_im = _s5_discretize(lam_re, lam_im, log_dt, b_re, b_im, c_re, c_im)
    xspec = pl.BlockSpec((bsz, S5_TC, d), lambda k: (0, k, 0))
    return pl.pallas_call(
        _s5_kernel,
        grid=(seq // S5_TC,),
        in_specs=[
            xspec,
            _const_spec((1, d)),
            _const_spec((d, d)),
            _const_spec(bblk.shape),
            _const_spec(cblk.shape),
            _const_spec(a_re.shape),
            _const_spec(a_im.shape),
            _const_spec((1, d)),
            _const_spec((d, 2 * d)),
        ],
        out_specs=xspec,
        out_shape=jax.ShapeDtypeStruct(x.shape, F32),
        scratch_shapes=[
            pltpu.VMEM((S5_NCT, bsz, S5_SLABS, LANES), F32),
            pltpu.VMEM((S5_NCT, bsz, S5_SLABS, LANES), F32),
            pltpu.VMEM((S5_SCAN_CT, bsz, S5_SLABS * S5_PITCH, LANES), F32),
            pltpu.VMEM((S5_SCAN_CT, bsz, S5_SLABS * S5_PITCH, LANES), F32),
            pltpu.VMEM((S5_SCAN_CT, bsz, S5_SLABS * S5_PITCH, LANES), F32),
            pltpu.VMEM((S5_SCAN_CT, bsz, S5_SLABS * S5_PITCH, LANES), F32),
        ],
        compiler_params=pltpu.CompilerParams(
            dimension_semantics=("arbitrary",), vmem_limit_bytes=VMEM_LIMIT),
        name="s5_layer",
    )(x, norm_g.reshape(1, d), w_in.astype(BF16), bblk, cblk, a_re, a_im,
      d_skip.reshape(1, d).astype(F32), w_out.astype(BF16))


def _gmlp_kernel(x_ref, wts_ref, g1_ref, g2_ref, g_ref, win_ref, vn_ref, ws_ref, bst_ref,
                 wout_ref, o_ref):
    tm = x_ref.shape[0]
    w = wts_ref[...]
    x = (x_ref[...] + w[:, 0:1] * _unpack_rows(g1_ref[...])
         + w[:, 1:2] * _unpack_rows(g2_ref[...]))
    xn = _rms(x, g_ref[...]).astype(BF16)
    u = jax.nn.gelu(jnp.dot(xn, win_ref[:, :GM_HALF], preferred_element_type=F32))
    v = jax.nn.gelu(jnp.dot(xn, win_ref[:, GM_HALF:], preferred_element_type=F32))
    vb = _rms(v, vn_ref[...]).astype(BF16)
    row = lax.broadcasted_iota(jnp.int32, (GM_SPAN, GM_SPAN), 0) // CHUNK
    col = lax.broadcasted_iota(jnp.int32, (GM_SPAN, GM_SPAN), 1) // CHUNK
    causal = row >= col
    ws = [jnp.where(causal, ws_ref[g], 0.0).astype(BF16) for g in range(GM_GROUPS)]
    spans = []
    for s in range(tm // GM_SPAN):
        parts = []
        for g in range(GM_GROUPS):
            vblk = vb[s * GM_SPAN:(s + 1) * GM_SPAN, g * GM_GROUP_CH:(g + 1) * GM_GROUP_CH]
            parts.append(jnp.dot(ws[g], vblk, preferred_element_type=F32)
                         + bst_ref[:, g:g + 1])
        spans.append(jnp.concatenate(parts, axis=1))
    mixed = jnp.concatenate(spans, axis=0)
    p = (u * mixed).astype(BF16)
    o_ref[...] = x + jnp.dot(p, wout_ref[...], preferred_element_type=F32)


def _gmlp_layer(x, moe_out, norm_g, w_in, v_norm, w_s, b_s, w_out, tm=512):
    bsz, seq, d = x.shape
    n_tok = bsz * seq
    xt = x.reshape(n_tok, d)
    wts, gathered = moe_out
    n_tiles = n_tok // tm
    xspec = pl.BlockSpec((tm, d), lambda i: (i, 0))
    out = pl.pallas_call(
        _gmlp_kernel,
        grid=(n_tiles,),
        in_specs=[
            xspec,
            pl.BlockSpec((tm, LANES), lambda i: (i, 0)),
            pl.BlockSpec((tm, d // 2), lambda i: (i, 0)),
            pl.BlockSpec((tm, d // 2), lambda i: (i + n_tiles, 0)),
            _const_spec((1, d)),
            _const_spec((d, 2 * GM_HALF)),
            _const_spec((1, GM_HALF)),
            _const_spec((GM_GROUPS, GM_SPAN, GM_SPAN)),
            _const_spec((GM_SPAN, GM_GROUPS)),
            _const_spec((GM_HALF, d)),
        ],
        out_specs=xspec,
        out_shape=jax.ShapeDtypeStruct((n_tok, d), F32),
        compiler_params=pltpu.CompilerParams(
            dimension_semantics=("parallel",), vmem_limit_bytes=VMEM_LIMIT),
        name="gmlp_layer",
    )(xt, wts, gathered, gathered, norm_g.reshape(1, d), w_in.astype(BF16),
      v_norm.reshape(1, GM_HALF),
      w_s.astype(F32), b_s.T.astype(F32), w_out.astype(BF16))
    return out.reshape(bsz, seq, d)


def _norm_proj_kernel(x_ref, g_ref, w_ref, o_ref):
    xn = _rms(x_ref[...], g_ref[...]).astype(BF16)
    o_ref[...] = jnp.dot(xn, w_ref[...], preferred_element_type=F32).astype(o_ref.dtype)


def _norm_proj(x, g, w, out_dtype, tm=512, tn=1024):
    m, d = x.shape
    n = w.shape[1]
    return pl.pallas_call(
        _norm_proj_kernel,
        grid=(m // tm, n // tn),
        in_specs=[
            pl.BlockSpec((tm, d), lambda i, j: (i, 0)),
            pl.BlockSpec((1, d), lambda i, j: (0, 0)),
            pl.BlockSpec((d, tn), lambda i, j: (0, j)),
        ],
        out_specs=pl.BlockSpec((tm, tn), lambda i, j: (i, j)),
        out_shape=jax.ShapeDtypeStruct((m, n), out_dtype),
        compiler_params=pltpu.CompilerParams(
            dimension_semantics=("parallel", "parallel"), vmem_limit_bytes=VMEM_LIMIT),
        name="norm_proj",
    )(x, g.reshape(1, d), w.astype(BF16))


def _route(x, g_ref, wr_ref, bias_ref, tri_ref, run_ref):
    tm = x.shape[0]
    xn = _rms(x, g_ref[...])
    xhi = xn.astype(BF16)
    xlo = (xn - xhi.astype(F32)).astype(BF16)
    hi = jnp.dot(xhi, wr_ref[...], preferred_element_type=F32)
    lo = jnp.dot(xlo, wr_ref[:, :LANES], preferred_element_type=F32)
    logits = hi[:, :LANES] + hi[:, LANES:] + lo + bias_ref[...]
    lane = lax.broadcasted_iota(jnp.int32, (tm, LANES), 1)
    neg = jnp.float32(-jnp.inf)

    def first_argmax(vals):
        mx = jnp.max(vals, axis=-1, keepdims=True)
        idx = jnp.min(jnp.where(vals == mx, lane, LANES), axis=-1, keepdims=True)
        return mx, idx

    gl = jnp.where(lane < MOE_GROUPS, logits, neg)
    gmax, gidx = first_argmax(gl)
    w_g = 1.0 / jnp.sum(jnp.exp(gl - gmax), axis=-1, keepdims=True)
    lo = MOE_GROUPS + MOE_PER_GROUP * gidx
    el = jnp.where((lane >= lo) & (lane < lo + MOE_PER_GROUP), logits, neg)
    m1, i1 = first_argmax(el)
    m2, i2 = first_argmax(jnp.where(lane == i1, neg, el))
    e21 = jnp.exp(m2 - m1)
    w1 = w_g / (1.0 + e21)
    w2 = w_g * e21 / (1.0 + e21)
    e1 = i1 - MOE_GROUPS
    e2 = i2 - MOE_GROUPS

    onehot = ((lane == e1) | (lane == e2)).astype(BF16)
    tot = run_ref[...] + jnp.dot(tri_ref[...], onehot, preferred_element_type=F32)
    r1 = jnp.sum(jnp.where(lane == e1, tot, 0.0), axis=-1, keepdims=True)
    r2 = jnp.sum(jnp.where(lane == e2, tot, 0.0), axis=-1, keepdims=True)
    run_ref[...] = run_ref[...] + jnp.sum(onehot.astype(F32), axis=0, keepdims=True)

    meta = jnp.where(lane == 0, e1.astype(F32), jnp.where(lane == 1, e2.astype(F32),
                     jnp.where(lane == 2, r1, jnp.where(lane == 3, r2, 0.0))))
    meta_t = jnp.transpose(meta)[:SUBLANES, :].astype(jnp.int32)
    wts = jnp.where(lane == 0, w1, jnp.where(lane == 1, w2, 0.0))
    return meta_t, wts, xn


def _xattn_router_kernel(x_ref, g_ref, wq_ref, kt_ref, v_ref, wo_ref, mg_ref, wr_ref,
                         bias_ref, tri_ref, o_ref, meta_ref, wts_ref, cnt_ref, xp_ref, run_ref):
    @pl.when((pl.program_id(0) == 0) & (pl.program_id(1) == 0))
    def _():
        run_ref[...] = jnp.zeros_like(run_ref)

    x = x_ref[0]
    xn = _rms(x, g_ref[...]).astype(BF16)
    q = jnp.dot(xn, wq_ref[...], preferred_element_type=F32) * (XA_HEAD_DIM ** -0.5)
    q = q.astype(BF16)
    heads = []
    for h in range(XA_HEADS):
        cols = slice(h * XA_HEAD_DIM, (h + 1) * XA_HEAD_DIM)
        s = jnp.dot(q[:, cols], kt_ref[0, cols, :], preferred_element_type=F32)
        e = jnp.exp(s - jnp.max(s, axis=-1, keepdims=True))
        p = (e / jnp.sum(e, axis=-1, keepdims=True)).astype(BF16)
        heads.append(jnp.dot(p, v_ref[0, :, cols], preferred_element_type=F32))
    o = jnp.concatenate(heads, axis=1).astype(BF16)
    out = x + jnp.dot(o, wo_ref[...], preferred_element_type=F32)
    o_ref[0] = out
    meta, wts, xn_moe = _route(out, mg_ref, wr_ref, bias_ref, tri_ref, run_ref)
    meta_ref[...] = meta
    wts_ref[...] = wts
    cnt_ref[...] = run_ref[...]
    xp_ref[...] = _pack_rows(xn_moe)


def _xattn_router_layer(x, mem, mem_g, norm_g, w_q, w_kv, w_o, moe_g, w_group, b_group,
                        w_expert, b_expert, tm=1024):
    bsz, seq, d = x.shape
    m = mem.shape[1]
    nt = seq // tm
    kv = _norm_proj(mem.reshape(bsz * m, d), mem_g, w_kv, BF16).reshape(bsz, m, 2 * d)
    kt = kv[..., :d].transpose(0, 2, 1)
    v = kv[..., d:]
    nr = MOE_GROUPS + MOE_EXPERTS
    w_r = jnp.zeros((d, LANES), F32).at[:, :MOE_GROUPS].set(w_group.astype(F32))
    w_r = w_r.at[:, MOE_GROUPS:nr].set(w_expert.astype(F32))
    bias = jnp.zeros((1, LANES), F32).at[0, :MOE_GROUPS].set(b_group.astype(F32))
    bias = bias.at[0, MOE_GROUPS:nr].set(b_expert.astype(F32))
    w_hi = w_r.astype(BF16)
    w_lo = (w_r - w_hi.astype(F32)).astype(BF16)
    w_hilo = jnp.concatenate([w_hi, w_lo], axis=1)
    earlier = jnp.tril(jnp.ones((tm, tm), BF16), -1)
    xspec = pl.BlockSpec((1, tm, d), lambda b, i: (b, i, 0))
    return pl.pallas_call(
        _xattn_router_kernel,
        grid=(bsz, nt),
        in_specs=[
            xspec,
            _const_spec((1, d)),
            _const_spec((d, d)),
            pl.BlockSpec((1, d, m), lambda b, i: (b, 0, 0)),
            pl.BlockSpec((1, m, d), lambda b, i: (b, 0, 0)),
            _const_spec((d, d)),
            _const_spec((1, d)),
            _const_spec((d, 2 * LANES)),
            _const_spec((1, LANES)),
            _const_spec((tm, tm)),
        ],
        out_specs=[
            xspec,
            pl.BlockSpec((SUBLANES, tm), lambda b, i: (0, b * nt + i)),
            pl.BlockSpec((tm, LANES), lambda b, i: (b * nt + i, 0)),
            pl.BlockSpec((1, LANES), lambda b, i: (0, 0)),
            pl.BlockSpec((tm, d // 2), lambda b, i: (b * nt + i, 0)),
        ],
        out_shape=[
            jax.ShapeDtypeStruct(x.shape, F32),
            jax.ShapeDtypeStruct((SUBLANES, bsz * seq), jnp.int32),
            jax.ShapeDtypeStruct((bsz * seq, LANES), F32),
            jax.ShapeDtypeStruct((1, LANES), F32),
            jax.ShapeDtypeStruct((bsz * seq, d // 2), jnp.int32),
        ],
        scratch_shapes=[pltpu.VMEM((1, LANES), F32)],
        compiler_params=pltpu.CompilerParams(
            dimension_semantics=("arbitrary", "arbitrary"), vmem_limit_bytes=VMEM_LIMIT),
        name="xattn_router",
    )(x, norm_g.reshape(1, d), w_q.astype(BF16), kt, v, w_o.astype(BF16),
      moe_g.reshape(1, d), w_hilo, bias, earlier)


PACKED = D_MODEL // 2
I32 = jnp.int32
HI_HALF = -65536
LO_HALF = 65535


def _pack_rows(v):
    lo = lax.bitcast_convert_type(v[:, :PACKED].astype(BF16).astype(F32), I32)
    hi = lax.bitcast_convert_type(v[:, PACKED:].astype(BF16).astype(F32), I32)
    return (hi & HI_HALF) | ((lo >> 16) & LO_HALF)


def _unpack_rows(p):
    lo = lax.bitcast_convert_type(p << 16, F32)
    hi = lax.bitcast_convert_type(p & HI_HALF, F32)
    return jnp.concatenate([lo, hi], axis=1)


def _expert_kernel(bstart_ref, nblk_ref, nused_ref, valid_ref, xd_hbm, wg_ref, wu_ref, wd_ref,
                   y_hbm, wgb, wub, wdb, xbuf, ybuf, lsem, ssem):
    e = pl.program_id(0)
    n_used = nused_ref[0]
    b0 = bstart_ref[e]
    nb = nblk_ref[e]

    def rows(b):
        return pl.ds(pl.multiple_of(b * MOE_BLK, MOE_BLK), MOE_BLK)

    def load(b, slot):
        return pltpu.make_async_copy(xd_hbm.at[rows(b), :], xbuf.at[slot], lsem.at[slot])

    def store(b, slot):
        return pltpu.make_async_copy(ybuf.at[slot], y_hbm.at[rows(b), :], ssem.at[slot])

    @pl.when((e == 0) & (n_used > 0))
    def _():
        load(0, 0).start()

    @pl.when(nb > 0)
    def _():
        wgb[...] = wg_ref[...].astype(BF16)
        wub[...] = wu_ref[...].astype(BF16)
        wdb[...] = wd_ref[...].astype(BF16)

    def block(k, carry):
        b = b0 + k
        slot = b % 2
        load(b, slot).wait()

        @pl.when(b + 1 < n_used)
        def _():
            load(b + 1, 1 - slot).start()

        @pl.when(b >= 2)
        def _():
            store(b - 2, slot).wait()

        row = lax.broadcasted_iota(jnp.int32, (MOE_BLK, PACKED), 0)
        xd = jnp.where(row < valid_ref[b], xbuf[slot], 0)
        xb = _unpack_rows(xd).astype(BF16)
        a = jnp.dot(xb, wgb[...], preferred_element_type=F32)
        up = jnp.dot(xb, wub[...], preferred_element_type=F32)
        h = (a * (1.0 / (1.0 + jnp.exp(-a))) * up).astype(BF16)
        ybuf[slot] = _pack_rows(jnp.dot(h, wdb[...], preferred_element_type=F32))
        store(b, slot).start()
        return carry

    lax.fori_loop(0, nb, block, 0)

    @pl.when(e == pl.num_programs(0) - 1)
    def _():
        for back in (1, 2):
            @pl.when(n_used >= back)
            def _():
                last = n_used - back
                store(last, last % 2).wait()


def _moe_experts(xd, blk_start, blk_count, n_used, valid, layer, w_gate, w_up, w_down):
    n_pad = xd.shape[0]
    d = D_MODEL

    def expert_blk(e, *_):
        return (layer, e, 0, 0)

    return pl.pallas_call(
        _expert_kernel,
        grid_spec=pltpu.PrefetchScalarGridSpec(
            num_scalar_prefetch=4,
            grid=(MOE_EXPERTS,),
            in_specs=[
                pl.BlockSpec(memory_space=pl.ANY),
                pl.BlockSpec((None, None, d, MOE_HIDDEN), expert_blk),
                pl.BlockSpec((None, None, d, MOE_HIDDEN), expert_blk),
                pl.BlockSpec((None, None, MOE_HIDDEN, d), expert_blk),
            ],
            out_specs=pl.BlockSpec(memory_space=pl.ANY),
            scratch_shapes=[pltpu.VMEM((d, MOE_HIDDEN), BF16), pltpu.VMEM((d, MOE_HIDDEN), BF16),
                            pltpu.VMEM((MOE_HIDDEN, d), BF16),
                            pltpu.VMEM((2, MOE_BLK, PACKED), I32),
                            pltpu.VMEM((2, MOE_BLK, PACKED), I32),
                            pltpu.SemaphoreType.DMA((2,)), pltpu.SemaphoreType.DMA((2,))],
        ),
        out_shape=jax.ShapeDtypeStruct((n_pad, PACKED), I32),
        compiler_params=pltpu.CompilerParams(
            dimension_semantics=("arbitrary",), vmem_limit_bytes=VMEM_LIMIT),
        name="moe_experts",
    )(blk_start.astype(jnp.int32), blk_count.astype(jnp.int32), n_used, valid, xd,
      w_gate, w_up, w_down)


def _pos_kernel(ps_ref, meta_ref, pos_ref):
    m = meta_ref[...]
    e = m[0:2, :]
    base = jnp.zeros_like(e)
    for k in range(MOE_EXPERTS):
        base = jnp.where(e == k, ps_ref[k], base)
    pos_ref[...] = base + m[2:4, :]


def _moe_positions(pad_start, meta, tm=2048):
    n_tok = meta.shape[1]
    return pl.pallas_call(
        _pos_kernel,
        grid_spec=pltpu.PrefetchScalarGridSpec(
            num_scalar_prefetch=1,
            grid=(n_tok // tm,),
            in_specs=[pl.BlockSpec((SUBLANES, tm), lambda i, ps: (0, i))],
            out_specs=pl.BlockSpec((2, tm), lambda i, ps: (0, i)),
        ),
        out_shape=jax.ShapeDtypeStruct((2, n_tok), jnp.int32),
        compiler_params=pltpu.CompilerParams(dimension_semantics=("parallel",)),
        name="moe_positions",
    )(pad_start, meta)


SC_CORES = 2
SC_SUBCORES = 16
SC_CHUNK = 64


def _sc_gather_rows(table, idx):
    n_rows = idx.shape[0]
    width = table.shape[1]
    workers = SC_CORES * SC_SUBCORES
    per_worker = n_rows // workers
    n_chunks = per_worker // SC_CHUNK
    mesh = plsc.VectorSubcoreMesh(core_axis_name="c", subcore_axis_name="s")

    def body(table_hbm, idx_hbm, out_hbm, idx_v, rows_v, sem):
        wid = lax.axis_index("s") * SC_CORES + lax.axis_index("c")
        base = wid * per_worker

        @pl.loop(0, n_chunks)
        def _(j):
            off = pl.multiple_of(base + j * SC_CHUNK, SC_CHUNK)
            pltpu.sync_copy(idx_hbm.at[pl.ds(off, SC_CHUNK)], idx_v)
            pltpu.async_copy(table_hbm.at[idx_v], rows_v, sem).wait()
            pltpu.sync_copy(rows_v, out_hbm.at[pl.ds(off, SC_CHUNK)])

    return pl.kernel(
        body,
        out_type=jax.ShapeDtypeStruct((n_rows, width), table.dtype),
        mesh=mesh,
        scratch_types=[pltpu.VMEM((SC_CHUNK,), jnp.int32),
                       pltpu.VMEM((SC_CHUNK, width), table.dtype),
                       pltpu.SemaphoreType.DMA],
        name="sc_gather_rows",
    )(table, idx)


def _sc_scatter_rows(rows, idx, n_out):
    n_rows, width = rows.shape
    workers = SC_CORES * SC_SUBCORES
    per_worker = n_rows // workers
    n_chunks = per_worker // SC_CHUNK
    mesh = plsc.VectorSubcoreMesh(core_axis_name="c", subcore_axis_name="s")

    def body(rows_hbm, idx_hbm, out_hbm, idx_v, rows_v, sem):
        wid = lax.axis_index("s") * SC_CORES + lax.axis_index("c")
        base = wid * per_worker

        @pl.loop(0, n_chunks)
        def _(j):
            off = pl.multiple_of(base + j * SC_CHUNK, SC_CHUNK)
            pltpu.sync_copy(rows_hbm.at[pl.ds(off, SC_CHUNK)], rows_v)
            for k in range(2):
                koff = pl.multiple_of(k * n_rows + off, SC_CHUNK)
                pltpu.sync_copy(idx_hbm.at[pl.ds(koff, SC_CHUNK)], idx_v)
                pltpu.async_copy(rows_v, out_hbm.at[idx_v], sem).wait()

    return pl.kernel(
        body,
        out_type=jax.ShapeDtypeStruct((n_out, width), rows.dtype),
        mesh=mesh,
        scratch_types=[pltpu.VMEM((SC_CHUNK,), jnp.int32),
                       pltpu.VMEM((SC_CHUNK, width), rows.dtype),
                       pltpu.SemaphoreType.DMA],
        name="sc_scatter_rows",
    )(rows, idx)


def _combine_kernel(x_ref, wts_ref, g1_ref, g2_ref, fg_ref, o_ref, *, final):
    w = wts_ref[...]
    out = (x_ref[...] + w[:, 0:1] * _unpack_rows(g1_ref[...])
           + w[:, 1:2] * _unpack_rows(g2_ref[...]))
    if final:
        out = _rms(out, fg_ref[...])
    o_ref[...] = out


def _moe_layer(x, meta, cnt, xp, layer, w_gate, w_up, w_down):
    n_tok = x.shape[0] * x.shape[1]

    counts = cnt[0, :MOE_EXPERTS].astype(jnp.int32)
    padded = ((counts + MOE_BLK - 1) // MOE_BLK) * MOE_BLK
    pad_end = jnp.cumsum(padded)
    pad_start = pad_end - padded
    n_blocks = (n_tok * 2) // MOE_BLK + MOE_EXPERTS
    n_pad = n_blocks * MOE_BLK
    blk_row = jnp.arange(n_blocks, dtype=jnp.int32) * MOE_BLK
    block_expert = jnp.minimum(jnp.sum(blk_row[:, None] >= pad_end[None, :], axis=1),
                               MOE_EXPERTS - 1).astype(jnp.int32)
    n_used = (pad_end[-1:] // MOE_BLK).astype(jnp.int32)
    own = block_expert[:, None] == jnp.arange(MOE_EXPERTS, dtype=jnp.int32)[None, :]
    seg_end = jnp.sum(jnp.where(own, (pad_start + counts)[None, :], 0), axis=1)
    valid = jnp.clip(seg_end - blk_row, 0, MOE_BLK).astype(jnp.int32)

    pos = _moe_positions(pad_start, meta).reshape(2 * n_tok)
    xd = _sc_scatter_rows(xp, pos, n_pad)
    y = _moe_experts(xd, pad_start // MOE_BLK, padded // MOE_BLK, n_used, valid, layer,
                     w_gate, w_up, w_down)

    return _sc_gather_rows(y, pos)


def _moe_combine(x, moe_out, final_g=None):
    bsz, seq, d = x.shape
    n_tok = bsz * seq
    xt = x.reshape(n_tok, d)
    wts, gathered = moe_out
    tm = COMBINE_TM
    n_tiles = n_tok // tm
    final = final_g is not None
    fg = (final_g if final else jnp.ones((d,), F32)).reshape(1, d)
    out = pl.pallas_call(
        functools.partial(_combine_kernel, final=final),
        grid=(n_tiles,),
        in_specs=[
            pl.BlockSpec((tm, d), lambda i: (i, 0)),
            pl.BlockSpec((tm, LANES), lambda i: (i, 0)),
            pl.BlockSpec((tm, PACKED), lambda i: (i, 0)),
            pl.BlockSpec((tm, PACKED), lambda i: (i + n_tiles, 0)),
            pl.BlockSpec((1, d), lambda i: (0, 0)),
        ],
        out_specs=pl.BlockSpec((tm, d), lambda i: (i, 0)),
        out_shape=jax.ShapeDtypeStruct((n_tok, d), F32),
        compiler_params=pltpu.CompilerParams(
            dimension_semantics=("parallel",), vmem_limit_bytes=VMEM_LIMIT),
        name="moe_combine",
    )(xt, wts, gathered, gathered, fg)
    return out.reshape(bsz, seq, d)


def kernel(x, mem, s5_norm, s5_w_in, s5_lambda_re, s5_lambda_im, s5_log_dt, s5_b_re, s5_b_im, s5_c_re, s5_c_im, s5_d, s5_w_out, gm_norm, gm_w_in, gm_v_norm, gm_w_s, gm_b_s, gm_w_out, mem_norm, xa_norm, xa_w_q, xa_w_kv, xa_w_o, moe_norm, moe_w_group, moe_b_group, moe_w_expert, moe_b_expert, moe_w_gate, moe_w_up, moe_w_down, final_norm):
    moe_out = None
    for i in range(DEPTH):
        j = i // 2
        if i % 2 == 0:
            if moe_out is not None:
                x = _moe_combine(x, moe_out)
            x = _s5_layer(x, s5_norm[j], s5_w_in[j], s5_lambda_re[j], s5_lambda_im[j],
                          s5_log_dt[j], s5_b_re[j], s5_b_im[j], s5_c_re[j], s5_c_im[j],
                          s5_d[j], s5_w_out[j])
        else:
            x = _gmlp_layer(x, moe_out, gm_norm[j], gm_w_in[j], gm_v_norm[j], gm_w_s[j],
                            gm_b_s[j], gm_w_out[j])
        x, meta, wts, cnt, xp = _xattn_router_layer(
            x, mem, mem_norm, xa_norm[i], xa_w_q[i], xa_w_kv[i], xa_w_o[i], moe_norm[i],
            moe_w_group[i], moe_b_group[i], moe_w_expert[i], moe_b_expert[i])
        moe_out = (wts, _moe_layer(x, meta, cnt, xp, i, moe_w_gate, moe_w_up, moe_w_down))
    return _moe_combine(x, moe_out, final_g=final_norm)
```

```python
import functools

import jax
import jax.numpy as jnp
from jax import lax
from jax.experimental import pallas as pl
from jax.experimental.pallas import tpu as pltpu
from jax.experimental.pallas import tpu_sc as plsc

F32 = jnp.float32
BF16 = jnp.bfloat16

D_MODEL = 1024
DEPTH = 2
CHUNK = 64
S5_GROUP_CH = 16
S5_GROUPS = 64
S5_STATE = 64
GM_HALF = 2 * D_MODEL
GM_GROUPS = 8
GM_SPAN = 128
GM_GROUP_CH = GM_HALF // GM_GROUPS
XA_HEADS = 4
XA_HEAD_DIM = D_MODEL // XA_HEADS
MOE_GROUPS = 4
MOE_PER_GROUP = 8
MOE_EXPERTS = MOE_GROUPS * MOE_PER_GROUP
MOE_HIDDEN = D_MODEL // 2
RMS_EPS = 1e-6

LANES = 128
SUBLANES = 8
VMEM_LIMIT = 56 * 1024 * 1024

S5_CT = 256
S5_NCT = D_MODEL // S5_CT
S5_STATES_CT = (S5_CT // S5_GROUP_CH) * S5_STATE
S5_SLABS = S5_STATES_CT // LANES
S5_TC = 128
S5_PITCH = S5_TC + 4
S5_SCAN_CT = 2

MOE_BLK = 512
COMBINE_TM = 512


def _rms(x, g):
    ms = jnp.mean(x * x, axis=-1, keepdims=True)
    return x * lax.rsqrt(ms + RMS_EPS) * g


def _const_spec(shape):
    nd = len(shape)
    return pl.BlockSpec(shape, lambda *_: (0,) * nd, pipeline_mode=pl.Buffered(1))


def _s5_kernel(x_ref, g_ref, win_ref, bblk_ref, cblk_ref, are_ref, aim_ref, d_ref,
               wout_ref, o_ref, hre_ref, him_ref, bre_ref, bim_ref, sre_ref, sim_ref):
    nb = x_ref.shape[0]
    tc = x_ref.shape[1]

    @pl.when(pl.program_id(0) == 0)
    def _():
        hre_ref[...] = jnp.zeros_like(hre_ref)
        him_ref[...] = jnp.zeros_like(him_ref)

    x = x_ref[...].reshape(nb * tc, D_MODEL)
    xn = _rms(x, g_ref[...]).astype(BF16)
    u = jnp.dot(xn, win_ref[...], preferred_element_type=F32)

    y_parts = []
    for ct0 in range(0, S5_NCT, S5_SCAN_CT):
        tiles = range(ct0, ct0 + S5_SCAN_CT)
        for c, ct in enumerate(tiles):
            u_ct = u[:, ct * S5_CT:(ct + 1) * S5_CT]
            bu = jnp.dot(u_ct.astype(BF16), bblk_ref[ct], preferred_element_type=F32)
            for b in range(nb):
                for j in range(S5_SLABS):
                    rows = slice(b * tc, (b + 1) * tc)
                    bre_ref[c, b, pl.ds(j * S5_PITCH, tc), :] = bu[rows, j * LANES:(j + 1) * LANES]
                    bim_ref[c, b, pl.ds(j * S5_PITCH, tc), :] = bu[
                        rows, S5_STATES_CT + j * LANES:S5_STATES_CT + (j + 1) * LANES]
        chains = [(c, ct, b) for c, ct in enumerate(tiles) for b in range(nb)]
        a_re = [are_ref[ct] for ct in tiles]
        a_im = [aim_ref[ct] for ct in tiles]

        def step(t, carry):
            idx = pl.ds(t, S5_SLABS, stride=S5_PITCH)
            bu_t = [(bre_ref[c, b, idx, :], bim_ref[c, b, idx, :]) for c, _, b in chains]
            new = []
            for k, (c, _, b) in enumerate(chains):
                hr, hi = carry[2 * k], carry[2 * k + 1]
                new.append(a_re[c] * hr - a_im[c] * hi + bu_t[k][0])
                new.append(a_re[c] * hi + a_im[c] * hr + bu_t[k][1])
            for k, (c, _, b) in enumerate(chains):
                sre_ref[c, b, idx, :] = new[2 * k]
                sim_ref[c, b, idx, :] = new[2 * k + 1]
            return tuple(new)

        init = []
        for _, ct, b in chains:
            init += [hre_ref[ct, b], him_ref[ct, b]]
        fin = lax.fori_loop(0, tc, step, tuple(init), unroll=2)
        for k, (_, ct, b) in enumerate(chains):
            hre_ref[ct, b] = fin[2 * k]
            him_ref[ct, b] = fin[2 * k + 1]

        for c, ct in enumerate(tiles):
            xs = []
            for b in range(nb):
                cols = [sre_ref[c, b, pl.ds(j * S5_PITCH, tc), :] for j in range(S5_SLABS)]
                cols += [sim_ref[c, b, pl.ds(j * S5_PITCH, tc), :] for j in range(S5_SLABS)]
                xs.append(jnp.concatenate(cols, axis=1))
            xst = jnp.concatenate(xs, axis=0).astype(BF16)
            y_parts.append(jnp.dot(xst, cblk_ref[ct], preferred_element_type=F32))

    y = jnp.concatenate(y_parts, axis=1) + d_ref[...] * u
    yg = jax.nn.gelu(y).astype(BF16)
    z = jnp.dot(yg, wout_ref[...], preferred_element_type=F32)
    out = x + z[:, :D_MODEL] * (1.0 / (1.0 + jnp.exp(-z[:, D_MODEL:])))
    o_ref[...] = out.reshape(nb, tc, D_MODEL)


def _s5_discretize(lam_re, lam_im, log_dt, b_re, b_im, c_re, c_im):
    lr = lam_re.astype(F32)
    li = lam_im.astype(F32)
    dt = jnp.exp(log_dt.astype(F32))[:, None]
    mag = jnp.exp(lr * dt)
    ab_re = mag * jnp.cos(li * dt)
    ab_im = mag * jnp.sin(li * dt)
    den = lr * lr + li * li
    coef_re = ((ab_re - 1.0) * lr + ab_im * li) / den
    coef_im = (ab_im * lr - (ab_re - 1.0) * li) / den
    br = b_re.astype(F32)
    bi = b_im.astype(F32)
    bb_re = coef_re[..., None] * br - coef_im[..., None] * bi
    bb_im = coef_re[..., None] * bi + coef_im[..., None] * br
    gpt = S5_CT // S5_GROUP_CH
    eye = jnp.eye(gpt, dtype=F32)

    def in_blocks(bb):
        t = bb.reshape(S5_NCT, gpt, S5_STATE, S5_GROUP_CH).transpose(0, 1, 3, 2)
        blk = t[:, :, :, None, :] * eye[None, :, None, :, None]
        return blk.reshape(S5_NCT, S5_CT, S5_STATES_CT)

    def out_blocks(c):
        t = c.reshape(S5_NCT, gpt, S5_GROUP_CH, S5_STATE).transpose(0, 1, 3, 2)
        blk = t[:, :, :, None, :] * eye[None, :, None, :, None]
        return blk.reshape(S5_NCT, S5_STATES_CT, S5_CT)

    bblk = jnp.concatenate([in_blocks(bb_re), in_blocks(bb_im)], axis=2).astype(BF16)
    cblk = jnp.concatenate([out_blocks(c_re.astype(F32)),
                            out_blocks(-c_im.astype(F32))], axis=1).astype(BF16)
    a_re = ab_re.reshape(S5_NCT, S5_SLABS, LANES)
    a_im = ab_im.reshape(S5_NCT, S5_SLABS, LANES)
    return bblk, cblk, a_re, a_im


def _s5_layer(x, norm_g, w_in, lam_re, lam_im, log_dt, b_re, b_im, c_re, c_im, d_skip, w_out):
    bsz, seq, d = x.shape
    bblk, cblk, a_re, a_im = _s5_discretize(lam_re, lam_im, log_dt, b_re, b_im, c_re, c_im)
    xspec = pl.BlockSpec((bsz, S5_TC, d), lambda k: (0, k, 0))
    return pl.pallas_call(
        _s5_kernel,
        grid=(seq // S5_TC,),
        in_specs=[
            xspec,
            _const_spec((1, d)),
            _const_spec((d, d)),
            _const_spec(bblk.shape),
            _const_spec(cblk.shape),
            _const_spec(a_re.shape),
            _const_spec(a_im.shape),
            _const_spec((1, d)),
            _const_spec((d, 2 * d)),
        ],
        out_specs=xspec,
        out_shape=jax.ShapeDtypeStruct(x.shape, F32),
        scratch_shapes=[
            pltpu.VMEM((S5_NCT, bsz, S5_SLABS, LANES), F32),
            pltpu.VMEM((S5_NCT, bsz, S5_SLABS, LANES), F32),
            pltpu.VMEM((S5_SCAN_CT, bsz, S5_SLABS * S5_PITCH, LANES), F32),
            pltpu.VMEM((S5_SCAN_CT, bsz, S5_SLABS * S5_PITCH, LANES), F32),
            pltpu.VMEM((S5_SCAN_CT, bsz, S5_SLABS * S5_PITCH, LANES), F32),
            pltpu.VMEM((S5_SCAN_CT, bsz, S5_SLABS * S5_PITCH, LANES), F32),
        ],
        compiler_params=pltpu.CompilerParams(
            dimension_semantics=("arbitrary",), vmem_limit_bytes=VMEM_LIMIT),
        name="s5_layer",
    )(x, norm_g.reshape(1, d), w_in.astype(BF16), bblk, cblk, a_re, a_im,
      d_skip.reshape(1, d).astype(F32), w_out.astype(BF16))


def _gmlp_kernel(x_ref, wts_ref, g1_ref, g2_ref, g_ref, win_ref, vn_ref, ws_ref, bst_ref,
                 wout_ref, o_ref):
    tm = x_ref.shape[0]
    w = wts_ref[...]
    x = (x_ref[...] + w[:, 0:1] * _unpack_rows(g1_ref[...])
         + w[:, 1:2] * _unpack_rows(g2_ref[...]))
    xn = _rms(x, g_ref[...]).astype(BF16)
    u = jax.nn.gelu(jnp.dot(xn, win_ref[:, :GM_HALF], preferred_element_type=F32))
    v = jax.nn.gelu(jnp.dot(xn, win_ref[:, GM_HALF:], preferred_element_type=F32))
    vb = _rms(v, vn_ref[...]).astype(BF16)
    row = lax.broadcasted_iota(jnp.int32, (GM_SPAN, GM_SPAN), 0) // CHUNK
    col = lax.broadcasted_iota(jnp.int32, (GM_SPAN, GM_SPAN), 1) // CHUNK
    causal = row >= col
    ws = [jnp.where(causal, ws_ref[g], 0.0).astype(BF16) for g in range(GM_GROUPS)]
    spans = []
    for s in range(tm // GM_SPAN):
        parts = []
        for g in range(GM_GROUPS):
            vblk = vb[s * GM_SPAN:(s + 1) * GM_SPAN, g * GM_GROUP_CH:(g + 1) * GM_GROUP_CH]
            parts.append(jnp.dot(ws[g], vblk, preferred_element_type=F32)
                         + bst_ref[:, g:g + 1])
        spans.append(jnp.concatenate(parts, axis=1))
    mixed = jnp.concatenate(spans, axis=0)
    p = (u * mixed).astype(BF16)
    o_ref[...] = x + jnp.dot(p, wout_ref[...], preferred_element_type=F32)


def _gmlp_layer(x, moe_out, norm_g, w_in, v_norm, w_s, b_s, w_out, tm=512):
    bsz, seq, d = x.shape
    n_tok = bsz * seq
    xt = x.reshape(n_tok, d)
    wts, gathered = moe_out
    n_tiles = n_tok // tm
    xspec = pl.BlockSpec((tm, d), lambda i: (i, 0))
    out = pl.pallas_call(
        _gmlp_kernel,
        grid=(n_tiles,),
        in_specs=[
            xspec,
            pl.BlockSpec((tm, LANES), lambda i: (i, 0)),
            pl.BlockSpec((tm, d // 2), lambda i: (i, 0)),
            pl.BlockSpec((tm, d // 2), lambda i: (i + n_tiles, 0)),
            _const_spec((1, d)),
            _const_spec((d, 2 * GM_HALF)),
            _const_spec((1, GM_HALF)),
            _const_spec((GM_GROUPS, GM_SPAN, GM_SPAN)),
            _const_spec((GM_SPAN, GM_GROUPS)),
            _const_spec((GM_HALF, d)),
        ],
        out_specs=xspec,
        out_shape=jax.ShapeDtypeStruct((n_tok, d), F32),
        compiler_params=pltpu.CompilerParams(
            dimension_semantics=("parallel",), vmem_limit_bytes=VMEM_LIMIT),
        name="gmlp_layer",
    )(xt, wts, gathered, gathered, norm_g.reshape(1, d), w_in.astype(BF16),
      v_norm.reshape(1, GM_HALF),
      w_s.astype(F32), b_s.T.astype(F32), w_out.astype(BF16))
    return out.reshape(bsz, seq, d)


def _norm_proj_kernel(x_ref, g_ref, w_ref, o_ref):
    xn = _rms(x_ref[...], g_ref[...]).astype(BF16)
    o_ref[...] = jnp.dot(xn, w_ref[...], preferred_element_type=F32).astype(o_ref.dtype)


def _norm_proj(x, g, w, out_dtype, tm=512, tn=1024):
    m, d = x.shape
    n = w.shape[1]
    return pl.pallas_call(
        _norm_proj_kernel,
        grid=(m // tm, n // tn),
        in_specs=[
            pl.BlockSpec((tm, d), lambda i, j: (i, 0)),
            pl.BlockSpec((1, d), lambda i, j: (0, 0)),
            pl.BlockSpec((d, tn), lambda i, j: (0, j)),
        ],
        out_specs=pl.BlockSpec((tm, tn), lambda i, j: (i, j)),
        out_shape=jax.ShapeDtypeStruct((m, n), out_dtype),
        compiler_params=pltpu.CompilerParams(
            dimension_semantics=("parallel", "parallel"), vmem_limit_bytes=VMEM_LIMIT),
        name="norm_proj",
    )(x, g.reshape(1, d), w.astype(BF16))


def _route(x, g_ref, wr_ref, bias_ref, tri_ref, run_ref):
    tm = x.shape[0]
    xn = _rms(x, g_ref[...])
    xhi = xn.astype(BF16)
    xlo = (xn - xhi.astype(F32)).astype(BF16)
    hi = jnp.dot(xhi, wr_ref[...], preferred_element_type=F32)
    lo = jnp.dot(xlo, wr_ref[:, :LANES], preferred_element_type=F32)
    logits = hi[:, :LANES] + hi[:, LANES:] + lo + bias_ref[...]
    lane = lax.broadcasted_iota(jnp.int32, (tm, LANES), 1)
    neg = jnp.float32(-jnp.inf)

    def first_argmax(vals):
        mx = jnp.max(vals, axis=-1, keepdims=True)
        idx = jnp.min(jnp.where(vals == mx, lane, LANES), axis=-1, keepdims=True)
        return mx, idx

    gl = jnp.where(lane < MOE_GROUPS, logits, neg)
    gmax, gidx = first_argmax(gl)
    w_g = 1.0 / jnp.sum(jnp.exp(gl - gmax), axis=-1, keepdims=True)
    lo = MOE_GROUPS + MOE_PER_GROUP * gidx
    el = jnp.where((lane >= lo) & (lane < lo + MOE_PER_GROUP), logits, neg)
    m1, i1 = first_argmax(el)
    m2, i2 = first_argmax(jnp.where(lane == i1, neg, el))
    e21 = jnp.exp(m2 - m1)
    w1 = w_g / (1.0 + e21)
    w2 = w_g * e21 / (1.0 + e21)
    e1 = i1 - MOE_GROUPS
    e2 = i2 - MOE_GROUPS

    onehot = ((lane == e1) | (lane == e2)).astype(BF16)
    tot = run_ref[...] + jnp.dot(tri_ref[...], onehot, preferred_element_type=F32)
    r1 = jnp.sum(jnp.where(lane == e1, tot, 0.0), axis=-1, keepdims=True)
    r2 = jnp.sum(jnp.where(lane == e2, tot, 0.0), axis=-1, keepdims=True)
    run_ref[...] = run_ref[...] + jnp.sum(onehot.astype(F32), axis=0, keepdims=True)

    meta = jnp.where(lane == 0, e1.astype(F32), jnp.where(lane == 1, e2.astype(F32),
                     jnp.where(lane == 2, r1, jnp.where(lane == 3, r2, 0.0))))
    meta_t = jnp.transpose(meta)[:SUBLANES, :].astype(jnp.int32)
    wts = jnp.where(lane == 0, w1, jnp.where(lane == 1, w2, 0.0))
    return meta_t, wts, xn


def _xattn_router_kernel(x_ref, g_ref, wq_ref, kt_ref, v_ref, wo_ref, mg_ref, wr_ref,
                         bias_ref, tri_ref, o_ref, meta_ref, wts_ref, cnt_ref, xp_ref, run_ref):
    @pl.when((pl.program_id(0) == 0) & (pl.program_id(1) == 0))
    def _():
        run_ref[...] = jnp.zeros_like(run_ref)

    x = x_ref[0]
    xn = _rms(x, g_ref[...]).astype(BF16)
    q = jnp.dot(xn, wq_ref[...], preferred_element_type=F32) * (XA_HEAD_DIM ** -0.5)
    q = q.astype(BF16)
    heads = []
    for h in range(XA_HEADS):
        cols = slice(h * XA_HEAD_DIM, (h + 1) * XA_HEAD_DIM)
        s = jnp.dot(q[:, cols], kt_ref[0, cols, :], preferred_element_type=F32)
        e = jnp.exp(s - jnp.max(s, axis=-1, keepdims=True))
        p = (e / jnp.sum(e, axis=-1, keepdims=True)).astype(BF16)
        heads.append(jnp.dot(p, v_ref[0, :, cols], preferred_element_type=F32))
    o = jnp.concatenate(heads, axis=1).astype(BF16)
    out = x + jnp.dot(o, wo_ref[...], preferred_element_type=F32)
    o_ref[0] = out
    meta, wts, xn_moe = _route(out, mg_ref, wr_ref, bias_ref, tri_ref, run_ref)
    meta_ref[...] = meta
    wts_ref[...] = wts
    cnt_ref[...] = run_ref[...]
    xp_ref[...] = _pack_rows(xn_moe)


def _xattn_router_layer(x, mem, mem_g, norm_g, w_q, w_kv, w_o, moe_g, w_group, b_group,
                        w_expert, b_expert, tm=1024):
    bsz, seq, d = x.shape
    m = mem.shape[1]
    nt = seq // tm
    kv = _norm_proj(mem.reshape(bsz * m, d), mem_g, w_kv, BF16).reshape(bsz, m, 2 * d)
    kt = kv[..., :d].transpose(0, 2, 1)
    v = kv[..., d:]
    nr = MOE_GROUPS + MOE_EXPERTS
    w_r = jnp.zeros((d, LANES), F32).at[:, :MOE_GROUPS].set(w_group.astype(F32))
    w_r = w_r.at[:, MOE_GROUPS:nr].set(w_expert.astype(F32))
    bias = jnp.zeros((1, LANES), F32).at[0, :MOE_GROUPS].set(b_group.astype(F32))
    bias = bias.at[0, MOE_GROUPS:nr].set(b_expert.astype(F32))
    w_hi = w_r.astype(BF16)
    w_lo = (w_r - w_hi.astype(F32)).astype(BF16)
    w_hilo = jnp.concatenate([w_hi, w_lo], axis=1)
    earlier = jnp.tril(jnp.ones((tm, tm), BF16), -1)
    xspec = pl.BlockSpec((1, tm, d), lambda b, i: (b, i, 0))
    return pl.pallas_call(
        _xattn_router_kernel,
        grid=(bsz, nt),
        in_specs=[
            xspec,
            _const_spec((1, d)),
            _const_spec((d, d)),
            pl.BlockSpec((1, d, m), lambda b, i: (b, 0, 0)),
            pl.BlockSpec((1, m, d), lambda b, i: (b, 0, 0)),
            _const_spec((d, d)),
            _const_spec((1, d)),
            _const_spec((d, 2 * LANES)),
            _const_spec((1, LANES)),
            _const_spec((tm, tm)),
        ],
        out_specs=[
            xspec,
            pl.BlockSpec((SUBLANES, tm), lambda b, i: (0, b * nt + i)),
            pl.BlockSpec((tm, LANES), lambda b, i: (b * nt + i, 0)),
            pl.BlockSpec((1, LANES), lambda b, i: (0, 0)),
            pl.BlockSpec((tm, d // 2), lambda b, i: (b * nt + i, 0)),
        ],
        out_shape=[
            jax.ShapeDtypeStruct(x.shape, F32),
            jax.ShapeDtypeStruct((SUBLANES, bsz * seq), jnp.int32),
            jax.ShapeDtypeStruct((bsz * seq, LANES), F32),
            jax.ShapeDtypeStruct((1, LANES), F32),
            jax.ShapeDtypeStruct((bsz * seq, d // 2), jnp.int32),
        ],
        scratch_shapes=[pltpu.VMEM((1, LANES), F32)],
        compiler_params=pltpu.CompilerParams(
            dimension_semantics=("arbitrary", "arbitrary"), vmem_limit_bytes=VMEM_LIMIT),
        name="xattn_router",
    )(x, norm_g.reshape(1, d), w_q.astype(BF16), kt, v, w_o.astype(BF16),
      moe_g.reshape(1, d), w_hilo, bias, earlier)


PACKED = D_MODEL // 2
I32 = jnp.int32
HI_HALF = -65536
LO_HALF = 65535


def _pack_rows(v):
    lo = lax.bitcast_convert_type(v[:, :PACKED].astype(BF16).astype(F32), I32)
    hi = lax.bitcast_convert_type(v[:, PACKED:].astype(BF16).astype(F32), I32)
    return (hi & HI_HALF) | ((lo >> 16) & LO_HALF)


def _unpack_rows(p):
    lo = lax.bitcast_convert_type(p << 16, F32)
    hi = lax.bitcast_convert_type(p & HI_HALF, F32)
    return jnp.concatenate([lo, hi], axis=1)


def _expert_kernel(bstart_ref, nblk_ref, nused_ref, valid_ref, xd_hbm, wg_ref, wu_ref, wd_ref,
                   y_hbm, wgb, wub, wdb, xbuf, ybuf, lsem, ssem):
    e = pl.program_id(0)
    n_used = nused_ref[0]
    b0 = bstart_ref[e]
    nb = nblk_ref[e]

    def rows(b):
        return pl.ds(pl.multiple_of(b * MOE_BLK, MOE_BLK), MOE_BLK)

    def load(b, slot):
        return pltpu.make_async_copy(xd_hbm.at[rows(b), :], xbuf.at[slot], lsem.at[slot])

    def store(b, slot):
        return pltpu.make_async_copy(ybuf.at[slot], y_hbm.at[rows(b), :], ssem.at[slot])

    @pl.when((e == 0) & (n_used > 0))
    def _():
        load(0, 0).start()

    @pl.when(nb > 0)
    def _():
        wgb[...] = wg_ref[...].astype(BF16)
        wub[...] = wu_ref[...].astype(BF16)
        wdb[...] = wd_ref[...].astype(BF16)

    def block(k, carry):
        b = b0 + k
        slot = b % 2
        load(b, slot).wait()

        @pl.when(b + 1 < n_used)
        def _():
            load(b + 1, 1 - slot).start()

        @pl.when(b >= 2)
        def _():
            store(b - 2, slot).wait()

        row = lax.broadcasted_iota(jnp.int32, (MOE_BLK, PACKED), 0)
        xd = jnp.where(row < valid_ref[b], xbuf[slot], 0)
        xb = _unpack_rows(xd).astype(BF16)
        a = jnp.dot(xb, wgb[...], preferred_element_type=F32)
        up = jnp.dot(xb, wub[...], preferred_element_type=F32)
        h = (a * (1.0 / (1.0 + jnp.exp(-a))) * up).astype(BF16)
        ybuf[slot] = _pack_rows(jnp.dot(h, wdb[...], preferred_element_type=F32))
        store(b, slot).start()
        return carry

    lax.fori_loop(0, nb, block, 0)

    @pl.when(e == pl.num_programs(0) - 1)
    def _():
        for back in (1, 2):
            @pl.when(n_used >= back)
            def _():
                last = n_used - back
                store(last, last % 2).wait()


def _moe_experts(xd, blk_start, blk_count, n_used, valid, layer, w_gate, w_up, w_down):
    n_pad = xd.shape[0]
    d = D_MODEL

    def expert_blk(e, *_):
        return (layer, e, 0, 0)

    return pl.pallas_call(
        _expert_kernel,
        grid_spec=pltpu.PrefetchScalarGridSpec(
            num_scalar_prefetch=4,
            grid=(MOE_EXPERTS,),
            in_specs=[
                pl.BlockSpec(memory_space=pl.ANY),
                pl.BlockSpec((None, None, d, MOE_HIDDEN), expert_blk),
                pl.BlockSpec((None, None, d, MOE_HIDDEN), expert_blk),
                pl.BlockSpec((None, None, MOE_HIDDEN, d), expert_blk),
            ],
            out_specs=pl.BlockSpec(memory_space=pl.ANY),
            scratch_shapes=[pltpu.VMEM((d, MOE_HIDDEN), BF16), pltpu.VMEM((d, MOE_HIDDEN), BF16),
                            pltpu.VMEM((MOE_HIDDEN, d), BF16),
                            pltpu.VMEM((2, MOE_BLK, PACKED), I32),
                            pltpu.VMEM((2, MOE_BLK, PACKED), I32),
                            pltpu.SemaphoreType.DMA((2,)), pltpu.SemaphoreType.DMA((2,))],
        ),
        out_shape=jax.ShapeDtypeStruct((n_pad, PACKED), I32),
        compiler_params=pltpu.CompilerParams(
            dimension_semantics=("arbitrary",), vmem_limit_bytes=VMEM_LIMIT),
        name="moe_experts",
    )(blk_start.astype(jnp.int32), blk_count.astype(jnp.int32), n_used, valid, xd,
      w_gate, w_up, w_down)


def _pos_kernel(ps_ref, meta_ref, pos_ref):
    m = meta_ref[...]
    e = m[0:2, :]
    base = jnp.zeros_like(e)
    for k in range(MOE_EXPERTS):
        base = jnp.where(e == k, ps_ref[k], base)
    pos_ref[...] = base + m[2:4, :]


def _moe_positions(pad_start, meta, tm=2048):
    n_tok = meta.shape[1]
    return pl.pallas_call(
        _pos_kernel,
        grid_spec=pltpu.PrefetchScalarGridSpec(
            num_scalar_prefetch=1,
            grid=(n_tok // tm,),
            in_specs=[pl.BlockSpec((SUBLANES, tm), lambda i, ps: (0, i))],
            out_specs=pl.BlockSpec((2, tm), lambda i, ps: (0, i)),
        ),
        out_shape=jax.ShapeDtypeStruct((2, n_tok), jnp.int32),
        compiler_params=pltpu.CompilerParams(dimension_semantics=("parallel",)),
        name="moe_positions",
    )(pad_start, meta)


SC_CORES = 2
SC_SUBCORES = 16
SC_CHUNK = 32
SC_NBUF = 4


def _sc_gather_rows(table, idx):
    n_rows = idx.shape[0]
    width = table.shape[1]
    workers = SC_CORES * SC_SUBCORES
    per_worker = n_rows // workers
    n_chunks = per_worker // SC_CHUNK
    mesh = plsc.VectorSubcoreMesh(core_axis_name="c", subcore_axis_name="s")

    def body(table_hbm, idx_hbm, out_hbm, *scratch):
        idx_v = scratch[:SC_NBUF]
        rows_v = scratch[SC_NBUF:2 * SC_NBUF]
        gsem = scratch[2 * SC_NBUF:3 * SC_NBUF]
        ssem = scratch[3 * SC_NBUF:]
        wid = lax.axis_index("s") * SC_CORES + lax.axis_index("c")
        base = wid * per_worker

        def chunk(j):
            return pl.ds(pl.multiple_of(base + j * SC_CHUNK, SC_CHUNK), SC_CHUNK)

        def gather(b):
            return pltpu.make_async_copy(table_hbm.at[idx_v[b]], rows_v[b], gsem[b])

        def start_gather(j, b):
            pltpu.sync_copy(idx_hbm.at[chunk(j)], idx_v[b])
            gather(b).start()

        def store(j, b):
            return pltpu.make_async_copy(rows_v[b], out_hbm.at[chunk(j)], ssem[b])

        for b in range(SC_NBUF):
            start_gather(b, b)

        @pl.loop(0, n_chunks, step=SC_NBUF)
        def _(g):
            for b in range(SC_NBUF):
                gather(b).wait()
                store(g + b, b).start()
            for b in range(SC_NBUF):
                nxt = g + SC_NBUF + b

                @pl.when(nxt < n_chunks)
                def _():
                    store(g + b, b).wait()
                    start_gather(nxt, b)

        for b in range(SC_NBUF):
            store(0, b).wait()

    assert n_chunks % SC_NBUF == 0 and per_worker % SC_CHUNK == 0
    return pl.kernel(
        body,
        out_type=jax.ShapeDtypeStruct((n_rows, width), table.dtype),
        mesh=mesh,
        scratch_types=([pltpu.VMEM((SC_CHUNK,), jnp.int32)] * SC_NBUF
                       + [pltpu.VMEM((SC_CHUNK, width), table.dtype)] * SC_NBUF
                       + [pltpu.SemaphoreType.DMA] * (2 * SC_NBUF)),
        name="sc_gather_rows",
    )(table, idx)


def _sc_scatter_rows(rows, idx, n_out):
    n_rows, width = rows.shape
    workers = SC_CORES * SC_SUBCORES
    per_worker = n_rows // workers
    n_chunks = per_worker // SC_CHUNK
    mesh = plsc.VectorSubcoreMesh(core_axis_name="c", subcore_axis_name="s")

    def body(rows_hbm, idx_hbm, out_hbm, *scratch):
        idx_v = scratch[:2 * SC_NBUF]
        rows_v = scratch[2 * SC_NBUF:3 * SC_NBUF]
        lsem = scratch[3 * SC_NBUF:4 * SC_NBUF]
        ssem = scratch[4 * SC_NBUF:]
        wid = lax.axis_index("s") * SC_CORES + lax.axis_index("c")
        base = wid * per_worker

        def chunk(j, k=0):
            return pl.ds(pl.multiple_of(k * n_rows + base + j * SC_CHUNK, SC_CHUNK), SC_CHUNK)

        def load(j, b):
            return pltpu.make_async_copy(rows_hbm.at[chunk(j)], rows_v[b], lsem[b])

        def scatter(b, k):
            return pltpu.make_async_copy(rows_v[b], out_hbm.at[idx_v[2 * b + k]], ssem[b])

        for b in range(SC_NBUF):
            load(b, b).start()

        @pl.loop(0, n_chunks, step=SC_NBUF)
        def _(g):
            for b in range(SC_NBUF):
                for k in range(2):
                    pltpu.sync_copy(idx_hbm.at[chunk(g + b, k)], idx_v[2 * b + k])
                load(g + b, b).wait()
                for k in range(2):
                    scatter(b, k).start()
            for b in range(SC_NBUF):
                nxt = g + SC_NBUF + b

                @pl.when(nxt < n_chunks)
                def _():
                    for k in range(2):
                        scatter(b, k).wait()
                    load(nxt, b).start()

        for b in range(SC_NBUF):
            for k in range(2):
                scatter(b, k).wait()

    assert n_chunks % SC_NBUF == 0 and per_worker % SC_CHUNK == 0
    return pl.kernel(
        body,
        out_type=jax.ShapeDtypeStruct((n_out, width), rows.dtype),
        mesh=mesh,
        scratch_types=([pltpu.VMEM((SC_CHUNK,), jnp.int32)] * (2 * SC_NBUF)
                       + [pltpu.VMEM((SC_CHUNK, width), rows.dtype)] * SC_NBUF
                       + [pltpu.SemaphoreType.DMA] * (2 * SC_NBUF)),
        name="sc_scatter_rows",
    )(rows, idx)


def _combine_kernel(x_ref, wts_ref, g1_ref, g2_ref, fg_ref, o_ref, *, final):
    w = wts_ref[...]
    out = (x_ref[...] + w[:, 0:1] * _unpack_rows(g1_ref[...])
           + w[:, 1:2] * _unpack_rows(g2_ref[...]))
    if final:
        out = _rms(out, fg_ref[...])
    o_ref[...] = out


def _moe_layer(x, meta, cnt, xp, layer, w_gate, w_up, w_down):
    n_tok = x.shape[0] * x.shape[1]

    counts = cnt[0, :MOE_EXPERTS].astype(jnp.int32)
    padded = ((counts + MOE_BLK - 1) // MOE_BLK) * MOE_BLK
    pad_end = jnp.cumsum(padded)
    pad_start = pad_end - padded
    n_blocks = (n_tok * 2) // MOE_BLK + MOE_EXPERTS
    n_pad = n_blocks * MOE_BLK
    blk_row = jnp.arange(n_blocks, dtype=jnp.int32) * MOE_BLK
    block_expert = jnp.minimum(jnp.sum(blk_row[:, None] >= pad_end[None, :], axis=1),
                               MOE_EXPERTS - 1).astype(jnp.int32)
    n_used = (pad_end[-1:] // MOE_BLK).astype(jnp.int32)
    own = block_expert[:, None] == jnp.arange(MOE_EXPERTS, dtype=jnp.int32)[None, :]
    seg_end = jnp.sum(jnp.where(own, (pad_start + counts)[None, :], 0), axis=1)
    valid = jnp.clip(seg_end - blk_row, 0, MOE_BLK).astype(jnp.int32)

    pos = _moe_positions(pad_start, meta).reshape(2 * n_tok)
    xd = _sc_scatter_rows(xp, pos, n_pad)
    y = _moe_experts(xd, pad_start // MOE_BLK, padded // MOE_BLK, n_used, valid, layer,
                     w_gate, w_up, w_down)

    return _sc_gather_rows(y, pos)


def _moe_combine(x, moe_out, final_g=None):
    bsz, seq, d = x.shape
    n_tok = bsz * seq
    xt = x.reshape(n_tok, d)
    wts, gathered = moe_out
    tm = COMBINE_TM
    n_tiles = n_tok // tm
    final = final_g is not None
    fg = (final_g if final else jnp.ones((d,), F32)).reshape(1, d)
    out = pl.pallas_call(
        functools.partial(_combine_kernel, final=final),
        grid=(n_tiles,),
        in_specs=[
            pl.BlockSpec((tm, d), lambda i: (i, 0)),
            pl.BlockSpec((tm, LANES), lambda i: (i, 0)),
            pl.BlockSpec((tm, PACKED), lambda i: (i, 0)),
            pl.BlockSpec((tm, PACKED), lambda i: (i + n_tiles, 0)),
            pl.BlockSpec((1, d), lambda i: (0, 0)),
        ],
        out_specs=pl.BlockSpec((tm, d), lambda i: (i, 0)),
        out_shape=jax.ShapeDtypeStruct((n_tok, d), F32),
        compiler_params=pltpu.CompilerParams(
            dimension_semantics=("parallel",), vmem_limit_bytes=VMEM_LIMIT),
        name="moe_combine",
    )(xt, wts, gathered, gathered, fg)
    return out.reshape(bsz, seq, d)


def kernel(x, mem, s5_norm, s5_w_in, s5_lambda_re, s5_lambda_im, s5_log_dt, s5_b_re, s5_b_im, s5_c_re, s5_c_im, s5_d, s5_w_out, gm_norm, gm_w_in, gm_v_norm, gm_w_s, gm_b_s, gm_w_out, mem_norm, xa_norm, xa_w_q, xa_w_kv, xa_w_o, moe_norm, moe_w_group, moe_b_group, moe_w_expert, moe_b_expert, moe_w_gate, moe_w_up, moe_w_down, final_norm):
    moe_out = None
    for i in range(DEPTH):
        j = i // 2
        if i % 2 == 0:
            if moe_out is not None:
                x = _moe_combine(x, moe_out)
            x = _s5_layer(x, s5_norm[j], s5_w_in[j], s5_lambda_re[j], s5_lambda_im[j],
                          s5_log_dt[j], s5_b_re[j], s5_b_im[j], s5_c_re[j], s5_c_im[j],
                          s5_d[j], s5_w_out[j])
        else:
            x = _gmlp_layer(x, moe_out, gm_norm[j], gm_w_in[j], gm_v_norm[j], gm_w_s[j],
                            gm_b_s[j], gm_w_out[j])
        x, meta, wts, cnt, xp = _xattn_router_layer(
            x, mem, mem_norm, xa_norm[i], xa_w_q[i], xa_w_kv[i], xa_w_o[i], moe_norm[i],
            moe_w_group[i], moe_b_group[i], moe_w_expert[i], moe_b_expert[i])
        moe_out = (wts, _moe_layer(x, meta, cnt, xp, i, moe_w_gate, moe_w_up, moe_w_down))
    return _moe_combine(x, moe_out, final_g=final_norm)
```

```python
import functools

import jax
import jax.numpy as jnp
from jax import lax
from jax.experimental import pallas as pl
from jax.experimental.pallas import tpu as pltpu
from jax.experimental.pallas import tpu_sc as plsc

F32 = jnp.float32
BF16 = jnp.bfloat16

D_MODEL = 1024
DEPTH = 2
CHUNK = 64
S5_GROUP_CH = 16
S5_GROUPS = 64
S5_STATE = 64
GM_HALF = 2 * D_MODEL
GM_GROUPS = 8
GM_SPAN = 128
GM_GROUP_CH = GM_HALF // GM_GROUPS
XA_HEADS = 4
XA_HEAD_DIM = D_MODEL // XA_HEADS
XA_SUB = 512
MOE_GROUPS = 4
MOE_PER_GROUP = 8
MOE_EXPERTS = MOE_GROUPS * MOE_PER_GROUP
MOE_HIDDEN = D_MODEL // 2
RMS_EPS = 1e-6

LANES = 128
SUBLANES = 8
VMEM_LIMIT = 56 * 1024 * 1024

S5_CT = 256
S5_NCT = D_MODEL // S5_CT
S5_STATES_CT = (S5_CT // S5_GROUP_CH) * S5_STATE
S5_SLABS = S5_STATES_CT // LANES
S5_TC = 128
S5_PITCH = S5_TC + 4
S5_SCAN_CT = 2

MOE_BLK = 1024
COMBINE_TM = 512


def _rms(x, g):
    ms = jnp.mean(x * x, axis=-1, keepdims=True)
    return x * lax.rsqrt(ms + RMS_EPS) * g


def _const_spec(shape):
    nd = len(shape)
    return pl.BlockSpec(shape, lambda *_: (0,) * nd, pipeline_mode=pl.Buffered(1))


def _s5_kernel(x_ref, g_ref, win_ref, bblk_ref, cblk_ref, are_ref, aim_ref, d_ref,
               wout_ref, o_ref, hre_ref, him_ref, bre_ref, bim_ref, sre_ref, sim_ref):
    nb = x_ref.shape[0]
    tc = x_ref.shape[1]

    @pl.when(pl.program_id(0) == 0)
    def _():
        hre_ref[...] = jnp.zeros_like(hre_ref)
        him_ref[...] = jnp.zeros_like(him_ref)

    x = x_ref[...].reshape(nb * tc, D_MODEL)
    xn = _rms(x, g_ref[...]).astype(BF16)
    u = jnp.dot(xn, win_ref[...], preferred_element_type=F32)

    y_parts = []
    for ct0 in range(0, S5_NCT, S5_SCAN_CT):
        tiles = range(ct0, ct0 + S5_SCAN_CT)
        for c, ct in enumerate(tiles):
            u_ct = u[:, ct * S5_CT:(ct + 1) * S5_CT]
            bu = jnp.dot(u_ct.astype(BF16), bblk_ref[ct], preferred_element_type=F32)
            for b in range(nb):
                for j in range(S5_SLABS):
                    rows = slice(b * tc, (b + 1) * tc)
                    bre_ref[c, b, pl.ds(j * S5_PITCH, tc), :] = bu[rows, j * LANES:(j + 1) * LANES]
                    bim_ref[c, b, pl.ds(j * S5_PITCH, tc), :] = bu[
                        rows, S5_STATES_CT + j * LANES:S5_STATES_CT + (j + 1) * LANES]
        chains = [(c, ct, b) for c, ct in enumerate(tiles) for b in range(nb)]
        a_re = [are_ref[ct] for ct in tiles]
        a_im = [aim_ref[ct] for ct in tiles]

        def step(t, carry):
            idx = pl.ds(t, S5_SLABS, stride=S5_PITCH)
            bu_t = [(bre_ref[c, b, idx, :], bim_ref[c, b, idx, :]) for c, _, b in chains]
            new = []
            for k, (c, _, b) in enumerate(chains):
                hr, hi = carry[2 * k], carry[2 * k + 1]
                new.append(a_re[c] * hr - a_im[c] * hi + bu_t[k][0])
                new.append(a_re[c] * hi + a_im[c] * hr + bu_t[k][1])
            for k, (c, _, b) in enumerate(chains):
                sre_ref[c, b, idx, :] = new[2 * k]
                sim_ref[c, b, idx, :] = new[2 * k + 1]
            return tuple(new)

        init = []
        for _, ct, b in chains:
            init += [hre_ref[ct, b], him_ref[ct, b]]
        fin = lax.fori_loop(0, tc, step, tuple(init), unroll=2)
        for k, (_, ct, b) in enumerate(chains):
            hre_ref[ct, b] = fin[2 * k]
            him_ref[ct, b] = fin[2 * k + 1]

        for c, ct in enumerate(tiles):
            xs = []
            for b in range(nb):
                cols = [sre_ref[c, b, pl.ds(j * S5_PITCH, tc), :] for j in range(S5_SLABS)]
                cols += [sim_ref[c, b, pl.ds(j * S5_PITCH, tc), :] for j in range(S5_SLABS)]
                xs.append(jnp.concatenate(cols, axis=1))
            xst = jnp.concatenate(xs, axis=0).astype(BF16)
            y_parts.append(jnp.dot(xst, cblk_ref[ct], preferred_element_type=F32))

    y = jnp.concatenate(y_parts, axis=1) + d_ref[...] * u
    yg = jax.nn.gelu(y).astype(BF16)
    z = jnp.dot(yg, wout_ref[...], preferred_element_type=F32)
    out = x + z[:, :D_MODEL] * (1.0 / (1.0 + jnp.exp(-z[:, D_MODEL:])))
    o_ref[...] = out.reshape(nb, tc, D_MODEL)


def _s5_discretize(lam_re, lam_im, log_dt, b_re, b_im, c_re, c_im):
    lr = lam_re.astype(F32)
    li = lam_im.astype(F32)
    dt = jnp.exp(log_dt.astype(F32))[:, None]
    mag = jnp.exp(lr * dt)
    ab_re = mag * jnp.cos(li * dt)
    ab_im = mag * jnp.sin(li * dt)
    den = lr * lr + li * li
    coef_re = ((ab_re - 1.0) * lr + ab_im * li) / den
    coef_im = (ab_im * lr - (ab_re - 1.0) * li) / den
    br = b_re.astype(F32)
    bi = b_im.astype(F32)
    bb_re = coef_re[..., None] * br - coef_im[..., None] * bi
    bb_im = coef_re[..., None] * bi + coef_im[..., None] * br
    gpt = S5_CT // S5_GROUP_CH
    eye = jnp.eye(gpt, dtype=F32)

    def in_blocks(bb):
        t = bb.reshape(S5_NCT, gpt, S5_STATE, S5_GROUP_CH).transpose(0, 1, 3, 2)
        blk = t[:, :, :, None, :] * eye[None, :, None, :, None]
        return blk.reshape(S5_NCT, S5_CT, S5_STATES_CT)

    def out_blocks(c):
        t = c.reshape(S5_NCT, gpt, S5_GROUP_CH, S5_STATE).transpose(0, 1, 3, 2)
        blk = t[:, :, :, None, :] * eye[None, :, None, :, None]
        return blk.reshape(S5_NCT, S5_STATES_CT, S5_CT)

    bblk = jnp.concatenate([in_blocks(bb_re), in_blocks(bb_im)], axis=2).astype(BF16)
    cblk = jnp.concatenate([out_blocks(c_re.astype(F32)),
                            out_blocks(-c_im.astype(F32))], axis=1).astype(BF16)
    a_re = ab_re.reshape(S5_NCT, S5_SLABS, LANES)
    a_im = ab_im.reshape(S5_NCT, S5_SLABS, LANES)
    return bblk, cblk, a_re, a_im


def _s5_layer(x, norm_g, w_in, lam_re, lam_im, log_dt, b_re, b_im, c_re, c_im, d_skip, w_out):
    bsz, seq, d = x.shape
    bblk, cblk, a_re, a_im = _s5_discretize(lam_re, lam_im, log_dt, b_re, b_im, c_re, c_im)
    xspec = pl.BlockSpec((bsz, S5_TC, d), lambda k: (0, k, 0))
    return pl.pallas_call(
        _s5_kernel,
        grid=(seq // S5_TC,),
        in_specs=[
            xspec,
            _const_spec((1, d)),
            _const_spec((d, d)),
            _const_spec(bblk.shape),
            _const_spec(cblk.shape),
            _const_spec(a_re.shape),
            _const_spec(a_im.shape),
            _const_spec((1, d)),
            _const_spec((d, 2 * d)),
        ],
        out_specs=xspec,
        out_shape=jax.ShapeDtypeStruct(x.shape, F32),
        scratch_shapes=[
            pltpu.VMEM((S5_NCT, bsz, S5_SLABS, LANES), F32),
            pltpu.VMEM((S5_NCT, bsz, S5_SLABS, LANES), F32),
            pltpu.VMEM((S5_SCAN_CT, bsz, S5_SLABS * S5_PITCH, LANES), F32),
            pltpu.VMEM((S5_SCAN_CT, bsz, S5_SLABS * S5_PITCH, LANES), F32),
            pltpu.VMEM((S5_SCAN_CT, bsz, S5_SLABS * S5_PITCH, LANES), F32),
            pltpu.VMEM((S5_SCAN_CT, bsz, S5_SLABS * S5_PITCH, LANES), F32),
        ],
        compiler_params=pltpu.CompilerParams(
            dimension_semantics=("arbitrary",), vmem_limit_bytes=VMEM_LIMIT),
        name="s5_layer",
    )(x, norm_g.reshape(1, d), w_in.astype(BF16), bblk, cblk, a_re, a_im,
      d_skip.reshape(1, d).astype(F32), w_out.astype(BF16))


def _gmlp_kernel(x_ref, wts_ref, g1_ref, g2_ref, g_ref, win_ref, vn_ref, ws_ref, bst_ref,
                 wout_ref, o_ref):
    tm = x_ref.shape[0]
    w = wts_ref[...]
    x = (x_ref[...] + w[:, 0:1] * _unpack_rows(g1_ref[...])
         + w[:, 1:2] * _unpack_rows(g2_ref[...]))
    xn = _rms(x, g_ref[...]).astype(BF16)
    u = jax.nn.gelu(jnp.dot(xn, win_ref[:, :GM_HALF], preferred_element_type=F32))
    v = jax.nn.gelu(jnp.dot(xn, win_ref[:, GM_HALF:], preferred_element_type=F32))
    vb = _rms(v, vn_ref[...]).astype(BF16)
    row = lax.broadcasted_iota(jnp.int32, (GM_SPAN, GM_SPAN), 0) // CHUNK
    col = lax.broadcasted_iota(jnp.int32, (GM_SPAN, GM_SPAN), 1) // CHUNK
    causal = row >= col
    ws = [jnp.where(causal, ws_ref[g], 0.0).astype(BF16) for g in range(GM_GROUPS)]
    spans = []
    for s in range(tm // GM_SPAN):
        parts = []
        for g in range(GM_GROUPS):
            vblk = vb[s * GM_SPAN:(s + 1) * GM_SPAN, g * GM_GROUP_CH:(g + 1) * GM_GROUP_CH]
            parts.append(jnp.dot(ws[g], vblk, preferred_element_type=F32)
                         + bst_ref[:, g:g + 1])
        spans.append(jnp.concatenate(parts, axis=1))
    mixed = jnp.concatenate(spans, axis=0)
    p = (u * mixed).astype(BF16)
    o_ref[...] = x + jnp.dot(p, wout_ref[...], preferred_element_type=F32)


def _gmlp_layer(x, moe_out, norm_g, w_in, v_norm, w_s, b_s, w_out, tm=512):
    bsz, seq, d = x.shape
    n_tok = bsz * seq
    xt = x.reshape(n_tok, d)
    wts, gathered = moe_out
    n_tiles = n_tok // tm
    xspec = pl.BlockSpec((tm, d), lambda i: (i, 0))
    out = pl.pallas_call(
        _gmlp_kernel,
        grid=(n_tiles,),
        in_specs=[
            xspec,
            pl.BlockSpec((tm, LANES), lambda i: (i, 0)),
            pl.BlockSpec((tm, d // 2), lambda i: (i, 0)),
            pl.BlockSpec((tm, d // 2), lambda i: (i + n_tiles, 0)),
            _const_spec((1, d)),
            _const_spec((d, 2 * GM_HALF)),
            _const_spec((1, GM_HALF)),
            _const_spec((GM_GROUPS, GM_SPAN, GM_SPAN)),
            _const_spec((GM_SPAN, GM_GROUPS)),
            _const_spec((GM_HALF, d)),
        ],
        out_specs=xspec,
        out_shape=jax.ShapeDtypeStruct((n_tok, d), F32),
        compiler_params=pltpu.CompilerParams(
            dimension_semantics=("parallel",), vmem_limit_bytes=VMEM_LIMIT),
        name="gmlp_layer",
    )(xt, wts, gathered, gathered, norm_g.reshape(1, d), w_in.astype(BF16),
      v_norm.reshape(1, GM_HALF),
      w_s.astype(F32), b_s.T.astype(F32), w_out.astype(BF16))
    return out.reshape(bsz, seq, d)


def _norm_proj_kernel(x_ref, g_ref, w_ref, o_ref):
    xn = _rms(x_ref[...], g_ref[...]).astype(BF16)
    o_ref[...] = jnp.dot(xn, w_ref[...], preferred_element_type=F32).astype(o_ref.dtype)


def _norm_proj(x, g, w, out_dtype, tm=512, tn=1024):
    m, d = x.shape
    n = w.shape[1]
    return pl.pallas_call(
        _norm_proj_kernel,
        grid=(m // tm, n // tn),
        in_specs=[
            pl.BlockSpec((tm, d), lambda i, j: (i, 0)),
            pl.BlockSpec((1, d), lambda i, j: (0, 0)),
            pl.BlockSpec((d, tn), lambda i, j: (0, j)),
        ],
        out_specs=pl.BlockSpec((tm, tn), lambda i, j: (i, j)),
        out_shape=jax.ShapeDtypeStruct((m, n), out_dtype),
        compiler_params=pltpu.CompilerParams(
            dimension_semantics=("parallel", "parallel"), vmem_limit_bytes=VMEM_LIMIT),
        name="norm_proj",
    )(x, g.reshape(1, d), w.astype(BF16))


def _route(x, g_ref, wr_ref, bias_ref, tri_ref, run_ref):
    tm = x.shape[0]
    xn = _rms(x, g_ref[...])
    xhi = xn.astype(BF16)
    xlo = (xn - xhi.astype(F32)).astype(BF16)
    hi = jnp.dot(xhi, wr_ref[...], preferred_element_type=F32)
    lo = jnp.dot(xlo, wr_ref[:, :LANES], preferred_element_type=F32)
    logits = hi[:, :LANES] + hi[:, LANES:] + lo + bias_ref[...]
    lane = lax.broadcasted_iota(jnp.int32, (tm, LANES), 1).astype(F32)
    neg = jnp.float32(-jnp.inf)

    def first_argmax(vals):
        mx = jnp.max(vals, axis=-1, keepdims=True)
        idx = jnp.min(jnp.where(vals == mx, lane, float(LANES)), axis=-1, keepdims=True)
        return mx, idx

    gl = jnp.where(lane < MOE_GROUPS, logits, neg)
    gmax, gidx = first_argmax(gl)
    w_g = 1.0 / jnp.sum(jnp.exp(gl - gmax), axis=-1, keepdims=True)
    first = MOE_GROUPS + MOE_PER_GROUP * gidx
    el = jnp.where((lane >= first) & (lane < first + MOE_PER_GROUP), logits, neg)
    m1, i1 = first_argmax(el)
    m2, i2 = first_argmax(jnp.where(lane == i1, neg, el))
    e21 = jnp.exp(m2 - m1)
    w1 = w_g / (1.0 + e21)
    w2 = w_g * e21 / (1.0 + e21)
    e1 = i1 - MOE_GROUPS
    e2 = i2 - MOE_GROUPS

    onehot = ((lane == e1) | (lane == e2)).astype(BF16)
    tot = run_ref[...] + jnp.dot(tri_ref[...], onehot, preferred_element_type=F32)
    r1 = jnp.sum(jnp.where(lane == e1, tot, 0.0), axis=-1, keepdims=True)
    r2 = jnp.sum(jnp.where(lane == e2, tot, 0.0), axis=-1, keepdims=True)
    run_ref[...] = run_ref[...] + jnp.sum(onehot.astype(F32), axis=0, keepdims=True)

    meta = jnp.where(lane == 0, e1, jnp.where(lane == 1, e2,
                     jnp.where(lane == 2, r1, jnp.where(lane == 3, r2, 0.0))))
    meta_t = jnp.transpose(meta)[:SUBLANES, :].astype(jnp.int32)
    wts = jnp.where(lane == 0, w1, jnp.where(lane == 1, w2, 0.0))
    return meta_t, wts, xn


def _xattn_router_kernel(x_ref, g_ref, wq_ref, kt_ref, v_ref, wo_ref, mg_ref, wr_ref,
                         bias_ref, tri_ref, o_ref, meta_ref, wts_ref, cnt_ref, xp_ref, run_ref):
    @pl.when((pl.program_id(0) == 0) & (pl.program_id(1) == 0))
    def _():
        run_ref[...] = jnp.zeros_like(run_ref)

    tm = x_ref.shape[1]
    outs = []
    for r0 in range(0, tm, XA_SUB):
        x = x_ref[0, r0:r0 + XA_SUB, :]
        xn = _rms(x, g_ref[...]).astype(BF16)
        q = jnp.dot(xn, wq_ref[...], preferred_element_type=F32) * (XA_HEAD_DIM ** -0.5)
        q = q.astype(BF16)
        heads = []
        for h in range(XA_HEADS):
            cols = slice(h * XA_HEAD_DIM, (h + 1) * XA_HEAD_DIM)
            s = jnp.dot(q[:, cols], kt_ref[0, cols, :], preferred_element_type=F32)
            e = jnp.exp(s - jnp.max(s, axis=-1, keepdims=True))
            p = (e / jnp.sum(e, axis=-1, keepdims=True)).astype(BF16)
            heads.append(jnp.dot(p, v_ref[0, :, cols], preferred_element_type=F32))
        o = jnp.concatenate(heads, axis=1).astype(BF16)
        outs.append(x + jnp.dot(o, wo_ref[...], preferred_element_type=F32))
    out = jnp.concatenate(outs, axis=0)
    o_ref[0] = out
    meta, wts, xn_moe = _route(out, mg_ref, wr_ref, bias_ref, tri_ref, run_ref)
    meta_ref[...] = meta
    wts_ref[...] = wts
    cnt_ref[...] = run_ref[...]
    xp_ref[...] = _pack_rows(xn_moe)


def _xattn_router_layer(x, mem, mem_g, norm_g, w_q, w_kv, w_o, moe_g, w_group, b_group,
                        w_expert, b_expert, tm=1024):
    bsz, seq, d = x.shape
    m = mem.shape[1]
    nt = seq // tm
    kv = _norm_proj(mem.reshape(bsz * m, d), mem_g, w_kv, BF16).reshape(bsz, m, 2 * d)
    kt = kv[..., :d].transpose(0, 2, 1)
    v = kv[..., d:]
    nr = MOE_GROUPS + MOE_EXPERTS
    w_r = jnp.zeros((d, LANES), F32).at[:, :MOE_GROUPS].set(w_group.astype(F32))
    w_r = w_r.at[:, MOE_GROUPS:nr].set(w_expert.astype(F32))
    bias = jnp.zeros((1, LANES), F32).at[0, :MOE_GROUPS].set(b_group.astype(F32))
    bias = bias.at[0, MOE_GROUPS:nr].set(b_expert.astype(F32))
    w_hi = w_r.astype(BF16)
    w_lo = (w_r - w_hi.astype(F32)).astype(BF16)
    w_hilo = jnp.concatenate([w_hi, w_lo], axis=1)
    earlier = jnp.tril(jnp.ones((tm, tm), BF16), -1)
    xspec = pl.BlockSpec((1, tm, d), lambda b, i: (b, i, 0))
    return pl.pallas_call(
        _xattn_router_kernel,
        grid=(bsz, nt),
        in_specs=[
            xspec,
            _const_spec((1, d)),
            _const_spec((d, d)),
            pl.BlockSpec((1, d, m), lambda b, i: (b, 0, 0)),
            pl.BlockSpec((1, m, d), lambda b, i: (b, 0, 0)),
            _const_spec((d, d)),
            _const_spec((1, d)),
            _const_spec((d, 2 * LANES)),
            _const_spec((1, LANES)),
            _const_spec((tm, tm)),
        ],
        out_specs=[
            xspec,
            pl.BlockSpec((SUBLANES, tm), lambda b, i: (0, b * nt + i)),
            pl.BlockSpec((tm, LANES), lambda b, i: (b * nt + i, 0)),
            pl.BlockSpec((1, LANES), lambda b, i: (0, 0)),
            pl.BlockSpec((tm, d // 2), lambda b, i: (b * nt + i, 0)),
        ],
        out_shape=[
            jax.ShapeDtypeStruct(x.shape, F32),
            jax.ShapeDtypeStruct((SUBLANES, bsz * seq), jnp.int32),
            jax.ShapeDtypeStruct((bsz * seq, LANES), F32),
            jax.ShapeDtypeStruct((1, LANES), F32),
            jax.ShapeDtypeStruct((bsz * seq, d // 2), jnp.int32),
        ],
        scratch_shapes=[pltpu.VMEM((1, LANES), F32)],
        compiler_params=pltpu.CompilerParams(
            dimension_semantics=("arbitrary", "arbitrary"), vmem_limit_bytes=VMEM_LIMIT),
        name="xattn_router",
    )(x, norm_g.reshape(1, d), w_q.astype(BF16), kt, v, w_o.astype(BF16),
      moe_g.reshape(1, d), w_hilo, bias, earlier)


PACKED = D_MODEL // 2
I32 = jnp.int32
HI_HALF = -65536
LO_HALF = 65535


def _pack_rows(v):
    lo = lax.bitcast_convert_type(v[:, :PACKED].astype(BF16).astype(F32), I32)
    hi = lax.bitcast_convert_type(v[:, PACKED:].astype(BF16).astype(F32), I32)
    return (hi & HI_HALF) | ((lo >> 16) & LO_HALF)


def _unpack_rows(p):
    lo = lax.bitcast_convert_type(p << 16, F32)
    hi = lax.bitcast_convert_type(p & HI_HALF, F32)
    return jnp.concatenate([lo, hi], axis=1)


def _expert_kernel(bstart_ref, nblk_ref, nused_ref, valid_ref, xd_hbm, wg_ref, wu_ref, wd_ref,
                   y_hbm, wgb, wub, wdb, xbuf, ybuf, lsem, ssem):
    e = pl.program_id(0)
    n_used = nused_ref[0]
    b0 = bstart_ref[e]
    nb = nblk_ref[e]

    def rows(b):
        return pl.ds(pl.multiple_of(b * MOE_BLK, MOE_BLK), MOE_BLK)

    def load(b, slot):
        return pltpu.make_async_copy(xd_hbm.at[rows(b), :], xbuf.at[slot], lsem.at[slot])

    def store(b, slot):
        return pltpu.make_async_copy(ybuf.at[slot], y_hbm.at[rows(b), :], ssem.at[slot])

    @pl.when((e == 0) & (n_used > 0))
    def _():
        load(0, 0).start()

    @pl.when(nb > 0)
    def _():
        wgb[...] = wg_ref[...].astype(BF16)
        wub[...] = wu_ref[...].astype(BF16)
        wdb[...] = wd_ref[...].astype(BF16)

    def block(k, carry):
        b = b0 + k
        slot = b % 2
        load(b, slot).wait()

        @pl.when(b + 1 < n_used)
        def _():
            load(b + 1, 1 - slot).start()

        @pl.when(b >= 2)
        def _():
            store(b - 2, slot).wait()

        row = lax.broadcasted_iota(jnp.int32, (MOE_BLK, PACKED), 0)
        xd = jnp.where(row < valid_ref[b], xbuf[slot], 0)
        xb = _unpack_rows(xd).astype(BF16)
        a = jnp.dot(xb, wgb[...], preferred_element_type=F32)
        up = jnp.dot(xb, wub[...], preferred_element_type=F32)
        h = (a * (1.0 / (1.0 + jnp.exp(-a))) * up).astype(BF16)
        ybuf[slot] = _pack_rows(jnp.dot(h, wdb[...], preferred_element_type=F32))
        store(b, slot).start()
        return carry

    lax.fori_loop(0, nb, block, 0)

    @pl.when(e == pl.num_programs(0) - 1)
    def _():
        for back in (1, 2):
            @pl.when(n_used >= back)
            def _():
                last = n_used - back
                store(last, last % 2).wait()


def _moe_experts(xd, blk_start, blk_count, n_used, valid, layer, w_gate, w_up, w_down):
    n_pad = xd.shape[0]
    d = D_MODEL

    def expert_blk(e, *_):
        return (layer, e, 0, 0)

    return pl.pallas_call(
        _expert_kernel,
        grid_spec=pltpu.PrefetchScalarGridSpec(
            num_scalar_prefetch=4,
            grid=(MOE_EXPERTS,),
            in_specs=[
                pl.BlockSpec(memory_space=pl.ANY),
                pl.BlockSpec((None, None, d, MOE_HIDDEN), expert_blk),
                pl.BlockSpec((None, None, d, MOE_HIDDEN), expert_blk),
                pl.BlockSpec((None, None, MOE_HIDDEN, d), expert_blk),
            ],
            out_specs=pl.BlockSpec(memory_space=pl.ANY),
            scratch_shapes=[pltpu.VMEM((d, MOE_HIDDEN), BF16), pltpu.VMEM((d, MOE_HIDDEN), BF16),
                            pltpu.VMEM((MOE_HIDDEN, d), BF16),
                            pltpu.VMEM((2, MOE_BLK, PACKED), I32),
                            pltpu.VMEM((2, MOE_BLK, PACKED), I32),
                            pltpu.SemaphoreType.DMA((2,)), pltpu.SemaphoreType.DMA((2,))],
        ),
        out_shape=jax.ShapeDtypeStruct((n_pad, PACKED), I32),
        compiler_params=pltpu.CompilerParams(
            dimension_semantics=("arbitrary",), vmem_limit_bytes=VMEM_LIMIT),
        name="moe_experts",
    )(blk_start.astype(jnp.int32), blk_count.astype(jnp.int32), n_used, valid, xd,
      w_gate, w_up, w_down)


def _pos_kernel(ps_ref, meta_ref, pos_ref):
    m = meta_ref[...]
    e = m[0:2, :]
    base = jnp.zeros_like(e)
    for k in range(MOE_EXPERTS):
        base = jnp.where(e == k, ps_ref[k], base)
    pos_ref[...] = base + m[2:4, :]


def _moe_positions(pad_start, meta, tm=2048):
    n_tok = meta.shape[1]
    return pl.pallas_call(
        _pos_kernel,
        grid_spec=pltpu.PrefetchScalarGridSpec(
            num_scalar_prefetch=1,
            grid=(n_tok // tm,),
            in_specs=[pl.BlockSpec((SUBLANES, tm), lambda i, ps: (0, i))],
            out_specs=pl.BlockSpec((2, tm), lambda i, ps: (0, i)),
        ),
        out_shape=jax.ShapeDtypeStruct((2, n_tok), jnp.int32),
        compiler_params=pltpu.CompilerParams(dimension_semantics=("parallel",)),
        name="moe_positions",
    )(pad_start, meta)


SC_CORES = 2
SC_SUBCORES = 16
SC_CHUNK = 32
SC_NBUF = 4


def _sc_gather_rows(table, idx):
    n_rows = idx.shape[0]
    width = table.shape[1]
    workers = SC_CORES * SC_SUBCORES
    per_worker = n_rows // workers
    n_chunks = per_worker // SC_CHUNK
    mesh = plsc.VectorSubcoreMesh(core_axis_name="c", subcore_axis_name="s")

    def body(table_hbm, idx_hbm, out_hbm, *scratch):
        idx_v = scratch[:SC_NBUF]
        rows_v = scratch[SC_NBUF:2 * SC_NBUF]
        gsem = scratch[2 * SC_NBUF:3 * SC_NBUF]
        ssem = scratch[3 * SC_NBUF:]
        wid = lax.axis_index("s") * SC_CORES + lax.axis_index("c")
        base = wid * per_worker

        def chunk(j):
            return pl.ds(pl.multiple_of(base + j * SC_CHUNK, SC_CHUNK), SC_CHUNK)

        def gather(b):
            return pltpu.make_async_copy(table_hbm.at[idx_v[b]], rows_v[b], gsem[b])

        def start_gather(j, b):
            pltpu.sync_copy(idx_hbm.at[chunk(j)], idx_v[b])
            gather(b).start()

        def store(j, b):
            return pltpu.make_async_copy(rows_v[b], out_hbm.at[chunk(j)], ssem[b])

        for b in range(SC_NBUF):
            start_gather(b, b)

        @pl.loop(0, n_chunks, step=SC_NBUF)
        def _(g):
            for b in range(SC_NBUF):
                gather(b).wait()
                store(g + b, b).start()
            for b in range(SC_NBUF):
                nxt = g + SC_NBUF + b

                @pl.when(nxt < n_chunks)
                def _():
                    store(g + b, b).wait()
                    start_gather(nxt, b)

        for b in range(SC_NBUF):
            store(0, b).wait()

    assert n_chunks % SC_NBUF == 0 and per_worker % SC_CHUNK == 0
    return pl.kernel(
        body,
        out_type=jax.ShapeDtypeStruct((n_rows, width), table.dtype),
        mesh=mesh,
        scratch_types=([pltpu.VMEM((SC_CHUNK,), jnp.int32)] * SC_NBUF
                       + [pltpu.VMEM((SC_CHUNK, width), table.dtype)] * SC_NBUF
                       + [pltpu.SemaphoreType.DMA] * (2 * SC_NBUF)),
        name="sc_gather_rows",
    )(table, idx)


def _sc_scatter_rows(rows, idx, n_out):
    n_rows, width = rows.shape
    workers = SC_CORES * SC_SUBCORES
    per_worker = n_rows // workers
    n_chunks = per_worker // SC_CHUNK
    mesh = plsc.VectorSubcoreMesh(core_axis_name="c", subcore_axis_name="s")

    def body(rows_hbm, idx_hbm, out_hbm, *scratch):
        idx_v = scratch[:2 * SC_NBUF]
        rows_v = scratch[2 * SC_NBUF:3 * SC_NBUF]
        lsem = scratch[3 * SC_NBUF:4 * SC_NBUF]
        ssem = scratch[4 * SC_NBUF:]
        wid = lax.axis_index("s") * SC_CORES + lax.axis_index("c")
        base = wid * per_worker

        def chunk(j, k=0):
            return pl.ds(pl.multiple_of(k * n_rows + base + j * SC_CHUNK, SC_CHUNK), SC_CHUNK)

        def load(j, b):
            return pltpu.make_async_copy(rows_hbm.at[chunk(j)], rows_v[b], lsem[b])

        def scatter(b, k):
            return pltpu.make_async_copy(rows_v[b], out_hbm.at[idx_v[2 * b + k]], ssem[b])

        for b in range(SC_NBUF):
            load(b, b).start()

        @pl.loop(0, n_chunks, step=SC_NBUF)
        def _(g):
            for b in range(SC_NBUF):
                for k in range(2):
                    pltpu.sync_copy(idx_hbm.at[chunk(g + b, k)], idx_v[2 * b + k])
                load(g + b, b).wait()
                for k in range(2):
                    scatter(b, k).start()
            for b in range(SC_NBUF):
                nxt = g + SC_NBUF + b

                @pl.when(nxt < n_chunks)
                def _():
                    for k in range(2):
                        scatter(b, k).wait()
                    load(nxt, b).start()

        for b in range(SC_NBUF):
            for k in range(2):
                scatter(b, k).wait()

    assert n_chunks % SC_NBUF == 0 and per_worker % SC_CHUNK == 0
    return pl.kernel(
        body,
        out_type=jax.ShapeDtypeStruct((n_out, width), rows.dtype),
        mesh=mesh,
        scratch_types=([pltpu.VMEM((SC_CHUNK,), jnp.int32)] * (2 * SC_NBUF)
                       + [pltpu.VMEM((SC_CHUNK, width), rows.dtype)] * SC_NBUF
                       + [pltpu.SemaphoreType.DMA] * (2 * SC_NBUF)),
        name="sc_scatter_rows",
    )(rows, idx)


def _combine_kernel(x_ref, wts_ref, g1_ref, g2_ref, fg_ref, o_ref, *, final):
    w = wts_ref[...]
    out = (x_ref[...] + w[:, 0:1] * _unpack_rows(g1_ref[...])
           + w[:, 1:2] * _unpack_rows(g2_ref[...]))
    if final:
        out = _rms(out, fg_ref[...])
    o_ref[...] = out


def _moe_layer(x, meta, cnt, xp, layer, w_gate, w_up, w_down):
    n_tok = x.shape[0] * x.shape[1]

    counts = cnt[0, :MOE_EXPERTS].astype(jnp.int32)
    padded = ((counts + MOE_BLK - 1) // MOE_BLK) * MOE_BLK
    pad_end = jnp.cumsum(padded)
    pad_start = pad_end - padded
    n_blocks = (n_tok * 2) // MOE_BLK + MOE_EXPERTS
    n_pad = n_blocks * MOE_BLK
    blk_row = jnp.arange(n_blocks, dtype=jnp.int32) * MOE_BLK
    block_expert = jnp.minimum(jnp.sum(blk_row[:, None] >= pad_end[None, :], axis=1),
                               MOE_EXPERTS - 1).astype(jnp.int32)
    n_used = (pad_end[-1:] // MOE_BLK).astype(jnp.int32)
    own = block_expert[:, None] == jnp.arange(MOE_EXPERTS, dtype=jnp.int32)[None, :]
    seg_end = jnp.sum(jnp.where(own, (pad_start + counts)[None, :], 0), axis=1)
    valid = jnp.clip(seg_end - blk_row, 0, MOE_BLK).astype(jnp.int32)

    pos = _moe_positions(pad_start, meta).reshape(2 * n_tok)
    xd = _sc_scatter_rows(xp, pos, n_pad)
    y = _moe_experts(xd, pad_start // MOE_BLK, padded // MOE_BLK, n_used, valid, layer,
                     w_gate, w_up, w_down)

    return _sc_gather_rows(y, pos)


def _moe_combine(x, moe_out, final_g=None):
    bsz, seq, d = x.shape
    n_tok = bsz * seq
    xt = x.reshape(n_tok, d)
    wts, gathered = moe_out
    tm = COMBINE_TM
    n_tiles = n_tok // tm
    final = final_g is not None
    fg = (final_g if final else jnp.ones((d,), F32)).reshape(1, d)
    out = pl.pallas_call(
        functools.partial(_combine_kernel, final=final),
        grid=(n_tiles,),
        in_specs=[
            pl.BlockSpec((tm, d), lambda i: (i, 0)),
            pl.BlockSpec((tm, LANES), lambda i: (i, 0)),
            pl.BlockSpec((tm, PACKED), lambda i: (i, 0)),
            pl.BlockSpec((tm, PACKED), lambda i: (i + n_tiles, 0)),
            pl.BlockSpec((1, d), lambda i: (0, 0)),
        ],
        out_specs=pl.BlockSpec((tm, d), lambda i: (i, 0)),
        out_shape=jax.ShapeDtypeStruct((n_tok, d), F32),
        compiler_params=pltpu.CompilerParams(
            dimension_semantics=("parallel",), vmem_limit_bytes=VMEM_LIMIT),
        name="moe_combine",
    )(xt, wts, gathered, gathered, fg)
    return out.reshape(bsz, seq, d)


def kernel(x, mem, s5_norm, s5_w_in, s5_lambda_re, s5_lambda_im, s5_log_dt, s5_b_re, s5_b_im, s5_c_re, s5_c_im, s5_d, s5_w_out, gm_norm, gm_w_in, gm_v_norm, gm_w_s, gm_b_s, gm_w_out, mem_norm, xa_norm, xa_w_q, xa_w_kv, xa_w_o, moe_norm, moe_w_group, moe_b_group, moe_w_expert, moe_b_expert, moe_w_gate, moe_w_up, moe_w_down, final_norm):
    moe_out = None
    for i in range(DEPTH):
        j = i // 2
        if i % 2 == 0:
            if moe_out is not None:
                x = _moe_combine(x, moe_out)
            x = _s5_layer(x, s5_norm[j], s5_w_in[j], s5_lambda_re[j], s5_lambda_im[j],
                          s5_log_dt[j], s5_b_re[j], s5_b_im[j], s5_c_re[j], s5_c_im[j],
                          s5_d[j], s5_w_out[j])
        else:
            x = _gmlp_layer(x, moe_out, gm_norm[j], gm_w_in[j], gm_v_norm[j], gm_w_s[j],
                            gm_b_s[j], gm_w_out[j])
        x, meta, wts, cnt, xp = _xattn_router_layer(
            x, mem, mem_norm, xa_norm[i], xa_w_q[i], xa_w_kv[i], xa_w_o[i], moe_norm[i],
            moe_w_group[i], moe_b_group[i], moe_w_expert[i], moe_b_expert[i])
        moe_out = (wts, _moe_layer(x, meta, cnt, xp, i, moe_w_gate, moe_w_up, moe_w_down))
    return _moe_combine(x, moe_out, final_g=final_norm)
```

```python
import functools

import jax
import jax.numpy as jnp
from jax import lax
from jax.experimental import pallas as pl
from jax.experimental.pallas import tpu as pltpu
from jax.experimental.pallas import tpu_sc as plsc

F32 = jnp.float32
BF16 = jnp.bfloat16

D_MODEL = 1024
DEPTH = 2
CHUNK = 64
S5_GROUP_CH = 16
S5_GROUPS = 64
S5_STATE = 64
GM_HALF = 2 * D_MODEL
GM_GROUPS = 8
GM_SPAN = 128
GM_GROUP_CH = GM_HALF // GM_GROUPS
XA_HEADS = 4
XA_HEAD_DIM = D_MODEL // XA_HEADS
XA_SUB = 512
MOE_GROUPS = 4
MOE_PER_GROUP = 8
MOE_EXPERTS = MOE_GROUPS * MOE_PER_GROUP
MOE_HIDDEN = D_MODEL // 2
RMS_EPS = 1e-6

LANES = 128
SUBLANES = 8
VMEM_LIMIT = 56 * 1024 * 1024

S5_CT = 256
S5_NCT = D_MODEL // S5_CT
S5_STATES_CT = (S5_CT // S5_GROUP_CH) * S5_STATE
S5_SLABS = S5_STATES_CT // LANES
S5_TC = 128
S5_PITCH = S5_TC + 4
S5_SCAN_CT = 2

MOE_BLK = 256
MOE_UNIT = 1024
MOE_TAILS = (512, 256)
COMBINE_TM = 512


def _rms(x, g):
    ms = jnp.mean(x * x, axis=-1, keepdims=True)
    return x * lax.rsqrt(ms + RMS_EPS) * g


def _const_spec(shape):
    nd = len(shape)
    return pl.BlockSpec(shape, lambda *_: (0,) * nd, pipeline_mode=pl.Buffered(1))


def _s5_kernel(x_ref, g_ref, win_ref, bblk_ref, cblk_ref, are_ref, aim_ref, d_ref,
               wout_ref, o_ref, hre_ref, him_ref, bre_ref, bim_ref, sre_ref, sim_ref):
    nb = x_ref.shape[0]
    tc = x_ref.shape[1]

    @pl.when(pl.program_id(0) == 0)
    def _():
        hre_ref[...] = jnp.zeros_like(hre_ref)
        him_ref[...] = jnp.zeros_like(him_ref)

    x = x_ref[...].reshape(nb * tc, D_MODEL)
    xn = _rms(x, g_ref[...]).astype(BF16)
    u = jnp.dot(xn, win_ref[...], preferred_element_type=F32)

    y_parts = []
    for ct0 in range(0, S5_NCT, S5_SCAN_CT):
        tiles = range(ct0, ct0 + S5_SCAN_CT)
        for c, ct in enumerate(tiles):
            u_ct = u[:, ct * S5_CT:(ct + 1) * S5_CT]
            bu = jnp.dot(u_ct.astype(BF16), bblk_ref[ct], preferred_element_type=F32)
            for b in range(nb):
                for j in range(S5_SLABS):
                    rows = slice(b * tc, (b + 1) * tc)
                    bre_ref[c, b, pl.ds(j * S5_PITCH, tc), :] = bu[rows, j * LANES:(j + 1) * LANES]
                    bim_ref[c, b, pl.ds(j * S5_PITCH, tc), :] = bu[
                        rows, S5_STATES_CT + j * LANES:S5_STATES_CT + (j + 1) * LANES]
        chains = [(c, ct, b) for c, ct in enumerate(tiles) for b in range(nb)]
        a_re = [are_ref[ct] for ct in tiles]
        a_im = [aim_ref[ct] for ct in tiles]

        def step(t, carry):
            idx = pl.ds(t, S5_SLABS, stride=S5_PITCH)
            bu_t = [(bre_ref[c, b, idx, :], bim_ref[c, b, idx, :]) for c, _, b in chains]
            new = []
            for k, (c, _, b) in enumerate(chains):
                hr, hi = carry[2 * k], carry[2 * k + 1]
                new.append(a_re[c] * hr - a_im[c] * hi + bu_t[k][0])
                new.append(a_re[c] * hi + a_im[c] * hr + bu_t[k][1])
            for k, (c, _, b) in enumerate(chains):
                sre_ref[c, b, idx, :] = new[2 * k]
                sim_ref[c, b, idx, :] = new[2 * k + 1]
            return tuple(new)

        init = []
        for _, ct, b in chains:
            init += [hre_ref[ct, b], him_ref[ct, b]]
        fin = lax.fori_loop(0, tc, step, tuple(init), unroll=2)
        for k, (_, ct, b) in enumerate(chains):
            hre_ref[ct, b] = fin[2 * k]
            him_ref[ct, b] = fin[2 * k + 1]

        for c, ct in enumerate(tiles):
            xs = []
            for b in range(nb):
                cols = [sre_ref[c, b, pl.ds(j * S5_PITCH, tc), :] for j in range(S5_SLABS)]
                cols += [sim_ref[c, b, pl.ds(j * S5_PITCH, tc), :] for j in range(S5_SLABS)]
                xs.append(jnp.concatenate(cols, axis=1))
            xst = jnp.concatenate(xs, axis=0).astype(BF16)
            y_parts.append(jnp.dot(xst, cblk_ref[ct], preferred_element_type=F32))

    y = jnp.concatenate(y_parts, axis=1) + d_ref[...] * u
    yg = jax.nn.gelu(y).astype(BF16)
    z = jnp.dot(yg, wout_ref[...], preferred_element_type=F32)
    out = x + z[:, :D_MODEL] * (1.0 / (1.0 + jnp.exp(-z[:, D_MODEL:])))
    o_ref[...] = out.reshape(nb, tc, D_MODEL)


def _s5_discretize(lam_re, lam_im, log_dt, b_re, b_im, c_re, c_im):
    lr = lam_re.astype(F32)
    li = lam_im.astype(F32)
    dt = jnp.exp(log_dt.astype(F32))[:, None]
    mag = jnp.exp(lr * dt)
    ab_re = mag * jnp.cos(li * dt)
    ab_im = mag * jnp.sin(li * dt)
    den = lr * lr + li * li
    coef_re = ((ab_re - 1.0) * lr + ab_im * li) / den
    coef_im = (ab_im * lr - (ab_re - 1.0) * li) / den
    br = b_re.astype(F32)
    bi = b_im.astype(F32)
    bb_re = coef_re[..., None] * br - coef_im[..., None] * bi
    bb_im = coef_re[..., None] * bi + coef_im[..., None] * br
    gpt = S5_CT // S5_GROUP_CH
    ch_group = jnp.arange(S5_CT, dtype=jnp.int32) // S5_GROUP_CH
    st_group = jnp.arange(S5_STATES_CT, dtype=jnp.int32) // S5_STATE
    same = ch_group[:, None] == st_group[None, :]

    def in_blocks(bb):
        t = bb.reshape(S5_NCT, gpt, S5_STATE, S5_GROUP_CH).transpose(0, 3, 1, 2)
        t = t.reshape(S5_NCT, 1, S5_GROUP_CH, S5_STATES_CT)
        t = jnp.broadcast_to(t, (S5_NCT, gpt, S5_GROUP_CH, S5_STATES_CT))
        return jnp.where(same[None], t.reshape(S5_NCT, S5_CT, S5_STATES_CT), 0.0)

    def out_blocks(c):
        return in_blocks(c.transpose(0, 2, 1)).transpose(0, 2, 1)

    bblk = jnp.concatenate([in_blocks(bb_re), in_blocks(bb_im)], axis=2).astype(BF16)
    cblk = jnp.concatenate([out_blocks(c_re.astype(F32)),
                            out_blocks(-c_im.astype(F32))], axis=1).astype(BF16)
    a_re = ab_re.reshape(S5_NCT, S5_SLABS, LANES)
    a_im = ab_im.reshape(S5_NCT, S5_SLABS, LANES)
    return bblk, cblk, a_re, a_im


def _s5_layer(x, norm_g, w_in, lam_re, lam_im, log_dt, b_re, b_im, c_re, c_im, d_skip, w_out):
    bsz, seq, d = x.shape
    bblk, cblk, a_re, a_im = _s5_discretize(lam_re, lam_im, log_dt, b_re, b_im, c_re, c_im)
    xspec = pl.BlockSpec((bsz, S5_TC, d), lambda k: (0, k, 0))
    return pl.pallas_call(
        _s5_kernel,
        grid=(seq // S5_TC,),
        in_specs=[
            xspec,
            _const_spec((1, d)),
            _const_spec((d, d)),
            _const_spec(bblk.shape),
            _const_spec(cblk.shape),
            _const_spec(a_re.shape),
            _const_spec(a_im.shape),
            _const_spec((1, d)),
            _const_spec((d, 2 * d)),
        ],
        out_specs=xspec,
        out_shape=jax.ShapeDtypeStruct(x.shape, F32),
        scratch_shapes=[
            pltpu.VMEM((S5_NCT, bsz, S5_SLABS, LANES), F32),
            pltpu.VMEM((S5_NCT, bsz, S5_SLABS, LANES), F32),
            pltpu.VMEM((S5_SCAN_CT, bsz, S5_SLABS * S5_PITCH, LANES), F32),
            pltpu.VMEM((S5_SCAN_CT, bsz, S5_SLABS * S5_PITCH, LANES), F32),
            pltpu.VMEM((S5_SCAN_CT, bsz, S5_SLABS * S5_PITCH, LANES), F32),
            pltpu.VMEM((S5_SCAN_CT, bsz, S5_SLABS * S5_PITCH, LANES), F32),
        ],
        compiler_params=pltpu.CompilerParams(
            dimension_semantics=("arbitrary",), vmem_limit_bytes=VMEM_LIMIT),
        name="s5_layer",
    )(x, norm_g.reshape(1, d), w_in.astype(BF16), bblk, cblk, a_re, a_im,
      d_skip.reshape(1, d).astype(F32), w_out.astype(BF16))


def _gmlp_kernel(x_ref, wts_ref, g1_ref, g2_ref, g_ref, win_ref, vn_ref, ws_ref, bst_ref,
                 wout_ref, o_ref):
    tm = x_ref.shape[0]
    w = wts_ref[...]
    x = (x_ref[...] + w[:, 0:1] * _unpack_rows(g1_ref[...])
         + w[:, 1:2] * _unpack_rows(g2_ref[...]))
    xn = _rms(x, g_ref[...]).astype(BF16)
    u = jax.nn.gelu(jnp.dot(xn, win_ref[:, :GM_HALF], preferred_element_type=F32))
    v = jax.nn.gelu(jnp.dot(xn, win_ref[:, GM_HALF:], preferred_element_type=F32))
    vb = _rms(v, vn_ref[...]).astype(BF16)
    row = lax.broadcasted_iota(jnp.int32, (GM_SPAN, GM_SPAN), 0) // CHUNK
    col = lax.broadcasted_iota(jnp.int32, (GM_SPAN, GM_SPAN), 1) // CHUNK
    causal = row >= col
    ws = [jnp.where(causal, ws_ref[g], 0.0).astype(BF16) for g in range(GM_GROUPS)]
    spans = []
    for s in range(tm // GM_SPAN):
        parts = []
        for g in range(GM_GROUPS):
            vblk = vb[s * GM_SPAN:(s + 1) * GM_SPAN, g * GM_GROUP_CH:(g + 1) * GM_GROUP_CH]
            parts.append(jnp.dot(ws[g], vblk, preferred_element_type=F32)
                         + bst_ref[:, g:g + 1])
        spans.append(jnp.concatenate(parts, axis=1))
    mixed = jnp.concatenate(spans, axis=0)
    p = (u * mixed).astype(BF16)
    o_ref[...] = x + jnp.dot(p, wout_ref[...], preferred_element_type=F32)


def _gmlp_layer(x, moe_out, norm_g, w_in, v_norm, w_s, b_s, w_out, tm=512):
    bsz, seq, d = x.shape
    n_tok = bsz * seq
    xt = x.reshape(n_tok, d)
    wts, gathered = moe_out
    n_tiles = n_tok // tm
    xspec = pl.BlockSpec((tm, d), lambda i: (i, 0))
    out = pl.pallas_call(
        _gmlp_kernel,
        grid=(n_tiles,),
        in_specs=[
            xspec,
            pl.BlockSpec((tm, LANES), lambda i: (i, 0)),
            pl.BlockSpec((tm, d // 2), lambda i: (i, 0)),
            pl.BlockSpec((tm, d // 2), lambda i: (i + n_tiles, 0)),
            _const_spec((1, d)),
            _const_spec((d, 2 * GM_HALF)),
            _const_spec((1, GM_HALF)),
            _const_spec((GM_GROUPS, GM_SPAN, GM_SPAN)),
            _const_spec((GM_SPAN, GM_GROUPS)),
            _const_spec((GM_HALF, d)),
        ],
        out_specs=xspec,
        out_shape=jax.ShapeDtypeStruct((n_tok, d), F32),
        compiler_params=pltpu.CompilerParams(
            dimension_semantics=("parallel",), vmem_limit_bytes=VMEM_LIMIT),
        name="gmlp_layer",
    )(xt, wts, gathered, gathered, norm_g.reshape(1, d), w_in.astype(BF16),
      v_norm.reshape(1, GM_HALF),
      w_s.astype(F32), b_s.T.astype(F32), w_out.astype(BF16))
    return out.reshape(bsz, seq, d)


def _norm_proj_kernel(x_ref, g_ref, w_ref, o_ref):
    xn = _rms(x_ref[...], g_ref[...]).astype(BF16)
    o_ref[...] = jnp.dot(xn, w_ref[...], preferred_element_type=F32).astype(o_ref.dtype)


def _norm_proj(x, g, w, out_dtype, tm=512, tn=1024):
    m, d = x.shape
    n = w.shape[1]
    return pl.pallas_call(
        _norm_proj_kernel,
        grid=(m // tm, n // tn),
        in_specs=[
            pl.BlockSpec((tm, d), lambda i, j: (i, 0)),
            pl.BlockSpec((1, d), lambda i, j: (0, 0)),
            pl.BlockSpec((d, tn), lambda i, j: (0, j)),
        ],
        out_specs=pl.BlockSpec((tm, tn), lambda i, j: (i, j)),
        out_shape=jax.ShapeDtypeStruct((m, n), out_dtype),
        compiler_params=pltpu.CompilerParams(
            dimension_semantics=("parallel", "parallel"), vmem_limit_bytes=VMEM_LIMIT),
        name="norm_proj",
    )(x, g.reshape(1, d), w.astype(BF16))


def _route(x, g_ref, wr_ref, bias_ref, tri_ref, run_ref):
    tm = x.shape[0]
    xn = _rms(x, g_ref[...])
    xhi = xn.astype(BF16)
    xlo = (xn - xhi.astype(F32)).astype(BF16)
    hi = jnp.dot(xhi, wr_ref[...], preferred_element_type=F32)
    lo = jnp.dot(xlo, wr_ref[:, :LANES], preferred_element_type=F32)
    logits = hi[:, :LANES] + hi[:, LANES:] + lo + bias_ref[...]
    lane = lax.broadcasted_iota(jnp.int32, (tm, LANES), 1).astype(F32)
    neg = jnp.float32(-jnp.inf)

    def first_argmax(vals):
        mx = jnp.max(vals, axis=-1, keepdims=True)
        idx = jnp.min(jnp.where(vals == mx, lane, float(LANES)), axis=-1, keepdims=True)
        return mx, idx

    gl = jnp.where(lane < MOE_GROUPS, logits, neg)
    gmax, gidx = first_argmax(gl)
    w_g = 1.0 / jnp.sum(jnp.exp(gl - gmax), axis=-1, keepdims=True)
    first = MOE_GROUPS + MOE_PER_GROUP * gidx
    el = jnp.where((lane >= first) & (lane < first + MOE_PER_GROUP), logits, neg)
    m1, i1 = first_argmax(el)
    m2, i2 = first_argmax(jnp.where(lane == i1, neg, el))
    e21 = jnp.exp(m2 - m1)
    w1 = w_g / (1.0 + e21)
    w2 = w_g * e21 / (1.0 + e21)
    e1 = i1 - MOE_GROUPS
    e2 = i2 - MOE_GROUPS

    onehot = ((lane == e1) | (lane == e2)).astype(BF16)
    tot = run_ref[...] + jnp.dot(tri_ref[...], onehot, preferred_element_type=F32)
    r1 = jnp.sum(jnp.where(lane == e1, tot, 0.0), axis=-1, keepdims=True)
    r2 = jnp.sum(jnp.where(lane == e2, tot, 0.0), axis=-1, keepdims=True)
    run_ref[...] = run_ref[...] + jnp.sum(onehot.astype(F32), axis=0, keepdims=True)

    meta = jnp.where(lane == 0, e1, jnp.where(lane == 1, e2,
                     jnp.where(lane == 2, r1, jnp.where(lane == 3, r2, 0.0))))
    meta_t = jnp.transpose(meta)[:SUBLANES, :].astype(jnp.int32)
    wts = jnp.where(lane == 0, w1, jnp.where(lane == 1, w2, 0.0))
    return meta_t, wts, xn


def _xattn_router_kernel(x_ref, g_ref, wq_ref, kt_ref, v_ref, wo_ref, mg_ref, wr_ref,
                         bias_ref, tri_ref, o_ref, meta_ref, wts_ref, cnt_ref, xp_ref, run_ref):
    @pl.when((pl.program_id(0) == 0) & (pl.program_id(1) == 0))
    def _():
        run_ref[...] = jnp.zeros_like(run_ref)

    tm = x_ref.shape[1]
    outs = []
    for r0 in range(0, tm, XA_SUB):
        x = x_ref[0, r0:r0 + XA_SUB, :]
        xn = _rms(x, g_ref[...]).astype(BF16)
        q = jnp.dot(xn, wq_ref[...], preferred_element_type=F32) * (XA_HEAD_DIM ** -0.5)
        q = q.astype(BF16)
        heads = []
        for h in range(XA_HEADS):
            cols = slice(h * XA_HEAD_DIM, (h + 1) * XA_HEAD_DIM)
            s = jnp.dot(q[:, cols], kt_ref[0, cols, :], preferred_element_type=F32)
            e = jnp.exp(s - jnp.max(s, axis=-1, keepdims=True))
            p = (e / jnp.sum(e, axis=-1, keepdims=True)).astype(BF16)
            heads.append(jnp.dot(p, v_ref[0, :, cols], preferred_element_type=F32))
        o = jnp.concatenate(heads, axis=1).astype(BF16)
        outs.append(x + jnp.dot(o, wo_ref[...], preferred_element_type=F32))
    out = jnp.concatenate(outs, axis=0)
    o_ref[0] = out
    meta, wts, xn_moe = _route(out, mg_ref, wr_ref, bias_ref, tri_ref, run_ref)
    meta_ref[...] = meta
    wts_ref[...] = wts
    cnt_ref[...] = run_ref[...]
    xp_ref[...] = _pack_rows(xn_moe)


def _xattn_router_layer(x, mem, mem_g, norm_g, w_q, w_kv, w_o, moe_g, w_group, b_group,
                        w_expert, b_expert, tm=1024):
    bsz, seq, d = x.shape
    m = mem.shape[1]
    nt = seq // tm
    kv = _norm_proj(mem.reshape(bsz * m, d), mem_g, w_kv, BF16).reshape(bsz, m, 2 * d)
    kt = kv[..., :d].transpose(0, 2, 1)
    v = kv[..., d:]
    nr = MOE_GROUPS + MOE_EXPERTS
    w_r = jnp.zeros((d, LANES), F32).at[:, :MOE_GROUPS].set(w_group.astype(F32))
    w_r = w_r.at[:, MOE_GROUPS:nr].set(w_expert.astype(F32))
    bias = jnp.zeros((1, LANES), F32).at[0, :MOE_GROUPS].set(b_group.astype(F32))
    bias = bias.at[0, MOE_GROUPS:nr].set(b_expert.astype(F32))
    w_hi = w_r.astype(BF16)
    w_lo = (w_r - w_hi.astype(F32)).astype(BF16)
    w_hilo = jnp.concatenate([w_hi, w_lo], axis=1)
    earlier = jnp.tril(jnp.ones((tm, tm), BF16), -1)
    xspec = pl.BlockSpec((1, tm, d), lambda b, i: (b, i, 0))
    return pl.pallas_call(
        _xattn_router_kernel,
        grid=(bsz, nt),
        in_specs=[
            xspec,
            _const_spec((1, d)),
            _const_spec((d, d)),
            pl.BlockSpec((1, d, m), lambda b, i: (b, 0, 0)),
            pl.BlockSpec((1, m, d), lambda b, i: (b, 0, 0)),
            _const_spec((d, d)),
            _const_spec((1, d)),
            _const_spec((d, 2 * LANES)),
            _const_spec((1, LANES)),
            _const_spec((tm, tm)),
        ],
        out_specs=[
            xspec,
            pl.BlockSpec((SUBLANES, tm), lambda b, i: (0, b * nt + i)),
            pl.BlockSpec((tm, LANES), lambda b, i: (b * nt + i, 0)),
            pl.BlockSpec((1, LANES), lambda b, i: (0, 0)),
            pl.BlockSpec((tm, d // 2), lambda b, i: (b * nt + i, 0)),
        ],
        out_shape=[
            jax.ShapeDtypeStruct(x.shape, F32),
            jax.ShapeDtypeStruct((SUBLANES, bsz * seq), jnp.int32),
            jax.ShapeDtypeStruct((bsz * seq, LANES), F32),
            jax.ShapeDtypeStruct((1, LANES), F32),
            jax.ShapeDtypeStruct((bsz * seq, d // 2), jnp.int32),
        ],
        scratch_shapes=[pltpu.VMEM((1, LANES), F32)],
        compiler_params=pltpu.CompilerParams(
            dimension_semantics=("arbitrary", "arbitrary"), vmem_limit_bytes=VMEM_LIMIT),
        name="xattn_router",
    )(x, norm_g.reshape(1, d), w_q.astype(BF16), kt, v, w_o.astype(BF16),
      moe_g.reshape(1, d), w_hilo, bias, earlier)


PACKED = D_MODEL // 2
I32 = jnp.int32
HI_HALF = -65536
LO_HALF = 65535


def _pack_rows(v):
    lo = lax.bitcast_convert_type(v[:, :PACKED].astype(BF16).astype(F32), I32)
    hi = lax.bitcast_convert_type(v[:, PACKED:].astype(BF16).astype(F32), I32)
    return (hi & HI_HALF) | ((lo >> 16) & LO_HALF)


def _unpack_rows(p):
    lo = lax.bitcast_convert_type(p << 16, F32)
    hi = lax.bitcast_convert_type(p & HI_HALF, F32)
    return jnp.concatenate([lo, hi], axis=1)


def _expert_kernel(row0_ref, cnt_ref, nfull_ref, rem_ref, ustart_ref, used_ref, xd_hbm, wg_ref,
                   wu_ref, wd_ref, y_hbm, wgb, wub, wdb, xbuf, ybuf, cls_ref, lsem, ssem):
    e = pl.program_id(0)
    used_rows = used_ref[0]
    row0 = row0_ref[e]
    cnt = cnt_ref[e]
    nfull = nfull_ref[e]
    rem = rem_ref[e]
    u0 = ustart_ref[e]
    sizes = (MOE_UNIT,) + MOE_TAILS

    def load(r, slot):
        src = xd_hbm.at[pl.ds(pl.multiple_of(r, MOE_BLK), MOE_UNIT), :]
        return pltpu.make_async_copy(src, xbuf.at[slot], lsem.at[slot])

    def store(r, slot, size):
        dst = y_hbm.at[pl.ds(pl.multiple_of(r, MOE_BLK), size), :]
        return pltpu.make_async_copy(ybuf.at[slot, pl.ds(0, size), :], dst, ssem.at[slot])

    def wait_store(slot):
        for c, size in enumerate(sizes):
            @pl.when(cls_ref[slot] == c + 1)
            def _():
                store(0, slot, size).wait()

    @pl.when(e == 0)
    def _():
        cls_ref[0] = 0
        cls_ref[1] = 0

        @pl.when(used_rows > 0)
        def _():
            load(0, 0).start()

    @pl.when(nfull + rem > 0)
    def _():
        wgb[...] = wg_ref[...].astype(BF16)
        wub[...] = wu_ref[...].astype(BF16)
        wdb[...] = wd_ref[...].astype(BF16)

    def unit(u, r, c):
        size = sizes[c]
        slot = u % 2
        load(r, slot).wait()

        @pl.when(r + size < used_rows)
        def _():
            load(r + size, 1 - slot).start()

        wait_store(slot)
        row = lax.broadcasted_iota(jnp.int32, (size, PACKED), 0)
        xd = jnp.where(row < cnt - (r - row0), xbuf[slot, pl.ds(0, size), :], 0)
        xb = _unpack_rows(xd).astype(BF16)
        a = jnp.dot(xb, wgb[...], preferred_element_type=F32)
        up = jnp.dot(xb, wub[...], preferred_element_type=F32)
        h = (a * (1.0 / (1.0 + jnp.exp(-a))) * up).astype(BF16)
        ybuf[slot, pl.ds(0, size), :] = _pack_rows(
            jnp.dot(h, wdb[...], preferred_element_type=F32))
        store(r, slot, size).start()
        cls_ref[slot] = c + 1

    def full_unit(k, carry):
        unit(u0 + k, row0 + k * MOE_UNIT, 0)
        return carry

    lax.fori_loop(0, nfull, full_unit, 0)

    u = u0 + nfull
    r = row0 + nfull * MOE_UNIT
    for c, size in enumerate(MOE_TAILS, start=1):
        has = (rem & (size // MOE_BLK)) != 0

        @pl.when(has)
        def _():
            unit(u, r, c)
        u = u + has.astype(jnp.int32)
        r = r + has.astype(jnp.int32) * size

    @pl.when(e == pl.num_programs(0) - 1)
    def _():
        wait_store(0)
        wait_store(1)


def _moe_experts(xd, seg_start, counts, layer, w_gate, w_up, w_down):
    n_pad = xd.shape[0]
    d = D_MODEL
    nblk = (counts + MOE_BLK - 1) // MOE_BLK
    per_unit = MOE_UNIT // MOE_BLK
    nfull = nblk // per_unit
    rem = nblk % per_unit
    n_units = nfull + sum(((rem & (size // MOE_BLK)) != 0).astype(jnp.int32)
                          for size in MOE_TAILS)
    ustart = jnp.cumsum(n_units) - n_units
    used_rows = (seg_start[-1:] + nblk[-1:] * MOE_BLK).astype(jnp.int32)

    def expert_blk(e, *_):
        return (layer, e, 0, 0)

    return pl.pallas_call(
        _expert_kernel,
        grid_spec=pltpu.PrefetchScalarGridSpec(
            num_scalar_prefetch=6,
            grid=(MOE_EXPERTS,),
            in_specs=[
                pl.BlockSpec(memory_space=pl.ANY),
                pl.BlockSpec((None, None, d, MOE_HIDDEN), expert_blk),
                pl.BlockSpec((None, None, d, MOE_HIDDEN), expert_blk),
                pl.BlockSpec((None, None, MOE_HIDDEN, d), expert_blk),
            ],
            out_specs=pl.BlockSpec(memory_space=pl.ANY),
            scratch_shapes=[pltpu.VMEM((d, MOE_HIDDEN), BF16), pltpu.VMEM((d, MOE_HIDDEN), BF16),
                            pltpu.VMEM((MOE_HIDDEN, d), BF16),
                            pltpu.VMEM((2, MOE_UNIT, PACKED), I32),
                            pltpu.VMEM((2, MOE_UNIT, PACKED), I32),
                            pltpu.SMEM((2,), jnp.int32),
                            pltpu.SemaphoreType.DMA((2,)), pltpu.SemaphoreType.DMA((2,))],
        ),
        out_shape=jax.ShapeDtypeStruct((n_pad, PACKED), I32),
        compiler_params=pltpu.CompilerParams(
            dimension_semantics=("arbitrary",), vmem_limit_bytes=VMEM_LIMIT),
        name="moe_experts",
    )(seg_start.astype(jnp.int32), counts.astype(jnp.int32), nfull.astype(jnp.int32),
      rem.astype(jnp.int32), ustart.astype(jnp.int32), used_rows, xd, w_gate, w_up, w_down)


def _pos_kernel(ps_ref, meta_ref, pos_ref):
    m = meta_ref[...]
    e = m[0:2, :]
    base = jnp.zeros_like(e)
    for k in range(MOE_EXPERTS):
        base = jnp.where(e == k, ps_ref[k], base)
    pos_ref[...] = base + m[2:4, :]


def _moe_positions(pad_start, meta, tm=2048):
    n_tok = meta.shape[1]
    return pl.pallas_call(
        _pos_kernel,
        grid_spec=pltpu.PrefetchScalarGridSpec(
            num_scalar_prefetch=1,
            grid=(n_tok // tm,),
            in_specs=[pl.BlockSpec((SUBLANES, tm), lambda i, ps: (0, i))],
            out_specs=pl.BlockSpec((2, tm), lambda i, ps: (0, i)),
        ),
        out_shape=jax.ShapeDtypeStruct((2, n_tok), jnp.int32),
        compiler_params=pltpu.CompilerParams(dimension_semantics=("parallel",)),
        name="moe_positions",
    )(pad_start, meta)


SC_CORES = 2
SC_SUBCORES = 16
SC_CHUNK = 32
SC_NBUF = 4


def _sc_gather_rows(table, idx):
    n_rows = idx.shape[0]
    width = table.shape[1]
    workers = SC_CORES * SC_SUBCORES
    per_worker = n_rows // workers
    n_chunks = per_worker // SC_CHUNK
    mesh = plsc.VectorSubcoreMesh(core_axis_name="c", subcore_axis_name="s")

    def body(table_hbm, idx_hbm, out_hbm, *scratch):
        idx_v = scratch[:SC_NBUF]
        rows_v = scratch[SC_NBUF:2 * SC_NBUF]
        gsem = scratch[2 * SC_NBUF:3 * SC_NBUF]
        ssem = scratch[3 * SC_NBUF:]
        wid = lax.axis_index("s") * SC_CORES + lax.axis_index("c")
        base = wid * per_worker

        def chunk(j):
            return pl.ds(pl.multiple_of(base + j * SC_CHUNK, SC_CHUNK), SC_CHUNK)

        def gather(b):
            return pltpu.make_async_copy(table_hbm.at[idx_v[b]], rows_v[b], gsem[b])

        def start_gather(j, b):
            pltpu.sync_copy(idx_hbm.at[chunk(j)], idx_v[b])
            gather(b).start()

        def store(j, b):
            return pltpu.make_async_copy(rows_v[b], out_hbm.at[chunk(j)], ssem[b])

        for b in range(SC_NBUF):
            start_gather(b, b)

        @pl.loop(0, n_chunks, step=SC_NBUF)
        def _(g):
            for b in range(SC_NBUF):
                gather(b).wait()
                store(g + b, b).start()
            for b in range(SC_NBUF):
                nxt = g + SC_NBUF + b

                @pl.when(nxt < n_chunks)
                def _():
                    store(g + b, b).wait()
                    start_gather(nxt, b)

        for b in range(SC_NBUF):
            store(0, b).wait()

    assert n_chunks % SC_NBUF == 0 and per_worker % SC_CHUNK == 0
    return pl.kernel(
        body,
        out_type=jax.ShapeDtypeStruct((n_rows, width), table.dtype),
        mesh=mesh,
        scratch_types=([pltpu.VMEM((SC_CHUNK,), jnp.int32)] * SC_NBUF
                       + [pltpu.VMEM((SC_CHUNK, width), table.dtype)] * SC_NBUF
                       + [pltpu.SemaphoreType.DMA] * (2 * SC_NBUF)),
        name="sc_gather_rows",
    )(table, idx)


def _sc_scatter_rows(rows, idx, n_out):
    n_rows, width = rows.shape
    workers = SC_CORES * SC_SUBCORES
    per_worker = n_rows // workers
    n_chunks = per_worker // SC_CHUNK
    mesh = plsc.VectorSubcoreMesh(core_axis_name="c", subcore_axis_name="s")

    def body(rows_hbm, idx_hbm, out_hbm, *scratch):
        idx_v = scratch[:2 * SC_NBUF]
        rows_v = scratch[2 * SC_NBUF:3 * SC_NBUF]
        lsem = scratch[3 * SC_NBUF:4 * SC_NBUF]
        ssem = scratch[4 * SC_NBUF:]
        wid = lax.axis_index("s") * SC_CORES + lax.axis_index("c")
        base = wid * per_worker

        def chunk(j, k=0):
            return pl.ds(pl.multiple_of(k * n_rows + base + j * SC_CHUNK, SC_CHUNK), SC_CHUNK)

        def load(j, b):
            return pltpu.make_async_copy(rows_hbm.at[chunk(j)], rows_v[b], lsem[b])

        def scatter(b, k):
            return pltpu.make_async_copy(rows_v[b], out_hbm.at[idx_v[2 * b + k]], ssem[b])

        for b in range(SC_NBUF):
            load(b, b).start()

        @pl.loop(0, n_chunks, step=SC_NBUF)
        def _(g):
            for b in range(SC_NBUF):
                for k in range(2):
                    pltpu.sync_copy(idx_hbm.at[chunk(g + b, k)], idx_v[2 * b + k])
                load(g + b, b).wait()
                for k in range(2):
                    scatter(b, k).start()
            for b in range(SC_NBUF):
                nxt = g + SC_NBUF + b

                @pl.when(nxt < n_chunks)
                def _():
                    for k in range(2):
                        scatter(b, k).wait()
                    load(nxt, b).start()

        for b in range(SC_NBUF):
            for k in range(2):
                scatter(b, k).wait()

    assert n_chunks % SC_NBUF == 0 and per_worker % SC_CHUNK == 0
    return pl.kernel(
        body,
        out_type=jax.ShapeDtypeStruct((n_out, width), rows.dtype),
        mesh=mesh,
        scratch_types=([pltpu.VMEM((SC_CHUNK,), jnp.int32)] * (2 * SC_NBUF)
                       + [pltpu.VMEM((SC_CHUNK, width), rows.dtype)] * SC_NBUF
                       + [pltpu.SemaphoreType.DMA] * (2 * SC_NBUF)),
        name="sc_scatter_rows",
    )(rows, idx)


def _combine_kernel(x_ref, wts_ref, g1_ref, g2_ref, fg_ref, o_ref, *, final):
    w = wts_ref[...]
    out = (x_ref[...] + w[:, 0:1] * _unpack_rows(g1_ref[...])
           + w[:, 1:2] * _unpack_rows(g2_ref[...]))
    if final:
        out = _rms(out, fg_ref[...])
    o_ref[...] = out


def _moe_layer(x, meta, cnt, xp, layer, w_gate, w_up, w_down):
    n_tok = x.shape[0] * x.shape[1]

    counts = cnt[0, :MOE_EXPERTS].astype(jnp.int32)
    padded = ((counts + MOE_BLK - 1) // MOE_BLK) * MOE_BLK
    pad_start = jnp.cumsum(padded) - padded
    n_pad = n_tok * 2 + MOE_EXPERTS * MOE_BLK + MOE_UNIT

    pos = _moe_positions(pad_start, meta).reshape(2 * n_tok)
    xd = _sc_scatter_rows(xp, pos, n_pad)
    y = _moe_experts(xd, pad_start, counts, layer, w_gate, w_up, w_down)

    return _sc_gather_rows(y, pos)


def _moe_combine(x, moe_out, final_g=None):
    bsz, seq, d = x.shape
    n_tok = bsz * seq
    xt = x.reshape(n_tok, d)
    wts, gathered = moe_out
    tm = COMBINE_TM
    n_tiles = n_tok // tm
    final = final_g is not None
    fg = (final_g if final else jnp.ones((d,), F32)).reshape(1, d)
    out = pl.pallas_call(
        functools.partial(_combine_kernel, final=final),
        grid=(n_tiles,),
        in_specs=[
            pl.BlockSpec((tm, d), lambda i: (i, 0)),
            pl.BlockSpec((tm, LANES), lambda i: (i, 0)),
            pl.BlockSpec((tm, PACKED), lambda i: (i, 0)),
            pl.BlockSpec((tm, PACKED), lambda i: (i + n_tiles, 0)),
            pl.BlockSpec((1, d), lambda i: (0, 0)),
        ],
        out_specs=pl.BlockSpec((tm, d), lambda i: (i, 0)),
        out_shape=jax.ShapeDtypeStruct((n_tok, d), F32),
        compiler_params=pltpu.CompilerParams(
            dimension_semantics=("parallel",), vmem_limit_bytes=VMEM_LIMIT),
        name="moe_combine",
    )(xt, wts, gathered, gathered, fg)
    return out.reshape(bsz, seq, d)


def kernel(x, mem, s5_norm, s5_w_in, s5_lambda_re, s5_lambda_im, s5_log_dt, s5_b_re, s5_b_im, s5_c_re, s5_c_im, s5_d, s5_w_out, gm_norm, gm_w_in, gm_v_norm, gm_w_s, gm_b_s, gm_w_out, mem_norm, xa_norm, xa_w_q, xa_w_kv, xa_w_o, moe_norm, moe_w_group, moe_b_group, moe_w_expert, moe_b_expert, moe_w_gate, moe_w_up, moe_w_down, final_norm):
    moe_out = None
    for i in range(DEPTH):
        j = i // 2
        if i % 2 == 0:
            if moe_out is not None:
                x = _moe_combine(x, moe_out)
            x = _s5_layer(x, s5_norm[j], s5_w_in[j], s5_lambda_re[j], s5_lambda_im[j],
                          s5_log_dt[j], s5_b_re[j], s5_b_im[j], s5_c_re[j], s5_c_im[j],
                          s5_d[j], s5_w_out[j])
        else:
            x = _gmlp_layer(x, moe_out, gm_norm[j], gm_w_in[j], gm_v_norm[j], gm_w_s[j],
                            gm_b_s[j], gm_w_out[j])
        x, meta, wts, cnt, xp = _xattn_router_layer(
            x, mem, mem_norm, xa_norm[i], xa_w_q[i], xa_w_kv[i], xa_w_o[i], moe_norm[i],
            moe_w_group[i], moe_b_group[i], moe_w_expert[i], moe_b_expert[i])
        moe_out = (wts, _moe_layer(x, meta, cnt, xp, i, moe_w_gate, moe_w_up, moe_w_down))
    return _moe_combine(x, moe_out, final_g=final_norm)
```

```python
import functools

import jax
import jax.numpy as jnp
from jax import lax
from jax.experimental import pallas as pl
from jax.experimental.pallas import tpu as pltpu
from jax.experimental.pallas import tpu_sc as plsc

F32 = jnp.float32
BF16 = jnp.bfloat16

D_MODEL = 1024
DEPTH = 2
CHUNK = 64
S5_GROUP_CH = 16
S5_GROUPS = 64
S5_STATE = 64
GM_HALF = 2 * D_MODEL
GM_GROUPS = 8
GM_SPAN = 128
GM_GROUP_CH = GM_HALF // GM_GROUPS
XA_HEADS = 4
XA_HEAD_DIM = D_MODEL // XA_HEADS
XA_SUB = 512
MOE_GROUPS = 4
MOE_PER_GROUP = 8
MOE_EXPERTS = MOE_GROUPS * MOE_PER_GROUP
MOE_HIDDEN = D_MODEL // 2
RMS_EPS = 1e-6

LANES = 128
SUBLANES = 8
VMEM_LIMIT = 56 * 1024 * 1024

S5_CT = 256
S5_NCT = D_MODEL // S5_CT
S5_STATES_CT = (S5_CT // S5_GROUP_CH) * S5_STATE
S5_SLABS = S5_STATES_CT // LANES
S5_TC = 128
S5_PITCH = S5_TC + 4
S5_SCAN_CT = 2

MOE_BLK = 256
MOE_UNIT = 1024
MOE_TAILS = (512, 256)
COMBINE_TM = 512


def _rms(x, g):
    ms = jnp.mean(x * x, axis=-1, keepdims=True)
    return x * lax.rsqrt(ms + RMS_EPS) * g


def _const_spec(shape):
    nd = len(shape)
    return pl.BlockSpec(shape, lambda *_: (0,) * nd, pipeline_mode=pl.Buffered(1))


def _s5_kernel(x_ref, g_ref, win_ref, bblk_ref, cblk_ref, are_ref, aim_ref, d_ref,
               wout_ref, o_ref, hre_ref, him_ref, bre_ref, bim_ref, sre_ref, sim_ref):
    nb = x_ref.shape[0]
    tc = x_ref.shape[1]

    @pl.when(pl.program_id(0) == 0)
    def _():
        hre_ref[...] = jnp.zeros_like(hre_ref)
        him_ref[...] = jnp.zeros_like(him_ref)

    x = x_ref[...].reshape(nb * tc, D_MODEL)
    xn = _rms(x, g_ref[...]).astype(BF16)
    u = jnp.dot(xn, win_ref[...], preferred_element_type=F32)

    y_parts = []
    for ct0 in range(0, S5_NCT, S5_SCAN_CT):
        tiles = range(ct0, ct0 + S5_SCAN_CT)
        for c, ct in enumerate(tiles):
            u_ct = u[:, ct * S5_CT:(ct + 1) * S5_CT]
            bu = jnp.dot(u_ct.astype(BF16), bblk_ref[ct], preferred_element_type=F32)
            for b in range(nb):
                for j in range(S5_SLABS):
                    rows = slice(b * tc, (b + 1) * tc)
                    bre_ref[c, b, pl.ds(j * S5_PITCH, tc), :] = bu[rows, j * LANES:(j + 1) * LANES]
                    bim_ref[c, b, pl.ds(j * S5_PITCH, tc), :] = bu[
                        rows, S5_STATES_CT + j * LANES:S5_STATES_CT + (j + 1) * LANES]
        chains = [(c, ct, b) for c, ct in enumerate(tiles) for b in range(nb)]
        a_re = [are_ref[ct] for ct in tiles]
        a_im = [aim_ref[ct] for ct in tiles]

        def step(t, carry):
            idx = pl.ds(t, S5_SLABS, stride=S5_PITCH)
            bu_t = [(bre_ref[c, b, idx, :], bim_ref[c, b, idx, :]) for c, _, b in chains]
            new = []
            for k, (c, _, b) in enumerate(chains):
                hr, hi = carry[2 * k], carry[2 * k + 1]
                new.append(a_re[c] * hr - a_im[c] * hi + bu_t[k][0])
                new.append(a_re[c] * hi + a_im[c] * hr + bu_t[k][1])
            for k, (c, _, b) in enumerate(chains):
                sre_ref[c, b, idx, :] = new[2 * k]
                sim_ref[c, b, idx, :] = new[2 * k + 1]
            return tuple(new)

        init = []
        for _, ct, b in chains:
            init += [hre_ref[ct, b], him_ref[ct, b]]
        fin = lax.fori_loop(0, tc, step, tuple(init), unroll=2)
        for k, (_, ct, b) in enumerate(chains):
            hre_ref[ct, b] = fin[2 * k]
            him_ref[ct, b] = fin[2 * k + 1]

        for c, ct in enumerate(tiles):
            xs = []
            for b in range(nb):
                cols = [sre_ref[c, b, pl.ds(j * S5_PITCH, tc), :] for j in range(S5_SLABS)]
                cols += [sim_ref[c, b, pl.ds(j * S5_PITCH, tc), :] for j in range(S5_SLABS)]
                xs.append(jnp.concatenate(cols, axis=1))
            xst = jnp.concatenate(xs, axis=0).astype(BF16)
            y_parts.append(jnp.dot(xst, cblk_ref[ct], preferred_element_type=F32))

    y = jnp.concatenate(y_parts, axis=1) + d_ref[...] * u
    yg = jax.nn.gelu(y).astype(BF16)
    z = jnp.dot(yg, wout_ref[...], preferred_element_type=F32)
    out = x + z[:, :D_MODEL] * (1.0 / (1.0 + jnp.exp(-z[:, D_MODEL:])))
    o_ref[...] = out.reshape(nb, tc, D_MODEL)


def _s5_discretize(lam_re, lam_im, log_dt, b_re, b_im, c_re, c_im):
    lr = lam_re.astype(F32)
    li = lam_im.astype(F32)
    dt = jnp.exp(log_dt.astype(F32))[:, None]
    mag = jnp.exp(lr * dt)
    ab_re = mag * jnp.cos(li * dt)
    ab_im = mag * jnp.sin(li * dt)
    den = lr * lr + li * li
    coef_re = ((ab_re - 1.0) * lr + ab_im * li) / den
    coef_im = (ab_im * lr - (ab_re - 1.0) * li) / den
    br = b_re.astype(F32)
    bi = b_im.astype(F32)
    bb_re = coef_re[..., None] * br - coef_im[..., None] * bi
    bb_im = coef_re[..., None] * bi + coef_im[..., None] * br
    gpt = S5_CT // S5_GROUP_CH
    ch_group = jnp.arange(S5_CT, dtype=jnp.int32) // S5_GROUP_CH
    st_group = jnp.arange(S5_STATES_CT, dtype=jnp.int32) // S5_STATE
    same = ch_group[:, None] == st_group[None, :]

    def in_blocks(bb):
        t = bb.reshape(S5_NCT, gpt, S5_STATE, S5_GROUP_CH).transpose(0, 3, 1, 2)
        t = t.reshape(S5_NCT, 1, S5_GROUP_CH, S5_STATES_CT)
        t = jnp.broadcast_to(t, (S5_NCT, gpt, S5_GROUP_CH, S5_STATES_CT))
        return jnp.where(same[None], t.reshape(S5_NCT, S5_CT, S5_STATES_CT), 0.0)

    def out_blocks(c):
        return in_blocks(c.transpose(0, 2, 1)).transpose(0, 2, 1)

    bblk = jnp.concatenate([in_blocks(bb_re), in_blocks(bb_im)], axis=2).astype(BF16)
    cblk = jnp.concatenate([out_blocks(c_re.astype(F32)),
                            out_blocks(-c_im.astype(F32))], axis=1).astype(BF16)
    a_re = ab_re.reshape(S5_NCT, S5_SLABS, LANES)
    a_im = ab_im.reshape(S5_NCT, S5_SLABS, LANES)
    return bblk, cblk, a_re, a_im


def _s5_layer(x, norm_g, w_in, lam_re, lam_im, log_dt, b_re, b_im, c_re, c_im, d_skip, w_out):
    bsz, seq, d = x.shape
    bblk, cblk, a_re, a_im = _s5_discretize(lam_re, lam_im, log_dt, b_re, b_im, c_re, c_im)
    xspec = pl.BlockSpec((bsz, S5_TC, d), lambda k: (0, k, 0))
    return pl.pallas_call(
        _s5_kernel,
        grid=(seq // S5_TC,),
        in_specs=[
            xspec,
            _const_spec((1, d)),
            _const_spec((d, d)),
            _const_spec(bblk.shape),
            _const_spec(cblk.shape),
            _const_spec(a_re.shape),
            _const_spec(a_im.shape),
            _const_spec((1, d)),
            _const_spec((d, 2 * d)),
        ],
        out_specs=xspec,
        out_shape=jax.ShapeDtypeStruct(x.shape, F32),
        scratch_shapes=[
            pltpu.VMEM((S5_NCT, bsz, S5_SLABS, LANES), F32),
            pltpu.VMEM((S5_NCT, bsz, S5_SLABS, LANES), F32),
            pltpu.VMEM((S5_SCAN_CT, bsz, S5_SLABS * S5_PITCH, LANES), F32),
            pltpu.VMEM((S5_SCAN_CT, bsz, S5_SLABS * S5_PITCH, LANES), F32),
            pltpu.VMEM((S5_SCAN_CT, bsz, S5_SLABS * S5_PITCH, LANES), F32),
            pltpu.VMEM((S5_SCAN_CT, bsz, S5_SLABS * S5_PITCH, LANES), F32),
        ],
        compiler_params=pltpu.CompilerParams(
            dimension_semantics=("arbitrary",), vmem_limit_bytes=VMEM_LIMIT),
        name="s5_layer",
    )(x, norm_g.reshape(1, d), w_in.astype(BF16), bblk, cblk, a_re, a_im,
      d_skip.reshape(1, d).astype(F32), w_out.astype(BF16))


def _gmlp_kernel(x_ref, wts_ref, g1_ref, g2_ref, g_ref, win_ref, vn_ref, ws_ref, bst_ref,
                 wout_ref, o_ref):
    tm = x_ref.shape[0]
    w = wts_ref[...]
    x = (x_ref[...] + w[:, 0:1] * _unpack_rows(g1_ref[...])
         + w[:, 1:2] * _unpack_rows(g2_ref[...]))
    xn = _rms(x, g_ref[...]).astype(BF16)
    u = jax.nn.gelu(jnp.dot(xn, win_ref[:, :GM_HALF], preferred_element_type=F32))
    v = jax.nn.gelu(jnp.dot(xn, win_ref[:, GM_HALF:], preferred_element_type=F32))
    vb = _rms(v, vn_ref[...]).astype(BF16)
    row = lax.broadcasted_iota(jnp.int32, (GM_SPAN, GM_SPAN), 0) // CHUNK
    col = lax.broadcasted_iota(jnp.int32, (GM_SPAN, GM_SPAN), 1) // CHUNK
    causal = row >= col
    ws = [jnp.where(causal, ws_ref[g], 0.0).astype(BF16) for g in range(GM_GROUPS)]
    spans = []
    for s in range(tm // GM_SPAN):
        parts = []
        for g in range(GM_GROUPS):
            vblk = vb[s * GM_SPAN:(s + 1) * GM_SPAN, g * GM_GROUP_CH:(g + 1) * GM_GROUP_CH]
            parts.append(jnp.dot(ws[g], vblk, preferred_element_type=F32)
                         + bst_ref[:, g:g + 1])
        spans.append(jnp.concatenate(parts, axis=1))
    mixed = jnp.concatenate(spans, axis=0)
    p = (u * mixed).astype(BF16)
    o_ref[...] = x + jnp.dot(p, wout_ref[...], preferred_element_type=F32)


def _gmlp_layer(x, moe_out, norm_g, w_in, v_norm, w_s, b_s, w_out, tm=512):
    bsz, seq, d = x.shape
    n_tok = bsz * seq
    xt = x.reshape(n_tok, d)
    wts, gathered = moe_out
    n_tiles = n_tok // tm
    xspec = pl.BlockSpec((tm, d), lambda i: (i, 0))
    out = pl.pallas_call(
        _gmlp_kernel,
        grid=(n_tiles,),
        in_specs=[
            xspec,
            pl.BlockSpec((tm, LANES), lambda i: (i, 0)),
            pl.BlockSpec((tm, d // 2), lambda i: (i, 0)),
            pl.BlockSpec((tm, d // 2), lambda i: (i + n_tiles, 0)),
            _const_spec((1, d)),
            _const_spec((d, 2 * GM_HALF)),
            _const_spec((1, GM_HALF)),
            _const_spec((GM_GROUPS, GM_SPAN, GM_SPAN)),
            _const_spec((GM_SPAN, GM_GROUPS)),
            _const_spec((GM_HALF, d)),
        ],
        out_specs=xspec,
        out_shape=jax.ShapeDtypeStruct((n_tok, d), F32),
        compiler_params=pltpu.CompilerParams(
            dimension_semantics=("parallel",), vmem_limit_bytes=VMEM_LIMIT),
        name="gmlp_layer",
    )(xt, wts, gathered, gathered, norm_g.reshape(1, d), w_in.astype(BF16),
      v_norm.reshape(1, GM_HALF),
      w_s.astype(F32), b_s.T.astype(F32), w_out.astype(BF16))
    return out.reshape(bsz, seq, d)


def _norm_proj_kernel(x_ref, g_ref, w_ref, o_ref):
    xn = _rms(x_ref[...], g_ref[...]).astype(BF16)
    o_ref[...] = jnp.dot(xn, w_ref[...], preferred_element_type=F32).astype(o_ref.dtype)


def _norm_proj(x, g, w, out_dtype, tm=512, tn=1024):
    m, d = x.shape
    n = w.shape[1]
    return pl.pallas_call(
        _norm_proj_kernel,
        grid=(m // tm, n // tn),
        in_specs=[
            pl.BlockSpec((tm, d), lambda i, j: (i, 0)),
            pl.BlockSpec((1, d), lambda i, j: (0, 0)),
            pl.BlockSpec((d, tn), lambda i, j: (0, j)),
        ],
        out_specs=pl.BlockSpec((tm, tn), lambda i, j: (i, j)),
        out_shape=jax.ShapeDtypeStruct((m, n), out_dtype),
        compiler_params=pltpu.CompilerParams(
            dimension_semantics=("parallel", "parallel"), vmem_limit_bytes=VMEM_LIMIT),
        name="norm_proj",
    )(x, g.reshape(1, d), w.astype(BF16))


def _route(x, g_ref, wr_ref, bias_ref, tri_ref, run_ref):
    tm = x.shape[0]
    xn = _rms(x, g_ref[...])
    xhi = xn.astype(BF16)
    xlo = (xn - xhi.astype(F32)).astype(BF16)
    hi = jnp.dot(xhi, wr_ref[...], preferred_element_type=F32)
    lo = jnp.dot(xlo, wr_ref[:, :LANES], preferred_element_type=F32)
    logits = hi[:, :LANES] + hi[:, LANES:] + lo + bias_ref[...]
    lane = lax.broadcasted_iota(jnp.int32, (tm, LANES), 1).astype(F32)
    neg = jnp.float32(-jnp.inf)

    def first_argmax(vals):
        mx = jnp.max(vals, axis=-1, keepdims=True)
        idx = jnp.min(jnp.where(vals == mx, lane, float(LANES)), axis=-1, keepdims=True)
        return mx, idx

    gl = jnp.where(lane < MOE_GROUPS, logits, neg)
    gmax, gidx = first_argmax(gl)
    w_g = 1.0 / jnp.sum(jnp.exp(gl - gmax), axis=-1, keepdims=True)
    first = MOE_GROUPS + MOE_PER_GROUP * gidx
    el = jnp.where((lane >= first) & (lane < first + MOE_PER_GROUP), logits, neg)
    m1, i1 = first_argmax(el)
    m2, i2 = first_argmax(jnp.where(lane == i1, neg, el))
    e21 = jnp.exp(m2 - m1)
    w1 = w_g / (1.0 + e21)
    w2 = w_g * e21 / (1.0 + e21)
    e1 = i1 - MOE_GROUPS
    e2 = i2 - MOE_GROUPS

    onehot = ((lane == e1) | (lane == e2)).astype(BF16)
    tot = run_ref[...] + jnp.dot(tri_ref[...], onehot, preferred_element_type=F32)
    r1 = jnp.sum(jnp.where(lane == e1, tot, 0.0), axis=-1, keepdims=True)
    r2 = jnp.sum(jnp.where(lane == e2, tot, 0.0), axis=-1, keepdims=True)
    run_ref[...] = run_ref[...] + jnp.sum(onehot.astype(F32), axis=0, keepdims=True)

    meta = jnp.where(lane == 0, e1, jnp.where(lane == 1, e2,
                     jnp.where(lane == 2, r1, jnp.where(lane == 3, r2, 0.0))))
    meta_t = jnp.transpose(meta)[:SUBLANES, :].astype(jnp.int32)
    wts = jnp.where(lane == 0, w1, jnp.where(lane == 1, w2, 0.0))
    return meta_t, wts, xn


def _xattn_router_kernel(x_ref, g_ref, wq_ref, kt_ref, v_ref, wo_ref, mg_ref, wr_ref,
                         bias_ref, tri_ref, o_ref, meta_ref, wts_ref, cnt_ref, xp_ref, run_ref):
    @pl.when((pl.program_id(0) == 0) & (pl.program_id(1) == 0))
    def _():
        run_ref[...] = jnp.zeros_like(run_ref)

    tm = x_ref.shape[1]
    outs = []
    for r0 in range(0, tm, XA_SUB):
        x = x_ref[0, r0:r0 + XA_SUB, :]
        xn = _rms(x, g_ref[...]).astype(BF16)
        q = jnp.dot(xn, wq_ref[...], preferred_element_type=F32) * (XA_HEAD_DIM ** -0.5)
        q = q.astype(BF16)
        heads = []
        for h in range(XA_HEADS):
            cols = slice(h * XA_HEAD_DIM, (h + 1) * XA_HEAD_DIM)
            s = jnp.dot(q[:, cols], kt_ref[0, cols, :], preferred_element_type=F32)
            e = jnp.exp(s - jnp.max(s, axis=-1, keepdims=True))
            p = (e / jnp.sum(e, axis=-1, keepdims=True)).astype(BF16)
            heads.append(jnp.dot(p, v_ref[0, :, cols], preferred_element_type=F32))
        o = jnp.concatenate(heads, axis=1).astype(BF16)
        outs.append(x + jnp.dot(o, wo_ref[...], preferred_element_type=F32))
    out = jnp.concatenate(outs, axis=0)
    o_ref[0] = out
    meta, wts, xn_moe = _route(out, mg_ref, wr_ref, bias_ref, tri_ref, run_ref)
    meta_ref[...] = meta
    wts_ref[...] = wts
    cnt_ref[...] = run_ref[...]
    xp_ref[...] = _pack_rows(xn_moe)


XA_TM = 1024
N_PARTS = 2


def _xattn_router_params(mem, mem_g, norm_g, w_q, w_kv, w_o, moe_g, w_group, b_group,
                         w_expert, b_expert):
    bsz, m, d = mem.shape
    tm = XA_TM
    kv = _norm_proj(mem.reshape(bsz * m, d), mem_g, w_kv, BF16).reshape(bsz, m, 2 * d)
    kt = kv[..., :d].transpose(0, 2, 1)
    v = kv[..., d:]
    nr = MOE_GROUPS + MOE_EXPERTS
    w_r = jnp.zeros((d, LANES), F32).at[:, :MOE_GROUPS].set(w_group.astype(F32))
    w_r = w_r.at[:, MOE_GROUPS:nr].set(w_expert.astype(F32))
    bias = jnp.zeros((1, LANES), F32).at[0, :MOE_GROUPS].set(b_group.astype(F32))
    bias = bias.at[0, MOE_GROUPS:nr].set(b_expert.astype(F32))
    w_hi = w_r.astype(BF16)
    w_lo = (w_r - w_hi.astype(F32)).astype(BF16)
    w_hilo = jnp.concatenate([w_hi, w_lo], axis=1)
    earlier = jnp.tril(jnp.ones((tm, tm), BF16), -1)
    return (norm_g.reshape(1, d), w_q.astype(BF16), kt, v, w_o.astype(BF16),
            moe_g.reshape(1, d), w_hilo, bias, earlier)


def _xattn_router_part(x, b0, mem_b0, nb, params):
    _, seq, d = x.shape
    tm = XA_TM
    nt = seq // tm
    m = params[3].shape[1]
    return pl.pallas_call(
        _xattn_router_kernel,
        grid=(nb, nt),
        in_specs=[
            pl.BlockSpec((1, tm, d), lambda b, i: (b + b0, i, 0)),
            _const_spec((1, d)),
            _const_spec((d, d)),
            pl.BlockSpec((1, d, m), lambda b, i: (b + mem_b0, 0, 0)),
            pl.BlockSpec((1, m, d), lambda b, i: (b + mem_b0, 0, 0)),
            _const_spec((d, d)),
            _const_spec((1, d)),
            _const_spec((d, 2 * LANES)),
            _const_spec((1, LANES)),
            _const_spec((tm, tm)),
        ],
        out_specs=[
            pl.BlockSpec((1, tm, d), lambda b, i: (b, i, 0)),
            pl.BlockSpec((SUBLANES, tm), lambda b, i: (0, b * nt + i)),
            pl.BlockSpec((tm, LANES), lambda b, i: (b * nt + i, 0)),
            pl.BlockSpec((1, LANES), lambda b, i: (0, 0)),
            pl.BlockSpec((tm, d // 2), lambda b, i: (b * nt + i, 0)),
        ],
        out_shape=[
            jax.ShapeDtypeStruct((nb, seq, d), F32),
            jax.ShapeDtypeStruct((SUBLANES, nb * seq), jnp.int32),
            jax.ShapeDtypeStruct((nb * seq, LANES), F32),
            jax.ShapeDtypeStruct((1, LANES), F32),
            jax.ShapeDtypeStruct((nb * seq, d // 2), jnp.int32),
        ],
        scratch_shapes=[pltpu.VMEM((1, LANES), F32)],
        compiler_params=pltpu.CompilerParams(
            dimension_semantics=("arbitrary", "arbitrary"), vmem_limit_bytes=VMEM_LIMIT),
        name="xattn_router",
    )(x, *params)


PACKED = D_MODEL // 2
I32 = jnp.int32
HI_HALF = -65536
LO_HALF = 65535


def _pack_rows(v):
    lo = lax.bitcast_convert_type(v[:, :PACKED].astype(BF16).astype(F32), I32)
    hi = lax.bitcast_convert_type(v[:, PACKED:].astype(BF16).astype(F32), I32)
    return (hi & HI_HALF) | ((lo >> 16) & LO_HALF)


def _unpack_rows(p):
    lo = lax.bitcast_convert_type(p << 16, F32)
    hi = lax.bitcast_convert_type(p & HI_HALF, F32)
    return jnp.concatenate([lo, hi], axis=1)


def _expert_kernel(row0_ref, cnt_ref, nfull_ref, rem_ref, ustart_ref, used_ref, xd_hbm, wg_ref,
                   wu_ref, wd_ref, y_hbm, wgb, wub, wdb, xbuf, ybuf, cls_ref, lsem, ssem):
    e = pl.program_id(0)
    used_rows = used_ref[0]
    row0 = row0_ref[e]
    cnt = cnt_ref[e]
    nfull = nfull_ref[e]
    rem = rem_ref[e]
    u0 = ustart_ref[e]
    sizes = (MOE_UNIT,) + MOE_TAILS

    def load(r, slot):
        src = xd_hbm.at[pl.ds(pl.multiple_of(r, MOE_BLK), MOE_UNIT), :]
        return pltpu.make_async_copy(src, xbuf.at[slot], lsem.at[slot])

    def store(r, slot, size):
        dst = y_hbm.at[pl.ds(pl.multiple_of(r, MOE_BLK), size), :]
        return pltpu.make_async_copy(ybuf.at[slot, pl.ds(0, size), :], dst, ssem.at[slot])

    def wait_store(slot):
        for c, size in enumerate(sizes):
            @pl.when(cls_ref[slot] == c + 1)
            def _():
                store(0, slot, size).wait()

    @pl.when(e == 0)
    def _():
        cls_ref[0] = 0
        cls_ref[1] = 0

        @pl.when(used_rows > 0)
        def _():
            load(0, 0).start()

    @pl.when(nfull + rem > 0)
    def _():
        wgb[...] = wg_ref[...].astype(BF16)
        wub[...] = wu_ref[...].astype(BF16)
        wdb[...] = wd_ref[...].astype(BF16)

    def unit(u, r, c):
        size = sizes[c]
        slot = u % 2
        load(r, slot).wait()

        @pl.when(r + size < used_rows)
        def _():
            load(r + size, 1 - slot).start()

        wait_store(slot)
        row = lax.broadcasted_iota(jnp.int32, (size, PACKED), 0)
        xd = jnp.where(row < cnt - (r - row0), xbuf[slot, pl.ds(0, size), :], 0)
        xb = _unpack_rows(xd).astype(BF16)
        a = jnp.dot(xb, wgb[...], preferred_element_type=F32)
        up = jnp.dot(xb, wub[...], preferred_element_type=F32)
        h = (a * (1.0 / (1.0 + jnp.exp(-a))) * up).astype(BF16)
        ybuf[slot, pl.ds(0, size), :] = _pack_rows(
            jnp.dot(h, wdb[...], preferred_element_type=F32))
        store(r, slot, size).start()
        cls_ref[slot] = c + 1

    def full_unit(k, carry):
        unit(u0 + k, row0 + k * MOE_UNIT, 0)
        return carry

    lax.fori_loop(0, nfull, full_unit, 0)

    u = u0 + nfull
    r = row0 + nfull * MOE_UNIT
    for c, size in enumerate(MOE_TAILS, start=1):
        has = (rem & (size // MOE_BLK)) != 0

        @pl.when(has)
        def _():
            unit(u, r, c)
        u = u + has.astype(jnp.int32)
        r = r + has.astype(jnp.int32) * size

    @pl.when(e == pl.num_programs(0) - 1)
    def _():
        wait_store(0)
        wait_store(1)


def _moe_experts(xd, seg_start, counts, layer, w_gate, w_up, w_down):
    n_pad = xd.shape[0]
    d = D_MODEL
    nblk = (counts + MOE_BLK - 1) // MOE_BLK
    per_unit = MOE_UNIT // MOE_BLK
    nfull = nblk // per_unit
    rem = nblk % per_unit
    n_units = nfull + sum(((rem & (size // MOE_BLK)) != 0).astype(jnp.int32)
                          for size in MOE_TAILS)
    ustart = jnp.cumsum(n_units) - n_units
    used_rows = (seg_start[-1:] + nblk[-1:] * MOE_BLK).astype(jnp.int32)

    def expert_blk(e, *_):
        return (layer, e, 0, 0)

    return pl.pallas_call(
        _expert_kernel,
        grid_spec=pltpu.PrefetchScalarGridSpec(
            num_scalar_prefetch=6,
            grid=(MOE_EXPERTS,),
            in_specs=[
                pl.BlockSpec(memory_space=pl.ANY),
                pl.BlockSpec((None, None, d, MOE_HIDDEN), expert_blk),
                pl.BlockSpec((None, None, d, MOE_HIDDEN), expert_blk),
                pl.BlockSpec((None, None, MOE_HIDDEN, d), expert_blk),
            ],
            out_specs=pl.BlockSpec(memory_space=pl.ANY),
            scratch_shapes=[pltpu.VMEM((d, MOE_HIDDEN), BF16), pltpu.VMEM((d, MOE_HIDDEN), BF16),
                            pltpu.VMEM((MOE_HIDDEN, d), BF16),
                            pltpu.VMEM((2, MOE_UNIT, PACKED), I32),
                            pltpu.VMEM((2, MOE_UNIT, PACKED), I32),
                            pltpu.SMEM((2,), jnp.int32),
                            pltpu.SemaphoreType.DMA((2,)), pltpu.SemaphoreType.DMA((2,))],
        ),
        out_shape=jax.ShapeDtypeStruct((n_pad, PACKED), I32),
        compiler_params=pltpu.CompilerParams(
            dimension_semantics=("arbitrary",), vmem_limit_bytes=VMEM_LIMIT),
        name="moe_experts",
    )(seg_start.astype(jnp.int32), counts.astype(jnp.int32), nfull.astype(jnp.int32),
      rem.astype(jnp.int32), ustart.astype(jnp.int32), used_rows, xd, w_gate, w_up, w_down)


def _pos_kernel(ps_ref, meta_ref, pos_ref):
    m = meta_ref[...]
    e = m[0:2, :]
    base = jnp.zeros_like(e)
    for k in range(MOE_EXPERTS):
        base = jnp.where(e == k, ps_ref[k], base)
    pos_ref[...] = base + m[2:4, :]


def _moe_positions(pad_start, meta, tm=2048):
    n_tok = meta.shape[1]
    return pl.pallas_call(
        _pos_kernel,
        grid_spec=pltpu.PrefetchScalarGridSpec(
            num_scalar_prefetch=1,
            grid=(n_tok // tm,),
            in_specs=[pl.BlockSpec((SUBLANES, tm), lambda i, ps: (0, i))],
            out_specs=pl.BlockSpec((2, tm), lambda i, ps: (0, i)),
        ),
        out_shape=jax.ShapeDtypeStruct((2, n_tok), jnp.int32),
        compiler_params=pltpu.CompilerParams(dimension_semantics=("parallel",)),
        name="moe_positions",
    )(pad_start, meta)


SC_CORES = 2
SC_SUBCORES = 16
SC_CHUNK = 32
SC_NBUF = 4


def _sc_gather_rows(table, idx, after):
    n_rows = idx.shape[0]
    width = table.shape[1]
    workers = SC_CORES * SC_SUBCORES
    per_worker = n_rows // workers
    n_chunks = per_worker // SC_CHUNK
    mesh = plsc.VectorSubcoreMesh(core_axis_name="c", subcore_axis_name="s")

    def body(table_hbm, idx_hbm, after_hbm, out_hbm, *scratch):
        idx_v = scratch[:SC_NBUF]
        rows_v = scratch[SC_NBUF:2 * SC_NBUF]
        gsem = scratch[2 * SC_NBUF:3 * SC_NBUF]
        ssem = scratch[3 * SC_NBUF:]
        wid = lax.axis_index("s") * SC_CORES + lax.axis_index("c")
        base = wid * per_worker

        def chunk(j):
            return pl.ds(pl.multiple_of(base + j * SC_CHUNK, SC_CHUNK), SC_CHUNK)

        def gather(b):
            return pltpu.make_async_copy(table_hbm.at[idx_v[b]], rows_v[b], gsem[b])

        def start_gather(j, b):
            pltpu.sync_copy(idx_hbm.at[chunk(j)], idx_v[b])
            gather(b).start()

        def store(j, b):
            return pltpu.make_async_copy(rows_v[b], out_hbm.at[chunk(j)], ssem[b])

        for b in range(SC_NBUF):
            start_gather(b, b)

        @pl.loop(0, n_chunks, step=SC_NBUF)
        def _(g):
            for b in range(SC_NBUF):
                gather(b).wait()
                store(g + b, b).start()
            for b in range(SC_NBUF):
                nxt = g + SC_NBUF + b

                @pl.when(nxt < n_chunks)
                def _():
                    store(g + b, b).wait()
                    start_gather(nxt, b)

        for b in range(SC_NBUF):
            store(0, b).wait()

    assert n_chunks % SC_NBUF == 0 and per_worker % SC_CHUNK == 0
    return pl.kernel(
        body,
        out_type=jax.ShapeDtypeStruct((n_rows, width), table.dtype),
        mesh=mesh,
        scratch_types=([pltpu.VMEM((SC_CHUNK,), jnp.int32)] * SC_NBUF
                       + [pltpu.VMEM((SC_CHUNK, width), table.dtype)] * SC_NBUF
                       + [pltpu.SemaphoreType.DMA] * (2 * SC_NBUF)),
        name="sc_gather_rows",
    )(table, idx, after)


def _sc_scatter_rows(rows, idx, n_out, after):
    n_rows, width = rows.shape
    workers = SC_CORES * SC_SUBCORES
    per_worker = n_rows // workers
    n_chunks = per_worker // SC_CHUNK
    mesh = plsc.VectorSubcoreMesh(core_axis_name="c", subcore_axis_name="s")

    def body(rows_hbm, idx_hbm, after_hbm, out_hbm, *scratch):
        idx_v = scratch[:2 * SC_NBUF]
        rows_v = scratch[2 * SC_NBUF:3 * SC_NBUF]
        lsem = scratch[3 * SC_NBUF:4 * SC_NBUF]
        ssem = scratch[4 * SC_NBUF:]
        wid = lax.axis_index("s") * SC_CORES + lax.axis_index("c")
        base = wid * per_worker

        def chunk(j, k=0):
            return pl.ds(pl.multiple_of(k * n_rows + base + j * SC_CHUNK, SC_CHUNK), SC_CHUNK)

        def load(j, b):
            return pltpu.make_async_copy(rows_hbm.at[chunk(j)], rows_v[b], lsem[b])

        def scatter(b, k):
            return pltpu.make_async_copy(rows_v[b], out_hbm.at[idx_v[2 * b + k]], ssem[b])

        for b in range(SC_NBUF):
            load(b, b).start()

        @pl.loop(0, n_chunks, step=SC_NBUF)
        def _(g):
            for b in range(SC_NBUF):
                for k in range(2):
                    pltpu.sync_copy(idx_hbm.at[chunk(g + b, k)], idx_v[2 * b + k])
                load(g + b, b).wait()
                for k in range(2):
                    scatter(b, k).start()
            for b in range(SC_NBUF):
                nxt = g + SC_NBUF + b

                @pl.when(nxt < n_chunks)
                def _():
                    for k in range(2):
                        scatter(b, k).wait()
                    load(nxt, b).start()

        for b in range(SC_NBUF):
            for k in range(2):
                scatter(b, k).wait()

    assert n_chunks % SC_NBUF == 0 and per_worker % SC_CHUNK == 0
    return pl.kernel(
        body,
        out_type=jax.ShapeDtypeStruct((n_out, width), rows.dtype),
        mesh=mesh,
        scratch_types=([pltpu.VMEM((SC_CHUNK,), jnp.int32)] * (2 * SC_NBUF)
                       + [pltpu.VMEM((SC_CHUNK, width), rows.dtype)] * SC_NBUF
                       + [pltpu.SemaphoreType.DMA] * (2 * SC_NBUF)),
        name="sc_scatter_rows",
    )(rows, idx, after)


def _combine_kernel(x_ref, wts_ref, g1_ref, g2_ref, fg_ref, *rest, final):
    o_ref = rest[-1]
    w = wts_ref[...]
    out = (x_ref[...] + w[:, 0:1] * _unpack_rows(g1_ref[...])
           + w[:, 1:2] * _unpack_rows(g2_ref[...]))
    if final:
        out = _rms(out, fg_ref[...])
    o_ref[...] = out


def _moe_dispatch(meta, cnt, xp, after):
    n_tok = xp.shape[0]
    counts = cnt[0, :MOE_EXPERTS].astype(jnp.int32)
    padded = ((counts + MOE_BLK - 1) // MOE_BLK) * MOE_BLK
    pad_start = jnp.cumsum(padded) - padded
    n_pad = n_tok * 2 + MOE_EXPERTS * MOE_BLK + MOE_UNIT
    pos = _moe_positions(pad_start, meta).reshape(2 * n_tok)
    xd = _sc_scatter_rows(xp, pos, n_pad, pos if after is None else after)
    return xd, (pad_start, counts, pos)


def _moe_experts_gather(xd, tables, layer, w_gate, w_up, w_down, after):
    pad_start, counts, pos = tables
    y = _moe_experts(xd, pad_start, counts, layer, w_gate, w_up, w_down)
    return _sc_gather_rows(y, pos, after)


def _moe_combine(parts, moe_outs, final_g=None):
    nb, seq, d = parts[0].shape
    n_part = nb * seq
    tm = COMBINE_TM
    n_tiles = n_part // tm
    final = final_g is not None
    fg = (final_g if final else jnp.ones((d,), F32)).reshape(1, d)
    out = None
    for h, (x, (wts, gathered)) in enumerate(zip(parts, moe_outs)):
        in_specs = [
            pl.BlockSpec((tm, d), lambda i: (i, 0)),
            pl.BlockSpec((tm, LANES), lambda i: (i, 0)),
            pl.BlockSpec((tm, PACKED), lambda i: (i, 0)),
            pl.BlockSpec((tm, PACKED), lambda i: (i + n_tiles, 0)),
            pl.BlockSpec((1, d), lambda i: (0, 0)),
        ]
        args = [x.reshape(n_part, d), wts, gathered, gathered, fg]
        if out is not None:
            in_specs.append(pl.BlockSpec(memory_space=pl.ANY))
            args.append(out)
        out = pl.pallas_call(
            functools.partial(_combine_kernel, final=final),
            grid=(n_tiles,),
            in_specs=in_specs,
            out_specs=pl.BlockSpec((tm, d), lambda i, h=h: (i + h * n_tiles, 0)),
            out_shape=jax.ShapeDtypeStruct((len(parts) * n_part, d), F32),
            input_output_aliases={} if len(args) == 5 else {5: 0},
            compiler_params=pltpu.CompilerParams(
                dimension_semantics=("parallel",), vmem_limit_bytes=VMEM_LIMIT),
            name="moe_combine",
        )(*args)
    return out.reshape(len(parts) * nb, seq, d)


def kernel(x, mem, s5_norm, s5_w_in, s5_lambda_re, s5_lambda_im, s5_log_dt, s5_b_re, s5_b_im, s5_c_re, s5_c_im, s5_d, s5_w_out, gm_norm, gm_w_in, gm_v_norm, gm_w_s, gm_b_s, gm_w_out, mem_norm, xa_norm, xa_w_q, xa_w_kv, xa_w_o, moe_norm, moe_w_group, moe_b_group, moe_w_expert, moe_b_expert, moe_w_gate, moe_w_up, moe_w_down, final_norm):
    nb = x.shape[0] // N_PARTS
    parts = None
    moe_outs = None
    sc_last = None
    for i in range(DEPTH):
        j = i // 2
        if i % 2 == 0:
            if parts is not None:
                x = _moe_combine(parts, moe_outs)
            x = _s5_layer(x, s5_norm[j], s5_w_in[j], s5_lambda_re[j], s5_lambda_im[j],
                          s5_log_dt[j], s5_b_re[j], s5_b_im[j], s5_c_re[j], s5_c_im[j],
                          s5_d[j], s5_w_out[j])
            views = [(x, h * nb) for h in range(N_PARTS)]
        else:
            parts = [_gmlp_layer(p, mo, gm_norm[j], gm_w_in[j], gm_v_norm[j], gm_w_s[j],
                                 gm_b_s[j], gm_w_out[j]) for p, mo in zip(parts, moe_outs)]
            views = [(p, 0) for p in parts]
        params = _xattn_router_params(
            mem, mem_norm, xa_norm[i], xa_w_q[i], xa_w_kv[i], xa_w_o[i], moe_norm[i],
            moe_w_group[i], moe_b_group[i], moe_w_expert[i], moe_b_expert[i])
        parts, dispatched = [], []
        for h, (xv, b0) in enumerate(views):
            xo, meta, wts, cnt, xp = _xattn_router_part(xv, b0, h * nb, nb, params)
            xd, tables = _moe_dispatch(meta, cnt, xp, sc_last)
            sc_last = xd
            parts.append(xo)
            dispatched.append((wts, xd, tables))
        moe_outs = []
        for wts, xd, tables in dispatched:
            sc_last = _moe_experts_gather(xd, tables, i, moe_w_gate, moe_w_up, moe_w_down,
                                          sc_last)
            moe_outs.append((wts, sc_last))
    return _moe_combine(parts, moe_outs, final_g=final_norm)
```

```python
import functools

import jax
import jax.numpy as jnp
from jax import lax
from jax.experimental import pallas as pl
from jax.experimental.pallas import tpu as pltpu
from jax.experimental.pallas import tpu_sc as plsc

F32 = jnp.float32
BF16 = jnp.bfloat16

D_MODEL = 1024
DEPTH = 2
CHUNK = 64
S5_GROUP_CH = 16
S5_GROUPS = 64
S5_STATE = 64
GM_HALF = 2 * D_MODEL
GM_GROUPS = 8
GM_SPAN = 128
GM_GROUP_CH = GM_HALF // GM_GROUPS
XA_HEADS = 4
XA_HEAD_DIM = D_MODEL // XA_HEADS
XA_SUB = 512
MOE_GROUPS = 4
MOE_PER_GROUP = 8
MOE_EXPERTS = MOE_GROUPS * MOE_PER_GROUP
MOE_HIDDEN = D_MODEL // 2
RMS_EPS = 1e-6

LANES = 128
SUBLANES = 8
VMEM_LIMIT = 56 * 1024 * 1024

S5_CT = 256
S5_NCT = D_MODEL // S5_CT
S5_STATES_CT = (S5_CT // S5_GROUP_CH) * S5_STATE
S5_SLABS = S5_STATES_CT // LANES
S5_TC = 128
S5_PITCH = S5_TC + 4
S5_SCAN_CT = 2

MOE_BLK = 256
MOE_UNIT = 1024
MOE_TAILS = (512, 256)
COMBINE_TM = 512


def _rms(x, g):
    ms = jnp.mean(x * x, axis=-1, keepdims=True)
    return x * lax.rsqrt(ms + RMS_EPS) * g


def _const_spec(shape):
    nd = len(shape)
    return pl.BlockSpec(shape, lambda *_: (0,) * nd, pipeline_mode=pl.Buffered(1))


def _s5_kernel(x_ref, g_ref, win_ref, bblk_ref, cblk_ref, are_ref, aim_ref, d_ref,
               wout_ref, o_ref, hre_ref, him_ref, bre_ref, bim_ref, sre_ref, sim_ref):
    nb = x_ref.shape[0]
    tc = x_ref.shape[1]

    @pl.when(pl.program_id(0) == 0)
    def _():
        hre_ref[...] = jnp.zeros_like(hre_ref)
        him_ref[...] = jnp.zeros_like(him_ref)

    x = x_ref[...].reshape(nb * tc, D_MODEL)
    xn = _rms(x, g_ref[...]).astype(BF16)
    u = jnp.dot(xn, win_ref[...], preferred_element_type=F32)

    y_parts = []
    for ct0 in range(0, S5_NCT, S5_SCAN_CT):
        tiles = range(ct0, ct0 + S5_SCAN_CT)
        for c, ct in enumerate(tiles):
            u_ct = u[:, ct * S5_CT:(ct + 1) * S5_CT]
            bu = jnp.dot(u_ct.astype(BF16), bblk_ref[ct], preferred_element_type=F32)
            for b in range(nb):
                for j in range(S5_SLABS):
                    rows = slice(b * tc, (b + 1) * tc)
                    bre_ref[c, b, pl.ds(j * S5_PITCH, tc), :] = bu[rows, j * LANES:(j + 1) * LANES]
                    bim_ref[c, b, pl.ds(j * S5_PITCH, tc), :] = bu[
                        rows, S5_STATES_CT + j * LANES:S5_STATES_CT + (j + 1) * LANES]
        chains = [(c, ct, b) for c, ct in enumerate(tiles) for b in range(nb)]
        a_re = [are_ref[ct] for ct in tiles]
        a_im = [aim_ref[ct] for ct in tiles]

        def step(t, carry):
            idx = pl.ds(t, S5_SLABS, stride=S5_PITCH)
            bu_t = [(bre_ref[c, b, idx, :], bim_ref[c, b, idx, :]) for c, _, b in chains]
            new = []
            for k, (c, _, b) in enumerate(chains):
                hr, hi = carry[2 * k], carry[2 * k + 1]
                new.append(a_re[c] * hr - a_im[c] * hi + bu_t[k][0])
                new.append(a_re[c] * hi + a_im[c] * hr + bu_t[k][1])
            for k, (c, _, b) in enumerate(chains):
                sre_ref[c, b, idx, :] = new[2 * k]
                sim_ref[c, b, idx, :] = new[2 * k + 1]
            return tuple(new)

        init = []
        for _, ct, b in chains:
            init += [hre_ref[ct, b], him_ref[ct, b]]
        fin = lax.fori_loop(0, tc, step, tuple(init), unroll=2)
        for k, (_, ct, b) in enumerate(chains):
            hre_ref[ct, b] = fin[2 * k]
            him_ref[ct, b] = fin[2 * k + 1]

        for c, ct in enumerate(tiles):
            xs = []
            for b in range(nb):
                cols = [sre_ref[c, b, pl.ds(j * S5_PITCH, tc), :] for j in range(S5_SLABS)]
                cols += [sim_ref[c, b, pl.ds(j * S5_PITCH, tc), :] for j in range(S5_SLABS)]
                xs.append(jnp.concatenate(cols, axis=1))
            xst = jnp.concatenate(xs, axis=0).astype(BF16)
            y_parts.append(jnp.dot(xst, cblk_ref[ct], preferred_element_type=F32))

    y = jnp.concatenate(y_parts, axis=1) + d_ref[...] * u
    yg = jax.nn.gelu(y).astype(BF16)
    z = jnp.dot(yg, wout_ref[...], preferred_element_type=F32)
    out = x + z[:, :D_MODEL] * (1.0 / (1.0 + jnp.exp(-z[:, D_MODEL:])))
    o_ref[...] = out.reshape(nb, tc, D_MODEL)


def _s5_discretize(lam_re, lam_im, log_dt, b_re, b_im, c_re, c_im):
    lr = lam_re.astype(F32)
    li = lam_im.astype(F32)
    dt = jnp.exp(log_dt.astype(F32))[:, None]
    mag = jnp.exp(lr * dt)
    ab_re = mag * jnp.cos(li * dt)
    ab_im = mag * jnp.sin(li * dt)
    den = lr * lr + li * li
    coef_re = ((ab_re - 1.0) * lr + ab_im * li) / den
    coef_im = (ab_im * lr - (ab_re - 1.0) * li) / den
    br = b_re.astype(F32)
    bi = b_im.astype(F32)
    bb_re = coef_re[..., None] * br - coef_im[..., None] * bi
    bb_im = coef_re[..., None] * bi + coef_im[..., None] * br
    gpt = S5_CT // S5_GROUP_CH
    ch_group = jnp.arange(S5_CT, dtype=jnp.int32) // S5_GROUP_CH
    st_group = jnp.arange(S5_STATES_CT, dtype=jnp.int32) // S5_STATE
    same = ch_group[:, None] == st_group[None, :]

    def in_blocks(bb):
        t = bb.reshape(S5_NCT, gpt, S5_STATE, S5_GROUP_CH).transpose(0, 3, 1, 2)
        t = t.reshape(S5_NCT, 1, S5_GROUP_CH, S5_STATES_CT)
        t = jnp.broadcast_to(t, (S5_NCT, gpt, S5_GROUP_CH, S5_STATES_CT))
        return jnp.where(same[None], t.reshape(S5_NCT, S5_CT, S5_STATES_CT), 0.0)

    def out_blocks(c):
        return in_blocks(c.transpose(0, 2, 1)).transpose(0, 2, 1)

    bblk = jnp.concatenate([in_blocks(bb_re), in_blocks(bb_im)], axis=2).astype(BF16)
    cblk = jnp.concatenate([out_blocks(c_re.astype(F32)),
                            out_blocks(-c_im.astype(F32))], axis=1).astype(BF16)
    a_re = ab_re.reshape(S5_NCT, S5_SLABS, LANES)
    a_im = ab_im.reshape(S5_NCT, S5_SLABS, LANES)
    return bblk, cblk, a_re, a_im


def _s5_layer(x, norm_g, w_in, lam_re, lam_im, log_dt, b_re, b_im, c_re, c_im, d_skip, w_out):
    bsz, seq, d = x.shape
    bblk, cblk, a_re, a_im = _s5_discretize(lam_re, lam_im, log_dt, b_re, b_im, c_re, c_im)
    xspec = pl.BlockSpec((bsz, S5_TC, d), lambda k: (0, k, 0))
    return pl.pallas_call(
        _s5_kernel,
        grid=(seq // S5_TC,),
        in_specs=[
            xspec,
            _const_spec((1, d)),
            _const_spec((d, d)),
            _const_spec(bblk.shape),
            _const_spec(cblk.shape),
            _const_spec(a_re.shape),
            _const_spec(a_im.shape),
            _const_spec((1, d)),
            _const_spec((d, 2 * d)),
        ],
        out_specs=xspec,
        out_shape=jax.ShapeDtypeStruct(x.shape, F32),
        scratch_shapes=[
            pltpu.VMEM((S5_NCT, bsz, S5_SLABS, LANES), F32),
            pltpu.VMEM((S5_NCT, bsz, S5_SLABS, LANES), F32),
            pltpu.VMEM((S5_SCAN_CT, bsz, S5_SLABS * S5_PITCH, LANES), F32),
            pltpu.VMEM((S5_SCAN_CT, bsz, S5_SLABS * S5_PITCH, LANES), F32),
            pltpu.VMEM((S5_SCAN_CT, bsz, S5_SLABS * S5_PITCH, LANES), F32),
            pltpu.VMEM((S5_SCAN_CT, bsz, S5_SLABS * S5_PITCH, LANES), F32),
        ],
        compiler_params=pltpu.CompilerParams(
            dimension_semantics=("arbitrary",), vmem_limit_bytes=VMEM_LIMIT),
        name="s5_layer",
    )(x, norm_g.reshape(1, d), w_in.astype(BF16), bblk, cblk, a_re, a_im,
      d_skip.reshape(1, d).astype(F32), w_out.astype(BF16))


def _gmlp_kernel(x_ref, wts_ref, g1_ref, g2_ref, g_ref, win_ref, vn_ref, ws_ref, bst_ref,
                 wout_ref, *rest):
    o_ref = rest[-1]
    tm = x_ref.shape[0]
    w = wts_ref[...]
    x = (x_ref[...] + w[:, 0:1] * _unpack_rows(g1_ref[...])
         + w[:, 1:2] * _unpack_rows(g2_ref[...]))
    xn = _rms(x, g_ref[...]).astype(BF16)
    u = jax.nn.gelu(jnp.dot(xn, win_ref[:, :GM_HALF], preferred_element_type=F32))
    v = jax.nn.gelu(jnp.dot(xn, win_ref[:, GM_HALF:], preferred_element_type=F32))
    vb = _rms(v, vn_ref[...]).astype(BF16)
    row = lax.broadcasted_iota(jnp.int32, (GM_SPAN, GM_SPAN), 0) // CHUNK
    col = lax.broadcasted_iota(jnp.int32, (GM_SPAN, GM_SPAN), 1) // CHUNK
    causal = row >= col
    ws = [jnp.where(causal, ws_ref[g], 0.0).astype(BF16) for g in range(GM_GROUPS)]
    spans = []
    for s in range(tm // GM_SPAN):
        parts = []
        for g in range(GM_GROUPS):
            vblk = vb[s * GM_SPAN:(s + 1) * GM_SPAN, g * GM_GROUP_CH:(g + 1) * GM_GROUP_CH]
            parts.append(jnp.dot(ws[g], vblk, preferred_element_type=F32)
                         + bst_ref[:, g:g + 1])
        spans.append(jnp.concatenate(parts, axis=1))
    mixed = jnp.concatenate(spans, axis=0)
    p = (u * mixed).astype(BF16)
    o_ref[...] = x + jnp.dot(p, wout_ref[...], preferred_element_type=F32)


def _per_part(call, x, moe_out, tm, extra_specs, extra_args, name):
    bsz, seq, d = x.shape
    n_tok = bsz * seq
    wts, gathered = moe_out
    n_tiles = n_tok // tm // len(gathered)
    out = None
    for h, rows in enumerate(gathered):
        def tile(i, h=h):
            return (i + h * n_tiles, 0)
        in_specs = [
            pl.BlockSpec((tm, d), tile),
            pl.BlockSpec((tm, LANES), tile),
            pl.BlockSpec((tm, PACKED), lambda i: (i, 0)),
            pl.BlockSpec((tm, PACKED), lambda i: (i + n_tiles, 0)),
        ] + extra_specs
        args = [x.reshape(n_tok, d), wts, rows, rows] + extra_args
        aliases = {}
        if out is not None:
            aliases = {len(args): 0}
            in_specs.append(pl.BlockSpec(memory_space=pl.ANY))
            args.append(out)
        out = pl.pallas_call(
            call,
            grid=(n_tiles,),
            in_specs=in_specs,
            out_specs=pl.BlockSpec((tm, d), tile),
            out_shape=jax.ShapeDtypeStruct((n_tok, d), F32),
            input_output_aliases=aliases,
            compiler_params=pltpu.CompilerParams(
                dimension_semantics=("parallel",), vmem_limit_bytes=VMEM_LIMIT),
            name=name,
        )(*args)
    return out.reshape(bsz, seq, d)


def _gmlp_layer(x, moe_out, norm_g, w_in, v_norm, w_s, b_s, w_out, tm=512):
    d = x.shape[-1]
    specs = [
        _const_spec((1, d)),
        _const_spec((d, 2 * GM_HALF)),
        _const_spec((1, GM_HALF)),
        _const_spec((GM_GROUPS, GM_SPAN, GM_SPAN)),
        _const_spec((GM_SPAN, GM_GROUPS)),
        _const_spec((GM_HALF, d)),
    ]
    args = [norm_g.reshape(1, d), w_in.astype(BF16), v_norm.reshape(1, GM_HALF),
            w_s.astype(F32), b_s.T.astype(F32), w_out.astype(BF16)]
    return _per_part(_gmlp_kernel, x, moe_out, tm, specs, args, "gmlp_layer")


def _norm_proj_kernel(x_ref, g_ref, w_ref, o_ref):
    xn = _rms(x_ref[...], g_ref[...]).astype(BF16)
    o_ref[...] = jnp.dot(xn, w_ref[...], preferred_element_type=F32).astype(o_ref.dtype)


def _norm_proj(x, g, w, out_dtype, tm=512, tn=1024):
    m, d = x.shape
    n = w.shape[1]
    return pl.pallas_call(
        _norm_proj_kernel,
        grid=(m // tm, n // tn),
        in_specs=[
            pl.BlockSpec((tm, d), lambda i, j: (i, 0)),
            pl.BlockSpec((1, d), lambda i, j: (0, 0)),
            pl.BlockSpec((d, tn), lambda i, j: (0, j)),
        ],
        out_specs=pl.BlockSpec((tm, tn), lambda i, j: (i, j)),
        out_shape=jax.ShapeDtypeStruct((m, n), out_dtype),
        compiler_params=pltpu.CompilerParams(
            dimension_semantics=("parallel", "parallel"), vmem_limit_bytes=VMEM_LIMIT),
        name="norm_proj",
    )(x, g.reshape(1, d), w.astype(BF16))


def _route(x, g_ref, wr_ref, bias_ref, tri_ref, run_ref):
    tm = x.shape[0]
    xn = _rms(x, g_ref[...])
    xhi = xn.astype(BF16)
    xlo = (xn - xhi.astype(F32)).astype(BF16)
    hi = jnp.dot(xhi, wr_ref[...], preferred_element_type=F32)
    lo = jnp.dot(xlo, wr_ref[:, :LANES], preferred_element_type=F32)
    logits = hi[:, :LANES] + hi[:, LANES:] + lo + bias_ref[...]
    lane = lax.broadcasted_iota(jnp.int32, (tm, LANES), 1).astype(F32)
    neg = jnp.float32(-jnp.inf)

    def first_argmax(vals):
        mx = jnp.max(vals, axis=-1, keepdims=True)
        idx = jnp.min(jnp.where(vals == mx, lane, float(LANES)), axis=-1, keepdims=True)
        return mx, idx

    gl = jnp.where(lane < MOE_GROUPS, logits, neg)
    gmax, gidx = first_argmax(gl)
    w_g = 1.0 / jnp.sum(jnp.exp(gl - gmax), axis=-1, keepdims=True)
    first = MOE_GROUPS + MOE_PER_GROUP * gidx
    el = jnp.where((lane >= first) & (lane < first + MOE_PER_GROUP), logits, neg)
    m1, i1 = first_argmax(el)
    m2, i2 = first_argmax(jnp.where(lane == i1, neg, el))
    e21 = jnp.exp(m2 - m1)
    w1 = w_g / (1.0 + e21)
    w2 = w_g * e21 / (1.0 + e21)
    e1 = i1 - MOE_GROUPS
    e2 = i2 - MOE_GROUPS

    onehot = ((lane == e1) | (lane == e2)).astype(BF16)
    tot = run_ref[...] + jnp.dot(tri_ref[...], onehot, preferred_element_type=F32)
    r1 = jnp.sum(jnp.where(lane == e1, tot, 0.0), axis=-1, keepdims=True)
    r2 = jnp.sum(jnp.where(lane == e2, tot, 0.0), axis=-1, keepdims=True)
    run_ref[...] = run_ref[...] + jnp.sum(onehot.astype(F32), axis=0, keepdims=True)

    meta = jnp.where(lane == 0, e1, jnp.where(lane == 1, e2,
                     jnp.where(lane == 2, r1, jnp.where(lane == 3, r2, 0.0))))
    meta_t = jnp.transpose(meta)[:SUBLANES, :].astype(jnp.int32)
    wts = jnp.where(lane == 0, w1, jnp.where(lane == 1, w2, 0.0))
    return meta_t, wts, xn


def _xattn_router_kernel(x_ref, g_ref, wq_ref, kt_ref, v_ref, wo_ref, mg_ref, wr_ref,
                         bias_ref, tri_ref, o_ref, meta_ref, wts_ref, cnt_ref, xp_ref, run_ref):
    @pl.when((pl.program_id(0) == 0) & (pl.program_id(1) == 0))
    def _():
        run_ref[...] = jnp.zeros_like(run_ref)

    tm = x_ref.shape[1]
    outs = []
    for r0 in range(0, tm, XA_SUB):
        x = x_ref[0, r0:r0 + XA_SUB, :]
        xn = _rms(x, g_ref[...]).astype(BF16)
        q = jnp.dot(xn, wq_ref[...], preferred_element_type=F32) * (XA_HEAD_DIM ** -0.5)
        q = q.astype(BF16)
        heads = []
        for h in range(XA_HEADS):
            cols = slice(h * XA_HEAD_DIM, (h + 1) * XA_HEAD_DIM)
            s = jnp.dot(q[:, cols], kt_ref[0, cols, :], preferred_element_type=F32)
            e = jnp.exp(s - jnp.max(s, axis=-1, keepdims=True))
            p = (e / jnp.sum(e, axis=-1, keepdims=True)).astype(BF16)
            heads.append(jnp.dot(p, v_ref[0, :, cols], preferred_element_type=F32))
        o = jnp.concatenate(heads, axis=1).astype(BF16)
        outs.append(x + jnp.dot(o, wo_ref[...], preferred_element_type=F32))
    out = jnp.concatenate(outs, axis=0)
    o_ref[0] = out
    meta, wts, xn_moe = _route(out, mg_ref, wr_ref, bias_ref, tri_ref, run_ref)
    meta_ref[...] = meta
    wts_ref[...] = wts
    cnt_ref[...] = run_ref[...]
    xp_ref[...] = _pack_rows(xn_moe)


XA_TM = 1024
GATHER_PARTS = 2


def _xattn_router_layer(x, mem, mem_g, norm_g, w_q, w_kv, w_o, moe_g, w_group, b_group,
                        w_expert, b_expert):
    bsz, m, d = mem.shape
    tm = XA_TM
    kv = _norm_proj(mem.reshape(bsz * m, d), mem_g, w_kv, BF16).reshape(bsz, m, 2 * d)
    kt = kv[..., :d].transpose(0, 2, 1)
    v = kv[..., d:]
    nr = MOE_GROUPS + MOE_EXPERTS
    w_r = jnp.zeros((d, LANES), F32).at[:, :MOE_GROUPS].set(w_group.astype(F32))
    w_r = w_r.at[:, MOE_GROUPS:nr].set(w_expert.astype(F32))
    bias = jnp.zeros((1, LANES), F32).at[0, :MOE_GROUPS].set(b_group.astype(F32))
    bias = bias.at[0, MOE_GROUPS:nr].set(b_expert.astype(F32))
    w_hi = w_r.astype(BF16)
    w_lo = (w_r - w_hi.astype(F32)).astype(BF16)
    w_hilo = jnp.concatenate([w_hi, w_lo], axis=1)
    earlier = jnp.tril(jnp.ones((tm, tm), BF16), -1)
    nb, seq = x.shape[0], x.shape[1]
    nt = seq // tm
    xspec = pl.BlockSpec((1, tm, d), lambda b, i: (b, i, 0))
    return pl.pallas_call(
        _xattn_router_kernel,
        grid=(nb, nt),
        in_specs=[
            xspec,
            _const_spec((1, d)),
            _const_spec((d, d)),
            pl.BlockSpec((1, d, m), lambda b, i: (b, 0, 0)),
            pl.BlockSpec((1, m, d), lambda b, i: (b, 0, 0)),
            _const_spec((d, d)),
            _const_spec((1, d)),
            _const_spec((d, 2 * LANES)),
            _const_spec((1, LANES)),
            _const_spec((tm, tm)),
        ],
        out_specs=[
            xspec,
            pl.BlockSpec((SUBLANES, tm), lambda b, i: (0, b * nt + i)),
            pl.BlockSpec((tm, LANES), lambda b, i: (b * nt + i, 0)),
            pl.BlockSpec((1, LANES), lambda b, i: (0, 0)),
            pl.BlockSpec((tm, d // 2), lambda b, i: (b * nt + i, 0)),
        ],
        out_shape=[
            jax.ShapeDtypeStruct((nb, seq, d), F32),
            jax.ShapeDtypeStruct((SUBLANES, nb * seq), jnp.int32),
            jax.ShapeDtypeStruct((nb * seq, LANES), F32),
            jax.ShapeDtypeStruct((1, LANES), F32),
            jax.ShapeDtypeStruct((nb * seq, d // 2), jnp.int32),
        ],
        scratch_shapes=[pltpu.VMEM((1, LANES), F32)],
        compiler_params=pltpu.CompilerParams(
            dimension_semantics=("arbitrary", "arbitrary"), vmem_limit_bytes=VMEM_LIMIT),
        name="xattn_router",
    )(x, norm_g.reshape(1, d), w_q.astype(BF16), kt, v, w_o.astype(BF16),
      moe_g.reshape(1, d), w_hilo, bias, earlier)


PACKED = D_MODEL // 2
I32 = jnp.int32
HI_HALF = -65536
LO_HALF = 65535


def _pack_rows(v):
    lo = lax.bitcast_convert_type(v[:, :PACKED].astype(BF16).astype(F32), I32)
    hi = lax.bitcast_convert_type(v[:, PACKED:].astype(BF16).astype(F32), I32)
    return (hi & HI_HALF) | ((lo >> 16) & LO_HALF)


def _unpack_rows(p):
    lo = lax.bitcast_convert_type(p << 16, F32)
    hi = lax.bitcast_convert_type(p & HI_HALF, F32)
    return jnp.concatenate([lo, hi], axis=1)


def _expert_kernel(row0_ref, cnt_ref, nfull_ref, rem_ref, ustart_ref, used_ref, xd_hbm, wg_ref,
                   wu_ref, wd_ref, y_hbm, wgb, wub, wdb, xbuf, ybuf, cls_ref, lsem, ssem):
    e = pl.program_id(0)
    used_rows = used_ref[0]
    row0 = row0_ref[e]
    cnt = cnt_ref[e]
    nfull = nfull_ref[e]
    rem = rem_ref[e]
    u0 = ustart_ref[e]
    sizes = (MOE_UNIT,) + MOE_TAILS

    def load(r, slot):
        src = xd_hbm.at[pl.ds(pl.multiple_of(r, MOE_BLK), MOE_UNIT), :]
        return pltpu.make_async_copy(src, xbuf.at[slot], lsem.at[slot])

    def store(r, slot, size):
        dst = y_hbm.at[pl.ds(pl.multiple_of(r, MOE_BLK), size), :]
        return pltpu.make_async_copy(ybuf.at[slot, pl.ds(0, size), :], dst, ssem.at[slot])

    def wait_store(slot):
        for c, size in enumerate(sizes):
            @pl.when(cls_ref[slot] == c + 1)
            def _():
                store(0, slot, size).wait()

    @pl.when(e == 0)
    def _():
        cls_ref[0] = 0
        cls_ref[1] = 0

        @pl.when(used_rows > 0)
        def _():
            load(0, 0).start()

    @pl.when(nfull + rem > 0)
    def _():
        wgb[...] = wg_ref[...].astype(BF16)
        wub[...] = wu_ref[...].astype(BF16)
        wdb[...] = wd_ref[...].astype(BF16)

    def unit(u, r, c):
        size = sizes[c]
        slot = u % 2
        load(r, slot).wait()

        @pl.when(r + size < used_rows)
        def _():
            load(r + size, 1 - slot).start()

        wait_store(slot)
        row = lax.broadcasted_iota(jnp.int32, (size, PACKED), 0)
        xd = jnp.where(row < cnt - (r - row0), xbuf[slot, pl.ds(0, size), :], 0)
        xb = _unpack_rows(xd).astype(BF16)
        a = jnp.dot(xb, wgb[...], preferred_element_type=F32)
        up = jnp.dot(xb, wub[...], preferred_element_type=F32)
        h = (a * (1.0 / (1.0 + jnp.exp(-a))) * up).astype(BF16)
        ybuf[slot, pl.ds(0, size), :] = _pack_rows(
            jnp.dot(h, wdb[...], preferred_element_type=F32))
        store(r, slot, size).start()
        cls_ref[slot] = c + 1

    def full_unit(k, carry):
        unit(u0 + k, row0 + k * MOE_UNIT, 0)
        return carry

    lax.fori_loop(0, nfull, full_unit, 0)

    u = u0 + nfull
    r = row0 + nfull * MOE_UNIT
    for c, size in enumerate(MOE_TAILS, start=1):
        has = (rem & (size // MOE_BLK)) != 0

        @pl.when(has)
        def _():
            unit(u, r, c)
        u = u + has.astype(jnp.int32)
        r = r + has.astype(jnp.int32) * size

    @pl.when(e == pl.num_programs(0) - 1)
    def _():
        wait_store(0)
        wait_store(1)


def _moe_experts(xd, seg_start, counts, layer, w_gate, w_up, w_down):
    n_pad = xd.shape[0]
    d = D_MODEL
    nblk = (counts + MOE_BLK - 1) // MOE_BLK
    per_unit = MOE_UNIT // MOE_BLK
    nfull = nblk // per_unit
    rem = nblk % per_unit
    n_units = nfull + sum(((rem & (size // MOE_BLK)) != 0).astype(jnp.int32)
                          for size in MOE_TAILS)
    ustart = jnp.cumsum(n_units) - n_units
    used_rows = (seg_start[-1:] + nblk[-1:] * MOE_BLK).astype(jnp.int32)

    def expert_blk(e, *_):
        return (layer, e, 0, 0)

    return pl.pallas_call(
        _expert_kernel,
        grid_spec=pltpu.PrefetchScalarGridSpec(
            num_scalar_prefetch=6,
            grid=(MOE_EXPERTS,),
            in_specs=[
                pl.BlockSpec(memory_space=pl.ANY),
                pl.BlockSpec((None, None, d, MOE_HIDDEN), expert_blk),
                pl.BlockSpec((None, None, d, MOE_HIDDEN), expert_blk),
                pl.BlockSpec((None, None, MOE_HIDDEN, d), expert_blk),
            ],
            out_specs=pl.BlockSpec(memory_space=pl.ANY),
            scratch_shapes=[pltpu.VMEM((d, MOE_HIDDEN), BF16), pltpu.VMEM((d, MOE_HIDDEN), BF16),
                            pltpu.VMEM((MOE_HIDDEN, d), BF16),
                            pltpu.VMEM((2, MOE_UNIT, PACKED), I32),
                            pltpu.VMEM((2, MOE_UNIT, PACKED), I32),
                            pltpu.SMEM((2,), jnp.int32),
                            pltpu.SemaphoreType.DMA((2,)), pltpu.SemaphoreType.DMA((2,))],
        ),
        out_shape=jax.ShapeDtypeStruct((n_pad, PACKED), I32),
        compiler_params=pltpu.CompilerParams(
            dimension_semantics=("arbitrary",), vmem_limit_bytes=VMEM_LIMIT),
        name="moe_experts",
    )(seg_start.astype(jnp.int32), counts.astype(jnp.int32), nfull.astype(jnp.int32),
      rem.astype(jnp.int32), ustart.astype(jnp.int32), used_rows, xd, w_gate, w_up, w_down)


def _pos_kernel(ps_ref, meta_ref, pos_ref):
    m = meta_ref[...]
    e = m[0:2, :]
    base = jnp.zeros_like(e)
    for k in range(MOE_EXPERTS):
        base = jnp.where(e == k, ps_ref[k], base)
    pos_ref[...] = base + m[2:4, :]


def _moe_positions(pad_start, meta, tm=2048):
    n_tok = meta.shape[1]
    return pl.pallas_call(
        _pos_kernel,
        grid_spec=pltpu.PrefetchScalarGridSpec(
            num_scalar_prefetch=1,
            grid=(n_tok // tm,),
            in_specs=[pl.BlockSpec((SUBLANES, tm), lambda i, ps: (0, i))],
            out_specs=pl.BlockSpec((2, tm), lambda i, ps: (0, i)),
        ),
        out_shape=jax.ShapeDtypeStruct((2, n_tok), jnp.int32),
        compiler_params=pltpu.CompilerParams(dimension_semantics=("parallel",)),
        name="moe_positions",
    )(pad_start, meta)


SC_CORES = 2
SC_SUBCORES = 16
SC_CHUNK = 32
SC_NBUF = 4


def _sc_gather_rows(table, idx, after):
    n_rows = idx.shape[0]
    width = table.shape[1]
    workers = SC_CORES * SC_SUBCORES
    per_worker = n_rows // workers
    n_chunks = per_worker // SC_CHUNK
    mesh = plsc.VectorSubcoreMesh(core_axis_name="c", subcore_axis_name="s")

    def body(table_hbm, idx_hbm, after_hbm, out_hbm, *scratch):
        idx_v = scratch[:SC_NBUF]
        rows_v = scratch[SC_NBUF:2 * SC_NBUF]
        gsem = scratch[2 * SC_NBUF:3 * SC_NBUF]
        ssem = scratch[3 * SC_NBUF:]
        wid = lax.axis_index("s") * SC_CORES + lax.axis_index("c")
        base = wid * per_worker

        def chunk(j):
            return pl.ds(pl.multiple_of(base + j * SC_CHUNK, SC_CHUNK), SC_CHUNK)

        def gather(b):
            return pltpu.make_async_copy(table_hbm.at[idx_v[b]], rows_v[b], gsem[b])

        def start_gather(j, b):
            pltpu.sync_copy(idx_hbm.at[chunk(j)], idx_v[b])
            gather(b).start()

        def store(j, b):
            return pltpu.make_async_copy(rows_v[b], out_hbm.at[chunk(j)], ssem[b])

        for b in range(SC_NBUF):
            start_gather(b, b)

        @pl.loop(0, n_chunks, step=SC_NBUF)
        def _(g):
            for b in range(SC_NBUF):
                gather(b).wait()
                store(g + b, b).start()
            for b in range(SC_NBUF):
                nxt = g + SC_NBUF + b

                @pl.when(nxt < n_chunks)
                def _():
                    store(g + b, b).wait()
                    start_gather(nxt, b)

        for b in range(SC_NBUF):
            store(0, b).wait()

    assert n_chunks % SC_NBUF == 0 and per_worker % SC_CHUNK == 0
    return pl.kernel(
        body,
        out_type=jax.ShapeDtypeStruct((n_rows, width), table.dtype),
        mesh=mesh,
        scratch_types=([pltpu.VMEM((SC_CHUNK,), jnp.int32)] * SC_NBUF
                       + [pltpu.VMEM((SC_CHUNK, width), table.dtype)] * SC_NBUF
                       + [pltpu.SemaphoreType.DMA] * (2 * SC_NBUF)),
        name="sc_gather_rows",
    )(table, idx, after)


def _sc_scatter_rows(rows, idx, n_out):
    n_rows, width = rows.shape
    workers = SC_CORES * SC_SUBCORES
    per_worker = n_rows // workers
    n_chunks = per_worker // SC_CHUNK
    mesh = plsc.VectorSubcoreMesh(core_axis_name="c", subcore_axis_name="s")

    def body(rows_hbm, idx_hbm, out_hbm, *scratch):
        idx_v = scratch[:2 * SC_NBUF]
        rows_v = scratch[2 * SC_NBUF:3 * SC_NBUF]
        lsem = scratch[3 * SC_NBUF:4 * SC_NBUF]
        ssem = scratch[4 * SC_NBUF:]
        wid = lax.axis_index("s") * SC_CORES + lax.axis_index("c")
        base = wid * per_worker

        def chunk(j, k=0):
            return pl.ds(pl.multiple_of(k * n_rows + base + j * SC_CHUNK, SC_CHUNK), SC_CHUNK)

        def load(j, b):
            return pltpu.make_async_copy(rows_hbm.at[chunk(j)], rows_v[b], lsem[b])

        def scatter(b, k):
            return pltpu.make_async_copy(rows_v[b], out_hbm.at[idx_v[2 * b + k]], ssem[b])

        for b in range(SC_NBUF):
            load(b, b).start()

        @pl.loop(0, n_chunks, step=SC_NBUF)
        def _(g):
            for b in range(SC_NBUF):
                for k in range(2):
                    pltpu.sync_copy(idx_hbm.at[chunk(g + b, k)], idx_v[2 * b + k])
                load(g + b, b).wait()
                for k in range(2):
                    scatter(b, k).start()
            for b in range(SC_NBUF):
                nxt = g + SC_NBUF + b

                @pl.when(nxt < n_chunks)
                def _():
                    for k in range(2):
                        scatter(b, k).wait()
                    load(nxt, b).start()

        for b in range(SC_NBUF):
            for k in range(2):
                scatter(b, k).wait()

    assert n_chunks % SC_NBUF == 0 and per_worker % SC_CHUNK == 0
    return pl.kernel(
        body,
        out_type=jax.ShapeDtypeStruct((n_out, width), rows.dtype),
        mesh=mesh,
        scratch_types=([pltpu.VMEM((SC_CHUNK,), jnp.int32)] * (2 * SC_NBUF)
                       + [pltpu.VMEM((SC_CHUNK, width), rows.dtype)] * SC_NBUF
                       + [pltpu.SemaphoreType.DMA] * (2 * SC_NBUF)),
        name="sc_scatter_rows",
    )(rows, idx)


def _combine_kernel(x_ref, wts_ref, g1_ref, g2_ref, fg_ref, *rest, final):
    o_ref = rest[-1]
    w = wts_ref[...]
    out = (x_ref[...] + w[:, 0:1] * _unpack_rows(g1_ref[...])
           + w[:, 1:2] * _unpack_rows(g2_ref[...]))
    if final:
        out = _rms(out, fg_ref[...])
    o_ref[...] = out


def _moe_layer(meta, cnt, xp, layer, w_gate, w_up, w_down):
    n_tok = xp.shape[0]
    counts = cnt[0, :MOE_EXPERTS].astype(jnp.int32)
    padded = ((counts + MOE_BLK - 1) // MOE_BLK) * MOE_BLK
    pad_start = jnp.cumsum(padded) - padded
    n_pad = n_tok * 2 + MOE_EXPERTS * MOE_BLK + MOE_UNIT

    pos = _moe_positions(pad_start, meta)
    xd = _sc_scatter_rows(xp, pos.reshape(2 * n_tok), n_pad)
    y = _moe_experts(xd, pad_start, counts, layer, w_gate, w_up, w_down)

    per_part = n_tok // GATHER_PARTS
    gathered = []
    for h in range(GATHER_PARTS):
        idx = pos[:, h * per_part:(h + 1) * per_part].reshape(2 * per_part)
        gathered.append(_sc_gather_rows(y, idx, gathered[-1] if gathered else idx))
    return gathered


def _moe_combine(x, moe_out, final_g=None):
    d = x.shape[-1]
    final = final_g is not None
    fg = (final_g if final else jnp.ones((d,), F32)).reshape(1, d)
    return _per_part(functools.partial(_combine_kernel, final=final), x, moe_out, COMBINE_TM,
                     [pl.BlockSpec((1, d), lambda i: (0, 0))], [fg], "moe_combine")


def kernel(x, mem, s5_norm, s5_w_in, s5_lambda_re, s5_lambda_im, s5_log_dt, s5_b_re, s5_b_im, s5_c_re, s5_c_im, s5_d, s5_w_out, gm_norm, gm_w_in, gm_v_norm, gm_w_s, gm_b_s, gm_w_out, mem_norm, xa_norm, xa_w_q, xa_w_kv, xa_w_o, moe_norm, moe_w_group, moe_b_group, moe_w_expert, moe_b_expert, moe_w_gate, moe_w_up, moe_w_down, final_norm):
    moe_out = None
    for i in range(DEPTH):
        j = i // 2
        if i % 2 == 0:
            if moe_out is not None:
                x = _moe_combine(x, moe_out)
            x = _s5_layer(x, s5_norm[j], s5_w_in[j], s5_lambda_re[j], s5_lambda_im[j],
                          s5_log_dt[j], s5_b_re[j], s5_b_im[j], s5_c_re[j], s5_c_im[j],
                          s5_d[j], s5_w_out[j])
        else:
            x = _gmlp_layer(x, moe_out, gm_norm[j], gm_w_in[j], gm_v_norm[j], gm_w_s[j],
                            gm_b_s[j], gm_w_out[j])
        x, meta, wts, cnt, xp = _xattn_router_layer(
            x, mem, mem_norm, xa_norm[i], xa_w_q[i], xa_w_kv[i], xa_w_o[i], moe_norm[i],
            moe_w_group[i], moe_b_group[i], moe_w_expert[i], moe_b_expert[i])
        moe_out = (wts, _moe_layer(meta, cnt, xp, i, moe_w_gate, moe_w_up, moe_w_down))
    return _moe_combine(x, moe_out, final_g=final_norm)
```

```python
import functools

import jax
import jax.numpy as jnp
from jax import lax
from jax.experimental import pallas as pl
from jax.experimental.pallas import tpu as pltpu
from jax.experimental.pallas import tpu_sc as plsc

F32 = jnp.float32
BF16 = jnp.bfloat16

D_MODEL = 1024
DEPTH = 2
CHUNK = 64
S5_GROUP_CH = 16
S5_GROUPS = 64
S5_STATE = 64
GM_HALF = 2 * D_MODEL
GM_GROUPS = 8
GM_SPAN = 128
GM_GROUP_CH = GM_HALF // GM_GROUPS
XA_HEADS = 4
XA_HEAD_DIM = D_MODEL // XA_HEADS
XA_SUB = 512
MOE_GROUPS = 4
MOE_PER_GROUP = 8
MOE_EXPERTS = MOE_GROUPS * MOE_PER_GROUP
MOE_HIDDEN = D_MODEL // 2
RMS_EPS = 1e-6

LANES = 128
SUBLANES = 8
VMEM_LIMIT = 56 * 1024 * 1024

S5_CT = 256
S5_NCT = D_MODEL // S5_CT
S5_STATES_CT = (S5_CT // S5_GROUP_CH) * S5_STATE
S5_SLABS = S5_STATES_CT // LANES
S5_TC = 128
S5_PITCH = S5_TC + 4
S5_SCAN_CT = 2

MOE_BLK = 256
MOE_UNIT = 1024
MOE_TAILS = (512, 256)
COMBINE_TM = 512


def _rms(x, g):
    ms = jnp.mean(x * x, axis=-1, keepdims=True)
    return x * lax.rsqrt(ms + RMS_EPS) * g


def _const_spec(shape):
    nd = len(shape)
    return pl.BlockSpec(shape, lambda *_: (0,) * nd, pipeline_mode=pl.Buffered(1))


def _s5_kernel(x_ref, g_ref, win_ref, bblk_ref, cblk_ref, are_ref, aim_ref, d_ref,
               wout_ref, o_ref, hre_ref, him_ref, bre_ref, bim_ref, sre_ref, sim_ref):
    nb = x_ref.shape[0]
    tc = x_ref.shape[1]

    @pl.when(pl.program_id(0) == 0)
    def _():
        hre_ref[...] = jnp.zeros_like(hre_ref)
        him_ref[...] = jnp.zeros_like(him_ref)

    x = x_ref[...].reshape(nb * tc, D_MODEL)
    xn = _rms(x, g_ref[...]).astype(BF16)
    u = jnp.dot(xn, win_ref[...], preferred_element_type=F32)

    y_parts = []
    for ct0 in range(0, S5_NCT, S5_SCAN_CT):
        tiles = range(ct0, ct0 + S5_SCAN_CT)
        for c, ct in enumerate(tiles):
            u_ct = u[:, ct * S5_CT:(ct + 1) * S5_CT]
            bu = jnp.dot(u_ct.astype(BF16), bblk_ref[ct], preferred_element_type=F32)
            for b in range(nb):
                for j in range(S5_SLABS):
                    rows = slice(b * tc, (b + 1) * tc)
                    bre_ref[c, b, pl.ds(j * S5_PITCH, tc), :] = bu[rows, j * LANES:(j + 1) * LANES]
                    bim_ref[c, b, pl.ds(j * S5_PITCH, tc), :] = bu[
                        rows, S5_STATES_CT + j * LANES:S5_STATES_CT + (j + 1) * LANES]
        chains = [(c, ct, b) for c, ct in enumerate(tiles) for b in range(nb)]
        a_re = [are_ref[ct] for ct in tiles]
        a_im = [aim_ref[ct] for ct in tiles]

        def step(t, carry):
            idx = pl.ds(t, S5_SLABS, stride=S5_PITCH)
            bu_t = [(bre_ref[c, b, idx, :], bim_ref[c, b, idx, :]) for c, _, b in chains]
            new = []
            for k, (c, _, b) in enumerate(chains):
                hr, hi = carry[2 * k], carry[2 * k + 1]
                new.append(a_re[c] * hr - a_im[c] * hi + bu_t[k][0])
                new.append(a_re[c] * hi + a_im[c] * hr + bu_t[k][1])
            for k, (c, _, b) in enumerate(chains):
                sre_ref[c, b, idx, :] = new[2 * k]
                sim_ref[c, b, idx, :] = new[2 * k + 1]
            return tuple(new)

        init = []
        for _, ct, b in chains:
            init += [hre_ref[ct, b], him_ref[ct, b]]
        fin = lax.fori_loop(0, tc, step, tuple(init), unroll=2)
        for k, (_, ct, b) in enumerate(chains):
            hre_ref[ct, b] = fin[2 * k]
            him_ref[ct, b] = fin[2 * k + 1]

        for c, ct in enumerate(tiles):
            xs = []
            for b in range(nb):
                cols = [sre_ref[c, b, pl.ds(j * S5_PITCH, tc), :] for j in range(S5_SLABS)]
                cols += [sim_ref[c, b, pl.ds(j * S5_PITCH, tc), :] for j in range(S5_SLABS)]
                xs.append(jnp.concatenate(cols, axis=1))
            xst = jnp.concatenate(xs, axis=0).astype(BF16)
            y_parts.append(jnp.dot(xst, cblk_ref[ct], preferred_element_type=F32))

    y = jnp.concatenate(y_parts, axis=1) + d_ref[...] * u
    yg = jax.nn.gelu(y).astype(BF16)
    z = jnp.dot(yg, wout_ref[...], preferred_element_type=F32)
    out = x + z[:, :D_MODEL] * (1.0 / (1.0 + jnp.exp(-z[:, D_MODEL:])))
    o_ref[...] = out.reshape(nb, tc, D_MODEL)


def _s5_discretize(lam_re, lam_im, log_dt, b_re, b_im, c_re, c_im):
    lr = lam_re.astype(F32)
    li = lam_im.astype(F32)
    dt = jnp.exp(log_dt.astype(F32))[:, None]
    mag = jnp.exp(lr * dt)
    ab_re = mag * jnp.cos(li * dt)
    ab_im = mag * jnp.sin(li * dt)
    den = lr * lr + li * li
    coef_re = ((ab_re - 1.0) * lr + ab_im * li) / den
    coef_im = (ab_im * lr - (ab_re - 1.0) * li) / den
    br = b_re.astype(F32)
    bi = b_im.astype(F32)
    bb_re = coef_re[..., None] * br - coef_im[..., None] * bi
    bb_im = coef_re[..., None] * bi + coef_im[..., None] * br
    gpt = S5_CT // S5_GROUP_CH
    ch_group = jnp.arange(S5_CT, dtype=jnp.int32) // S5_GROUP_CH
    st_group = jnp.arange(S5_STATES_CT, dtype=jnp.int32) // S5_STATE
    same = ch_group[:, None] == st_group[None, :]

    def in_blocks(bb):
        t = bb.reshape(S5_NCT, gpt, S5_STATE, S5_GROUP_CH).transpose(0, 3, 1, 2)
        t = t.reshape(S5_NCT, 1, S5_GROUP_CH, S5_STATES_CT)
        t = jnp.broadcast_to(t, (S5_NCT, gpt, S5_GROUP_CH, S5_STATES_CT))
        return jnp.where(same[None], t.reshape(S5_NCT, S5_CT, S5_STATES_CT), 0.0)

    def out_blocks(c):
        return in_blocks(c.transpose(0, 2, 1)).transpose(0, 2, 1)

    bblk = jnp.concatenate([in_blocks(bb_re), in_blocks(bb_im)], axis=2).astype(BF16)
    cblk = jnp.concatenate([out_blocks(c_re.astype(F32)),
                            out_blocks(-c_im.astype(F32))], axis=1).astype(BF16)
    a_re = ab_re.reshape(S5_NCT, S5_SLABS, LANES)
    a_im = ab_im.reshape(S5_NCT, S5_SLABS, LANES)
    return bblk, cblk, a_re, a_im


def _s5_layer(x, norm_g, w_in, lam_re, lam_im, log_dt, b_re, b_im, c_re, c_im, d_skip, w_out):
    bsz, seq, d = x.shape
    bblk, cblk, a_re, a_im = _s5_discretize(lam_re, lam_im, log_dt, b_re, b_im, c_re, c_im)
    xspec = pl.BlockSpec((bsz, S5_TC, d), lambda k: (0, k, 0))
    return pl.pallas_call(
        _s5_kernel,
        grid=(seq // S5_TC,),
        in_specs=[
            xspec,
            _const_spec((1, d)),
            _const_spec((d, d)),
            _const_spec(bblk.shape),
            _const_spec(cblk.shape),
            _const_spec(a_re.shape),
            _const_spec(a_im.shape),
            _const_spec((1, d)),
            _const_spec((d, 2 * d)),
        ],
        out_specs=xspec,
        out_shape=jax.ShapeDtypeStruct(x.shape, F32),
        scratch_shapes=[
            pltpu.VMEM((S5_NCT, bsz, S5_SLABS, LANES), F32),
            pltpu.VMEM((S5_NCT, bsz, S5_SLABS, LANES), F32),
            pltpu.VMEM((S5_SCAN_CT, bsz, S5_SLABS * S5_PITCH, LANES), F32),
            pltpu.VMEM((S5_SCAN_CT, bsz, S5_SLABS * S5_PITCH, LANES), F32),
            pltpu.VMEM((S5_SCAN_CT, bsz, S5_SLABS * S5_PITCH, LANES), F32),
            pltpu.VMEM((S5_SCAN_CT, bsz, S5_SLABS * S5_PITCH, LANES), F32),
        ],
        compiler_params=pltpu.CompilerParams(
            dimension_semantics=("arbitrary",), vmem_limit_bytes=VMEM_LIMIT),
        name="s5_layer",
    )(x, norm_g.reshape(1, d), w_in.astype(BF16), bblk, cblk, a_re, a_im,
      d_skip.reshape(1, d).astype(F32), w_out.astype(BF16))


def _gmlp_kernel(x_ref, wts_ref, g1_ref, g2_ref, g_ref, win_ref, vn_ref, ws_ref, bst_ref,
                 wout_ref, *rest):
    o_ref = rest[-1]
    tm = x_ref.shape[0]
    w = wts_ref[...]
    x = (x_ref[...] + w[:, 0:1] * _unpack_rows(g1_ref[...])
         + w[:, 1:2] * _unpack_rows(g2_ref[...]))
    xn = _rms(x, g_ref[...]).astype(BF16)
    v = jax.nn.gelu(jnp.dot(xn, win_ref[:, GM_HALF:], preferred_element_type=F32))
    vb = _rms(v, vn_ref[...]).astype(BF16)
    u = jax.nn.gelu(jnp.dot(xn, win_ref[:, :GM_HALF], preferred_element_type=F32))
    row = lax.broadcasted_iota(jnp.int32, (GM_SPAN, GM_SPAN), 0) // CHUNK
    col = lax.broadcasted_iota(jnp.int32, (GM_SPAN, GM_SPAN), 1) // CHUNK
    causal = row >= col
    ws = [jnp.where(causal, ws_ref[g], 0.0).astype(BF16) for g in range(GM_GROUPS)]
    spans = []
    for s in range(tm // GM_SPAN):
        parts = []
        for g in range(GM_GROUPS):
            vblk = vb[s * GM_SPAN:(s + 1) * GM_SPAN, g * GM_GROUP_CH:(g + 1) * GM_GROUP_CH]
            parts.append(jnp.dot(ws[g], vblk, preferred_element_type=F32)
                         + bst_ref[:, g:g + 1])
        spans.append(jnp.concatenate(parts, axis=1))
    mixed = jnp.concatenate(spans, axis=0)
    p = (u * mixed).astype(BF16)
    o_ref[...] = x + jnp.dot(p, wout_ref[...], preferred_element_type=F32)


def _per_part(call, x, moe_out, tm, extra_specs, extra_args, name):
    bsz, seq, d = x.shape
    n_tok = bsz * seq
    wts, gathered = moe_out
    n_tiles = n_tok // tm // len(gathered)
    out = None
    for h, rows in enumerate(gathered):
        def tile(i, h=h):
            return (i + h * n_tiles, 0)
        in_specs = [
            pl.BlockSpec((tm, d), tile),
            pl.BlockSpec((tm, LANES), tile),
            pl.BlockSpec((tm, PACKED), lambda i: (i, 0)),
            pl.BlockSpec((tm, PACKED), lambda i: (i + n_tiles, 0)),
        ] + extra_specs
        args = [x.reshape(n_tok, d), wts, rows, rows] + extra_args
        aliases = {}
        if out is not None:
            aliases = {len(args): 0}
            in_specs.append(pl.BlockSpec(memory_space=pl.ANY))
            args.append(out)
        out = pl.pallas_call(
            call,
            grid=(n_tiles,),
            in_specs=in_specs,
            out_specs=pl.BlockSpec((tm, d), tile),
            out_shape=jax.ShapeDtypeStruct((n_tok, d), F32),
            input_output_aliases=aliases,
            compiler_params=pltpu.CompilerParams(
                dimension_semantics=("parallel",), vmem_limit_bytes=VMEM_LIMIT),
            name=name,
        )(*args)
    return out.reshape(bsz, seq, d)


def _gmlp_layer(x, moe_out, norm_g, w_in, v_norm, w_s, b_s, w_out, tm=512):
    d = x.shape[-1]
    specs = [
        _const_spec((1, d)),
        _const_spec((d, 2 * GM_HALF)),
        _const_spec((1, GM_HALF)),
        _const_spec((GM_GROUPS, GM_SPAN, GM_SPAN)),
        _const_spec((GM_SPAN, GM_GROUPS)),
        _const_spec((GM_HALF, d)),
    ]
    args = [norm_g.reshape(1, d), w_in.astype(BF16), v_norm.reshape(1, GM_HALF),
            w_s.astype(F32), b_s.T.astype(F32), w_out.astype(BF16)]
    return _per_part(_gmlp_kernel, x, moe_out, tm, specs, args, "gmlp_layer")


def _norm_proj_kernel(x_ref, g_ref, w_ref, o_ref):
    xn = _rms(x_ref[...], g_ref[...]).astype(BF16)
    o_ref[...] = jnp.dot(xn, w_ref[...], preferred_element_type=F32).astype(o_ref.dtype)


def _norm_proj(x, g, w, out_dtype, tm=512, tn=1024):
    m, d = x.shape
    n = w.shape[1]
    return pl.pallas_call(
        _norm_proj_kernel,
        grid=(m // tm, n // tn),
        in_specs=[
            pl.BlockSpec((tm, d), lambda i, j: (i, 0)),
            pl.BlockSpec((1, d), lambda i, j: (0, 0)),
            pl.BlockSpec((d, tn), lambda i, j: (0, j)),
        ],
        out_specs=pl.BlockSpec((tm, tn), lambda i, j: (i, j)),
        out_shape=jax.ShapeDtypeStruct((m, n), out_dtype),
        compiler_params=pltpu.CompilerParams(
            dimension_semantics=("parallel", "parallel"), vmem_limit_bytes=VMEM_LIMIT),
        name="norm_proj",
    )(x, g.reshape(1, d), w.astype(BF16))


def _route(x, g_ref, wr_ref, bias_ref, tri_ref, run_ref):
    tm = x.shape[0]
    xn = _rms(x, g_ref[...])
    xhi = xn.astype(BF16)
    xlo = (xn - xhi.astype(F32)).astype(BF16)
    hi = jnp.dot(xhi, wr_ref[...], preferred_element_type=F32)
    lo = jnp.dot(xlo, wr_ref[:, :LANES], preferred_element_type=F32)
    logits = hi[:, :LANES] + hi[:, LANES:] + lo + bias_ref[...]
    lane = lax.broadcasted_iota(jnp.int32, (tm, LANES), 1).astype(F32)
    neg = jnp.float32(-jnp.inf)

    def first_argmax(vals):
        mx = jnp.max(vals, axis=-1, keepdims=True)
        idx = jnp.min(jnp.where(vals == mx, lane, float(LANES)), axis=-1, keepdims=True)
        return mx, idx

    gl = jnp.where(lane < MOE_GROUPS, logits, neg)
    gmax, gidx = first_argmax(gl)
    w_g = 1.0 / jnp.sum(jnp.exp(gl - gmax), axis=-1, keepdims=True)
    first = MOE_GROUPS + MOE_PER_GROUP * gidx
    el = jnp.where((lane >= first) & (lane < first + MOE_PER_GROUP), logits, neg)
    m1, i1 = first_argmax(el)
    m2, i2 = first_argmax(jnp.where(lane == i1, neg, el))
    e21 = jnp.exp(m2 - m1)
    w1 = w_g / (1.0 + e21)
    w2 = w_g * e21 / (1.0 + e21)
    e1 = i1 - MOE_GROUPS
    e2 = i2 - MOE_GROUPS

    onehot = ((lane == e1) | (lane == e2)).astype(BF16)
    tot = run_ref[...] + jnp.dot(tri_ref[...], onehot, preferred_element_type=F32)
    r1 = jnp.sum(jnp.where(lane == e1, tot, 0.0), axis=-1, keepdims=True)
    r2 = jnp.sum(jnp.where(lane == e2, tot, 0.0), axis=-1, keepdims=True)
    run_ref[...] = run_ref[...] + jnp.sum(onehot.astype(F32), axis=0, keepdims=True)

    meta = jnp.where(lane == 0, e1, jnp.where(lane == 1, e2,
                     jnp.where(lane == 2, r1, jnp.where(lane == 3, r2, 0.0))))
    meta_t = jnp.transpose(meta)[:SUBLANES, :].astype(jnp.int32)
    wts = jnp.where(lane == 0, w1, jnp.where(lane == 1, w2, 0.0))
    return meta_t, wts, xn


def _xattn_router_kernel(x_ref, g_ref, wq_ref, kt_ref, v_ref, wo_ref, mg_ref, wr_ref,
                         bias_ref, tri_ref, o_ref, meta_ref, wts_ref, cnt_ref, xp_ref, run_ref):
    @pl.when((pl.program_id(0) == 0) & (pl.program_id(1) == 0))
    def _():
        run_ref[...] = jnp.zeros_like(run_ref)

    tm = x_ref.shape[1]
    subs = range(0, tm, XA_SUB)
    head_cols = [slice(h * XA_HEAD_DIM, (h + 1) * XA_HEAD_DIM) for h in range(XA_HEADS)]
    xs = [x_ref[0, r0:r0 + XA_SUB, :] for r0 in subs]
    qs = []
    for x in xs:
        xn = _rms(x, g_ref[...]).astype(BF16)
        q = jnp.dot(xn, wq_ref[...], preferred_element_type=F32) * (XA_HEAD_DIM ** -0.5)
        qs.append(q.astype(BF16))
    scores = [[jnp.dot(q[:, cols], kt_ref[0, cols, :], preferred_element_type=F32)
               for cols in head_cols] for q in qs]
    probs = []
    for per_head in scores:
        row = []
        for s in per_head:
            e = jnp.exp(s - jnp.max(s, axis=-1, keepdims=True))
            row.append((e / jnp.sum(e, axis=-1, keepdims=True)).astype(BF16))
        probs.append(row)
    outs = []
    for x, row in zip(xs, probs):
        heads = [jnp.dot(p, v_ref[0, :, cols], preferred_element_type=F32)
                 for p, cols in zip(row, head_cols)]
        o = jnp.concatenate(heads, axis=1).astype(BF16)
        outs.append(x + jnp.dot(o, wo_ref[...], preferred_element_type=F32))
    for r0, out in zip(subs, outs):
        o_ref[0, r0:r0 + XA_SUB, :] = out
        meta, wts, xn_moe = _route(out, mg_ref, wr_ref, bias_ref, tri_ref, run_ref)
        meta_ref[:, r0:r0 + XA_SUB] = meta
        wts_ref[r0:r0 + XA_SUB, :] = wts
        xp_ref[r0:r0 + XA_SUB, :] = _pack_rows(xn_moe)
    cnt_ref[...] = run_ref[...]


XA_TM = 1024
GATHER_PARTS = 1


def _xattn_router_layer(x, mem, mem_g, norm_g, w_q, w_kv, w_o, moe_g, w_group, b_group,
                        w_expert, b_expert):
    bsz, m, d = mem.shape
    tm = XA_TM
    kv = _norm_proj(mem.reshape(bsz * m, d), mem_g, w_kv, BF16).reshape(bsz, m, 2 * d)
    kt = kv[..., :d].transpose(0, 2, 1)
    v = kv[..., d:]
    nr = MOE_GROUPS + MOE_EXPERTS
    w_r = jnp.zeros((d, LANES), F32).at[:, :MOE_GROUPS].set(w_group.astype(F32))
    w_r = w_r.at[:, MOE_GROUPS:nr].set(w_expert.astype(F32))
    bias = jnp.zeros((1, LANES), F32).at[0, :MOE_GROUPS].set(b_group.astype(F32))
    bias = bias.at[0, MOE_GROUPS:nr].set(b_expert.astype(F32))
    w_hi = w_r.astype(BF16)
    w_lo = (w_r - w_hi.astype(F32)).astype(BF16)
    w_hilo = jnp.concatenate([w_hi, w_lo], axis=1)
    earlier = jnp.tril(jnp.ones((XA_SUB, XA_SUB), BF16), -1)
    nb, seq = x.shape[0], x.shape[1]
    nt = seq // tm
    xspec = pl.BlockSpec((1, tm, d), lambda b, i: (b, i, 0))
    return pl.pallas_call(
        _xattn_router_kernel,
        grid=(nb, nt),
        in_specs=[
            xspec,
            _const_spec((1, d)),
            _const_spec((d, d)),
            pl.BlockSpec((1, d, m), lambda b, i: (b, 0, 0)),
            pl.BlockSpec((1, m, d), lambda b, i: (b, 0, 0)),
            _const_spec((d, d)),
            _const_spec((1, d)),
            _const_spec((d, 2 * LANES)),
            _const_spec((1, LANES)),
            _const_spec((XA_SUB, XA_SUB)),
        ],
        out_specs=[
            xspec,
            pl.BlockSpec((SUBLANES, tm), lambda b, i: (0, b * nt + i)),
            pl.BlockSpec((tm, LANES), lambda b, i: (b * nt + i, 0)),
            pl.BlockSpec((1, LANES), lambda b, i: (0, 0)),
            pl.BlockSpec((tm, d // 2), lambda b, i: (b * nt + i, 0)),
        ],
        out_shape=[
            jax.ShapeDtypeStruct((nb, seq, d), F32),
            jax.ShapeDtypeStruct((SUBLANES, nb * seq), jnp.int32),
            jax.ShapeDtypeStruct((nb * seq, LANES), F32),
            jax.ShapeDtypeStruct((1, LANES), F32),
            jax.ShapeDtypeStruct((nb * seq, d // 2), jnp.int32),
        ],
        scratch_shapes=[pltpu.VMEM((1, LANES), F32)],
        compiler_params=pltpu.CompilerParams(
            dimension_semantics=("arbitrary", "arbitrary"), vmem_limit_bytes=VMEM_LIMIT),
        name="xattn_router",
    )(x, norm_g.reshape(1, d), w_q.astype(BF16), kt, v, w_o.astype(BF16),
      moe_g.reshape(1, d), w_hilo, bias, earlier)


PACKED = D_MODEL // 2
I32 = jnp.int32
HI_HALF = -65536
LO_HALF = 65535


def _pack_rows(v):
    lo = lax.bitcast_convert_type(v[:, :PACKED].astype(BF16).astype(F32), I32)
    hi = lax.bitcast_convert_type(v[:, PACKED:].astype(BF16).astype(F32), I32)
    return (hi & HI_HALF) | ((lo >> 16) & LO_HALF)


def _unpack_rows(p):
    lo = lax.bitcast_convert_type(p << 16, F32)
    hi = lax.bitcast_convert_type(p & HI_HALF, F32)
    return jnp.concatenate([lo, hi], axis=1)


def _expert_kernel(row0_ref, cnt_ref, nfull_ref, rem_ref, ustart_ref, used_ref, xd_hbm, wg_ref,
                   wu_ref, wd_ref, y_hbm, wgb, wub, wdb, xbuf, ybuf, cls_ref, lsem, ssem):
    e = pl.program_id(0)
    used_rows = used_ref[0]
    row0 = row0_ref[e]
    cnt = cnt_ref[e]
    nfull = nfull_ref[e]
    rem = rem_ref[e]
    u0 = ustart_ref[e]
    sizes = (MOE_UNIT,) + MOE_TAILS

    def load(r, slot):
        src = xd_hbm.at[pl.ds(pl.multiple_of(r, MOE_BLK), MOE_UNIT), :]
        return pltpu.make_async_copy(src, xbuf.at[slot], lsem.at[slot])

    def store(r, slot, size):
        dst = y_hbm.at[pl.ds(pl.multiple_of(r, MOE_BLK), size), :]
        return pltpu.make_async_copy(ybuf.at[slot, pl.ds(0, size), :], dst, ssem.at[slot])

    def wait_store(slot):
        for c, size in enumerate(sizes):
            @pl.when(cls_ref[slot] == c + 1)
            def _():
                store(0, slot, size).wait()

    @pl.when(e == 0)
    def _():
        cls_ref[0] = 0
        cls_ref[1] = 0

        @pl.when(used_rows > 0)
        def _():
            load(0, 0).start()

    @pl.when(nfull + rem > 0)
    def _():
        wgb[...] = wg_ref[...].astype(BF16)
        wub[...] = wu_ref[...].astype(BF16)
        wdb[...] = wd_ref[...].astype(BF16)

    def unit(u, r, c):
        size = sizes[c]
        slot = u % 2
        load(r, slot).wait()

        @pl.when(r + size < used_rows)
        def _():
            load(r + size, 1 - slot).start()

        wait_store(slot)
        row = lax.broadcasted_iota(jnp.int32, (size, PACKED), 0)
        xd = jnp.where(row < cnt - (r - row0), xbuf[slot, pl.ds(0, size), :], 0)
        xb = _unpack_rows(xd).astype(BF16)
        a = jnp.dot(xb, wgb[...], preferred_element_type=F32)
        up = jnp.dot(xb, wub[...], preferred_element_type=F32)
        h = (a * (1.0 / (1.0 + jnp.exp(-a))) * up).astype(BF16)
        ybuf[slot, pl.ds(0, size), :] = _pack_rows(
            jnp.dot(h, wdb[...], preferred_element_type=F32))
        store(r, slot, size).start()
        cls_ref[slot] = c + 1

    def full_unit(k, carry):
        unit(u0 + k, row0 + k * MOE_UNIT, 0)
        return carry

    lax.fori_loop(0, nfull, full_unit, 0)

    u = u0 + nfull
    r = row0 + nfull * MOE_UNIT
    for c, size in enumerate(MOE_TAILS, start=1):
        has = (rem & (size // MOE_BLK)) != 0

        @pl.when(has)
        def _():
            unit(u, r, c)
        u = u + has.astype(jnp.int32)
        r = r + has.astype(jnp.int32) * size

    @pl.when(e == pl.num_programs(0) - 1)
    def _():
        wait_store(0)
        wait_store(1)


def _moe_experts(xd, seg_start, counts, layer, w_gate, w_up, w_down):
    n_pad = xd.shape[0]
    d = D_MODEL
    nblk = (counts + MOE_BLK - 1) // MOE_BLK
    per_unit = MOE_UNIT // MOE_BLK
    nfull = nblk // per_unit
    rem = nblk % per_unit
    n_units = nfull + sum(((rem & (size // MOE_BLK)) != 0).astype(jnp.int32)
                          for size in MOE_TAILS)
    ustart = jnp.cumsum(n_units) - n_units
    used_rows = (seg_start[-1:] + nblk[-1:] * MOE_BLK).astype(jnp.int32)

    def expert_blk(e, *_):
        return (layer, e, 0, 0)

    return pl.pallas_call(
        _expert_kernel,
        grid_spec=pltpu.PrefetchScalarGridSpec(
            num_scalar_prefetch=6,
            grid=(MOE_EXPERTS,),
            in_specs=[
                pl.BlockSpec(memory_space=pl.ANY),
                pl.BlockSpec((None, None, d, MOE_HIDDEN), expert_blk),
                pl.BlockSpec((None, None, d, MOE_HIDDEN), expert_blk),
                pl.BlockSpec((None, None, MOE_HIDDEN, d), expert_blk),
            ],
            out_specs=pl.BlockSpec(memory_space=pl.ANY),
            scratch_shapes=[pltpu.VMEM((d, MOE_HIDDEN), BF16), pltpu.VMEM((d, MOE_HIDDEN), BF16),
                            pltpu.VMEM((MOE_HIDDEN, d), BF16),
                            pltpu.VMEM((2, MOE_UNIT, PACKED), I32),
                            pltpu.VMEM((2, MOE_UNIT, PACKED), I32),
                            pltpu.SMEM((2,), jnp.int32),
                            pltpu.SemaphoreType.DMA((2,)), pltpu.SemaphoreType.DMA((2,))],
        ),
        out_shape=jax.ShapeDtypeStruct((n_pad, PACKED), I32),
        compiler_params=pltpu.CompilerParams(
            dimension_semantics=("arbitrary",), vmem_limit_bytes=VMEM_LIMIT),
        name="moe_experts",
    )(seg_start.astype(jnp.int32), counts.astype(jnp.int32), nfull.astype(jnp.int32),
      rem.astype(jnp.int32), ustart.astype(jnp.int32), used_rows, xd, w_gate, w_up, w_down)


def _pos_kernel(ps_ref, meta_ref, pos_ref):
    m = meta_ref[...]
    e = m[0:2, :]
    base = jnp.zeros_like(e)
    for k in range(MOE_EXPERTS):
        base = jnp.where(e == k, ps_ref[k], base)
    pos_ref[...] = base + m[2:4, :]


def _moe_positions(pad_start, meta, tm=2048):
    n_tok = meta.shape[1]
    return pl.pallas_call(
        _pos_kernel,
        grid_spec=pltpu.PrefetchScalarGridSpec(
            num_scalar_prefetch=1,
            grid=(n_tok // tm,),
            in_specs=[pl.BlockSpec((SUBLANES, tm), lambda i, ps: (0, i))],
            out_specs=pl.BlockSpec((2, tm), lambda i, ps: (0, i)),
        ),
        out_shape=jax.ShapeDtypeStruct((2, n_tok), jnp.int32),
        compiler_params=pltpu.CompilerParams(dimension_semantics=("parallel",)),
        name="moe_positions",
    )(pad_start, meta)


SC_CORES = 2
SC_SUBCORES = 16
SC_CHUNK = 32
SC_NBUF = 4


def _sc_gather_rows(table, idx, after):
    n_rows = idx.shape[0]
    width = table.shape[1]
    workers = SC_CORES * SC_SUBCORES
    per_worker = n_rows // workers
    n_chunks = per_worker // SC_CHUNK
    mesh = plsc.VectorSubcoreMesh(core_axis_name="c", subcore_axis_name="s")

    def body(table_hbm, idx_hbm, after_hbm, out_hbm, *scratch):
        idx_v = scratch[:SC_NBUF]
        rows_v = scratch[SC_NBUF:2 * SC_NBUF]
        gsem = scratch[2 * SC_NBUF:3 * SC_NBUF]
        ssem = scratch[3 * SC_NBUF:]
        wid = lax.axis_index("s") * SC_CORES + lax.axis_index("c")
        base = wid * per_worker

        def chunk(j):
            return pl.ds(pl.multiple_of(base + j * SC_CHUNK, SC_CHUNK), SC_CHUNK)

        def gather(b):
            return pltpu.make_async_copy(table_hbm.at[idx_v[b]], rows_v[b], gsem[b])

        def start_gather(j, b):
            pltpu.sync_copy(idx_hbm.at[chunk(j)], idx_v[b])
            gather(b).start()

        def store(j, b):
            return pltpu.make_async_copy(rows_v[b], out_hbm.at[chunk(j)], ssem[b])

        for b in range(SC_NBUF):
            start_gather(b, b)

        @pl.loop(0, n_chunks, step=SC_NBUF)
        def _(g):
            for b in range(SC_NBUF):
                gather(b).wait()
                store(g + b, b).start()
            for b in range(SC_NBUF):
                nxt = g + SC_NBUF + b

                @pl.when(nxt < n_chunks)
                def _():
                    store(g + b, b).wait()
                    start_gather(nxt, b)

        for b in range(SC_NBUF):
            store(0, b).wait()

    assert n_chunks % SC_NBUF == 0 and per_worker % SC_CHUNK == 0
    return pl.kernel(
        body,
        out_type=jax.ShapeDtypeStruct((n_rows, width), table.dtype),
        mesh=mesh,
        scratch_types=([pltpu.VMEM((SC_CHUNK,), jnp.int32)] * SC_NBUF
                       + [pltpu.VMEM((SC_CHUNK, width), table.dtype)] * SC_NBUF
                       + [pltpu.SemaphoreType.DMA] * (2 * SC_NBUF)),
        name="sc_gather_rows",
    )(table, idx, after)


def _sc_scatter_rows(rows, idx, n_out):
    n_rows, width = rows.shape
    workers = SC_CORES * SC_SUBCORES
    per_worker = n_rows // workers
    n_chunks = per_worker // SC_CHUNK
    mesh = plsc.VectorSubcoreMesh(core_axis_name="c", subcore_axis_name="s")

    def body(rows_hbm, idx_hbm, out_hbm, *scratch):
        idx_v = scratch[:2 * SC_NBUF]
        rows_v = scratch[2 * SC_NBUF:3 * SC_NBUF]
        lsem = scratch[3 * SC_NBUF:4 * SC_NBUF]
        ssem = scratch[4 * SC_NBUF:]
        wid = lax.axis_index("s") * SC_CORES + lax.axis_index("c")
        base = wid * per_worker

        def chunk(j, k=0):
            return pl.ds(pl.multiple_of(k * n_rows + base + j * SC_CHUNK, SC_CHUNK), SC_CHUNK)

        def load(j, b):
            return pltpu.make_async_copy(rows_hbm.at[chunk(j)], rows_v[b], lsem[b])

        def scatter(b, k):
            return pltpu.make_async_copy(rows_v[b], out_hbm.at[idx_v[2 * b + k]], ssem[b])

        for b in range(SC_NBUF):
            load(b, b).start()

        @pl.loop(0, n_chunks, step=SC_NBUF)
        def _(g):
            for b in range(SC_NBUF):
                for k in range(2):
                    pltpu.sync_copy(idx_hbm.at[chunk(g + b, k)], idx_v[2 * b + k])
                load(g + b, b).wait()
                for k in range(2):
                    scatter(b, k).start()
            for b in range(SC_NBUF):
                nxt = g + SC_NBUF + b

                @pl.when(nxt < n_chunks)
                def _():
                    for k in range(2):
                        scatter(b, k).wait()
                    load(nxt, b).start()

        for b in range(SC_NBUF):
            for k in range(2):
                scatter(b, k).wait()

    assert n_chunks % SC_NBUF == 0 and per_worker % SC_CHUNK == 0
    return pl.kernel(
        body,
        out_type=jax.ShapeDtypeStruct((n_out, width), rows.dtype),
        mesh=mesh,
        scratch_types=([pltpu.VMEM((SC_CHUNK,), jnp.int32)] * (2 * SC_NBUF)
                       + [pltpu.VMEM((SC_CHUNK, width), rows.dtype)] * SC_NBUF
                       + [pltpu.SemaphoreType.DMA] * (2 * SC_NBUF)),
        name="sc_scatter_rows",
    )(rows, idx)


def _combine_kernel(x_ref, wts_ref, g1_ref, g2_ref, fg_ref, *rest, final):
    o_ref = rest[-1]
    w = wts_ref[...]
    out = (x_ref[...] + w[:, 0:1] * _unpack_rows(g1_ref[...])
           + w[:, 1:2] * _unpack_rows(g2_ref[...]))
    if final:
        out = _rms(out, fg_ref[...])
    o_ref[...] = out


def _moe_layer(meta, cnt, xp, layer, w_gate, w_up, w_down):
    n_tok = xp.shape[0]
    counts = cnt[0, :MOE_EXPERTS].astype(jnp.int32)
    padded = ((counts + MOE_BLK - 1) // MOE_BLK) * MOE_BLK
    pad_start = jnp.cumsum(padded) - padded
    n_pad = n_tok * 2 + MOE_EXPERTS * MOE_BLK + MOE_UNIT

    pos = _moe_positions(pad_start, meta)
    xd = _sc_scatter_rows(xp, pos.reshape(2 * n_tok), n_pad)
    y = _moe_experts(xd, pad_start, counts, layer, w_gate, w_up, w_down)

    per_part = n_tok // GATHER_PARTS
    gathered = []
    for h in range(GATHER_PARTS):
        idx = pos[:, h * per_part:(h + 1) * per_part].reshape(2 * per_part)
        gathered.append(_sc_gather_rows(y, idx, gathered[-1] if gathered else idx))
    return gathered


def _moe_combine(x, moe_out, final_g=None):
    d = x.shape[-1]
    final = final_g is not None
    fg = (final_g if final else jnp.ones((d,), F32)).reshape(1, d)
    return _per_part(functools.partial(_combine_kernel, final=final), x, moe_out, COMBINE_TM,
                     [pl.BlockSpec((1, d), lambda i: (0, 0))], [fg], "moe_combine")


def kernel(x, mem, s5_norm, s5_w_in, s5_lambda_re, s5_lambda_im, s5_log_dt, s5_b_re, s5_b_im, s5_c_re, s5_c_im, s5_d, s5_w_out, gm_norm, gm_w_in, gm_v_norm, gm_w_s, gm_b_s, gm_w_out, mem_norm, xa_norm, xa_w_q, xa_w_kv, xa_w_o, moe_norm, moe_w_group, moe_b_group, moe_w_expert, moe_b_expert, moe_w_gate, moe_w_up, moe_w_down, final_norm):
    moe_out = None
    for i in range(DEPTH):
        j = i // 2
        if i % 2 == 0:
            if moe_out is not None:
                x = _moe_combine(x, moe_out)
            x = _s5_layer(x, s5_norm[j], s5_w_in[j], s5_lambda_re[j], s5_lambda_im[j],
                          s5_log_dt[j], s5_b_re[j], s5_b_im[j], s5_c_re[j], s5_c_im[j],
                          s5_d[j], s5_w_out[j])
        else:
            x = _gmlp_layer(x, moe_out, gm_norm[j], gm_w_in[j], gm_v_norm[j], gm_w_s[j],
                            gm_b_s[j], gm_w_out[j])
        x, meta, wts, cnt, xp = _xattn_router_layer(
            x, mem, mem_norm, xa_norm[i], xa_w_q[i], xa_w_kv[i], xa_w_o[i], moe_norm[i],
            moe_w_group[i], moe_b_group[i], moe_w_expert[i], moe_b_expert[i])
        moe_out = (wts, _moe_layer(meta, cnt, xp, i, moe_w_gate, moe_w_up, moe_w_down))
    return _moe_combine(x, moe_out, final_g=final_norm)
```

```python
import functools

import jax
import jax.numpy as jnp
from jax import lax
from jax.experimental import pallas as pl
from jax.experimental.pallas import tpu as pltpu
from jax.experimental.pallas import tpu_sc as plsc

F32 = jnp.float32
BF16 = jnp.bfloat16

D_MODEL = 1024
DEPTH = 2
CHUNK = 64
S5_GROUP_CH = 16
S5_GROUPS = 64
S5_STATE = 64
GM_HALF = 2 * D_MODEL
GM_GROUPS = 8
GM_SPAN = 128
GM_GROUP_CH = GM_HALF // GM_GROUPS
XA_HEADS = 4
XA_HEAD_DIM = D_MODEL // XA_HEADS
XA_SUB = 512
MOE_GROUPS = 4
MOE_PER_GROUP = 8
MOE_EXPERTS = MOE_GROUPS * MOE_PER_GROUP
MOE_HIDDEN = D_MODEL // 2
RMS_EPS = 1e-6

LANES = 128
SUBLANES = 8
VMEM_LIMIT = 56 * 1024 * 1024

S5_CT = 256
S5_NCT = D_MODEL // S5_CT
S5_STATES_CT = (S5_CT // S5_GROUP_CH) * S5_STATE
S5_SLABS = S5_STATES_CT // LANES
S5_TC = 128
S5_PITCH = S5_TC + 4
S5_SCAN_CT = 2

MOE_BLK = 256
MOE_UNIT = 1024
MOE_TAILS = (512, 256)
COMBINE_TM = 512


def _rms(x, g):
    ms = jnp.mean(x * x, axis=-1, keepdims=True)
    return x * lax.rsqrt(ms + RMS_EPS) * g


def _const_spec(shape):
    nd = len(shape)
    return pl.BlockSpec(shape, lambda *_: (0,) * nd, pipeline_mode=pl.Buffered(1))


def _s5_kernel(x_ref, g_ref, win_ref, bblk_ref, cblk_ref, are_ref, aim_ref, d_ref,
               wout_ref, o_ref, hre_ref, him_ref, bre_ref, bim_ref, sre_ref, sim_ref):
    nb = x_ref.shape[0]
    tc = x_ref.shape[1]

    @pl.when(pl.program_id(0) == 0)
    def _():
        hre_ref[...] = jnp.zeros_like(hre_ref)
        him_ref[...] = jnp.zeros_like(him_ref)

    x = x_ref[...].reshape(nb * tc, D_MODEL)
    xn = _rms(x, g_ref[...]).astype(BF16)
    u = jnp.dot(xn, win_ref[...], preferred_element_type=F32)

    y_parts = []
    for ct0 in range(0, S5_NCT, S5_SCAN_CT):
        tiles = range(ct0, ct0 + S5_SCAN_CT)
        for c, ct in enumerate(tiles):
            u_ct = u[:, ct * S5_CT:(ct + 1) * S5_CT]
            bu = jnp.dot(u_ct.astype(BF16), bblk_ref[ct], preferred_element_type=F32)
            for b in range(nb):
                for j in range(S5_SLABS):
                    rows = slice(b * tc, (b + 1) * tc)
                    bre_ref[c, b, pl.ds(j * S5_PITCH, tc), :] = bu[rows, j * LANES:(j + 1) * LANES]
                    bim_ref[c, b, pl.ds(j * S5_PITCH, tc), :] = bu[
                        rows, S5_STATES_CT + j * LANES:S5_STATES_CT + (j + 1) * LANES]
        chains = [(c, ct, b) for c, ct in enumerate(tiles) for b in range(nb)]
        a_re = [are_ref[ct] for ct in tiles]
        a_im = [aim_ref[ct] for ct in tiles]

        def step(t, carry):
            idx = pl.ds(t, S5_SLABS, stride=S5_PITCH)
            bu_t = [(bre_ref[c, b, idx, :], bim_ref[c, b, idx, :]) for c, _, b in chains]
            new = []
            for k, (c, _, b) in enumerate(chains):
                hr, hi = carry[2 * k], carry[2 * k + 1]
                new.append(a_re[c] * hr - a_im[c] * hi + bu_t[k][0])
                new.append(a_re[c] * hi + a_im[c] * hr + bu_t[k][1])
            for k, (c, _, b) in enumerate(chains):
                sre_ref[c, b, idx, :] = new[2 * k]
                sim_ref[c, b, idx, :] = new[2 * k + 1]
            return tuple(new)

        init = []
        for _, ct, b in chains:
            init += [hre_ref[ct, b], him_ref[ct, b]]
        fin = lax.fori_loop(0, tc, step, tuple(init), unroll=8)
        for k, (_, ct, b) in enumerate(chains):
            hre_ref[ct, b] = fin[2 * k]
            him_ref[ct, b] = fin[2 * k + 1]

        for c, ct in enumerate(tiles):
            xs = []
            for b in range(nb):
                cols = [sre_ref[c, b, pl.ds(j * S5_PITCH, tc), :] for j in range(S5_SLABS)]
                cols += [sim_ref[c, b, pl.ds(j * S5_PITCH, tc), :] for j in range(S5_SLABS)]
                xs.append(jnp.concatenate(cols, axis=1))
            xst = jnp.concatenate(xs, axis=0).astype(BF16)
            cols = slice(ct * S5_CT, (ct + 1) * S5_CT)
            y = jnp.dot(xst, cblk_ref[ct], preferred_element_type=F32) + d_ref[:, cols] * u[:, cols]
            y_parts.append(jax.nn.gelu(y).astype(BF16))

    yg = jnp.concatenate(y_parts, axis=1)
    gate = jnp.dot(yg, wout_ref[:, D_MODEL:], preferred_element_type=F32)
    sig = 1.0 / (1.0 + jnp.exp(-gate))
    val = jnp.dot(yg, wout_ref[:, :D_MODEL], preferred_element_type=F32)
    o_ref[...] = (x + val * sig).reshape(nb, tc, D_MODEL)


def _s5_discretize(lam_re, lam_im, log_dt, b_re, b_im, c_re, c_im):
    lr = lam_re.astype(F32)
    li = lam_im.astype(F32)
    dt = jnp.exp(log_dt.astype(F32))[:, None]
    mag = jnp.exp(lr * dt)
    ab_re = mag * jnp.cos(li * dt)
    ab_im = mag * jnp.sin(li * dt)
    den = lr * lr + li * li
    coef_re = ((ab_re - 1.0) * lr + ab_im * li) / den
    coef_im = (ab_im * lr - (ab_re - 1.0) * li) / den
    br = b_re.astype(F32)
    bi = b_im.astype(F32)
    bb_re = coef_re[..., None] * br - coef_im[..., None] * bi
    bb_im = coef_re[..., None] * bi + coef_im[..., None] * br
    gpt = S5_CT // S5_GROUP_CH
    ch_group = jnp.arange(S5_CT, dtype=jnp.int32) // S5_GROUP_CH
    st_group = jnp.arange(S5_STATES_CT, dtype=jnp.int32) // S5_STATE
    same = ch_group[:, None] == st_group[None, :]

    def in_blocks(bb):
        t = bb.reshape(S5_NCT, gpt, S5_STATE, S5_GROUP_CH).transpose(0, 3, 1, 2)
        t = t.reshape(S5_NCT, 1, S5_GROUP_CH, S5_STATES_CT)
        t = jnp.broadcast_to(t, (S5_NCT, gpt, S5_GROUP_CH, S5_STATES_CT))
        return jnp.where(same[None], t.reshape(S5_NCT, S5_CT, S5_STATES_CT), 0.0)

    def out_blocks(c):
        return in_blocks(c.transpose(0, 2, 1)).transpose(0, 2, 1)

    bblk = jnp.concatenate([in_blocks(bb_re), in_blocks(bb_im)], axis=2).astype(BF16)
    cblk = jnp.concatenate([out_blocks(c_re.astype(F32)),
                            out_blocks(-c_im.astype(F32))], axis=1).astype(BF16)
    a_re = ab_re.reshape(S5_NCT, S5_SLABS, LANES)
    a_im = ab_im.reshape(S5_NCT, S5_SLABS, LANES)
    return bblk, cblk, a_re, a_im


def _s5_layer(x, norm_g, w_in, lam_re, lam_im, log_dt, b_re, b_im, c_re, c_im, d_skip, w_out):
    bsz, seq, d = x.shape
    bblk, cblk, a_re, a_im = _s5_discretize(lam_re, lam_im, log_dt, b_re, b_im, c_re, c_im)
    xspec = pl.BlockSpec((bsz, S5_TC, d), lambda k: (0, k, 0))
    return pl.pallas_call(
        _s5_kernel,
        grid=(seq // S5_TC,),
        in_specs=[
            xspec,
            _const_spec((1, d)),
            _const_spec((d, d)),
            _const_spec(bblk.shape),
            _const_spec(cblk.shape),
            _const_spec(a_re.shape),
            _const_spec(a_im.shape),
            _const_spec((1, d)),
            _const_spec((d, 2 * d)),
        ],
        out_specs=xspec,
        out_shape=jax.ShapeDtypeStruct(x.shape, F32),
        scratch_shapes=[
            pltpu.VMEM((S5_NCT, bsz, S5_SLABS, LANES), F32),
            pltpu.VMEM((S5_NCT, bsz, S5_SLABS, LANES), F32),
            pltpu.VMEM((S5_SCAN_CT, bsz, S5_SLABS * S5_PITCH, LANES), F32),
            pltpu.VMEM((S5_SCAN_CT, bsz, S5_SLABS * S5_PITCH, LANES), F32),
            pltpu.VMEM((S5_SCAN_CT, bsz, S5_SLABS * S5_PITCH, LANES), F32),
            pltpu.VMEM((S5_SCAN_CT, bsz, S5_SLABS * S5_PITCH, LANES), F32),
        ],
        compiler_params=pltpu.CompilerParams(
            dimension_semantics=("arbitrary",), vmem_limit_bytes=VMEM_LIMIT),
        name="s5_layer",
    )(x, norm_g.reshape(1, d), w_in.astype(BF16), bblk, cblk, a_re, a_im,
      d_skip.reshape(1, d).astype(F32), w_out.astype(BF16))


def _gmlp_kernel(x_ref, wts_ref, g1_ref, g2_ref, g_ref, win_ref, vn_ref, ws_ref, bst_ref,
                 wout_ref, *rest):
    o_ref = rest[-1]
    tm = x_ref.shape[0]
    w = wts_ref[...]
    x = (x_ref[...] + w[:, 0:1] * _unpack_rows(g1_ref[...])
         + w[:, 1:2] * _unpack_rows(g2_ref[...]))
    xn = _rms(x, g_ref[...]).astype(BF16)
    v = jax.nn.gelu(jnp.dot(xn, win_ref[:, GM_HALF:], preferred_element_type=F32))
    vb = _rms(v, vn_ref[...]).astype(BF16)
    u = jax.nn.gelu(jnp.dot(xn, win_ref[:, :GM_HALF], preferred_element_type=F32))
    row = lax.broadcasted_iota(jnp.int32, (GM_SPAN, GM_SPAN), 0) // CHUNK
    col = lax.broadcasted_iota(jnp.int32, (GM_SPAN, GM_SPAN), 1) // CHUNK
    causal = row >= col
    ws = [jnp.where(causal, ws_ref[g], 0.0).astype(BF16) for g in range(GM_GROUPS)]
    spans = []
    for s in range(tm // GM_SPAN):
        parts = []
        for g in range(GM_GROUPS):
            vblk = vb[s * GM_SPAN:(s + 1) * GM_SPAN, g * GM_GROUP_CH:(g + 1) * GM_GROUP_CH]
            parts.append(jnp.dot(ws[g], vblk, preferred_element_type=F32)
                         + bst_ref[:, g:g + 1])
        spans.append(jnp.concatenate(parts, axis=1))
    mixed = jnp.concatenate(spans, axis=0)
    p = (u * mixed).astype(BF16)
    o_ref[...] = x + jnp.dot(p, wout_ref[...], preferred_element_type=F32)


def _per_part(call, x, moe_out, tm, extra_specs, extra_args, name):
    bsz, seq, d = x.shape
    n_tok = bsz * seq
    wts, gathered = moe_out
    n_tiles = n_tok // tm // len(gathered)
    out = None
    for h, rows in enumerate(gathered):
        def tile(i, h=h):
            return (i + h * n_tiles, 0)
        in_specs = [
            pl.BlockSpec((tm, d), tile),
            pl.BlockSpec((tm, LANES), tile),
            pl.BlockSpec((tm, PACKED), lambda i: (i, 0)),
            pl.BlockSpec((tm, PACKED), lambda i: (i + n_tiles, 0)),
        ] + extra_specs
        args = [x.reshape(n_tok, d), wts, rows, rows] + extra_args
        aliases = {}
        if out is not None:
            aliases = {len(args): 0}
            in_specs.append(pl.BlockSpec(memory_space=pl.ANY))
            args.append(out)
        out = pl.pallas_call(
            call,
            grid=(n_tiles,),
            in_specs=in_specs,
            out_specs=pl.BlockSpec((tm, d), tile),
            out_shape=jax.ShapeDtypeStruct((n_tok, d), F32),
            input_output_aliases=aliases,
            compiler_params=pltpu.CompilerParams(
                dimension_semantics=("parallel",), vmem_limit_bytes=VMEM_LIMIT),
            name=name,
        )(*args)
    return out.reshape(bsz, seq, d)


def _gmlp_layer(x, moe_out, norm_g, w_in, v_norm, w_s, b_s, w_out, tm=512):
    d = x.shape[-1]
    specs = [
        _const_spec((1, d)),
        _const_spec((d, 2 * GM_HALF)),
        _const_spec((1, GM_HALF)),
        _const_spec((GM_GROUPS, GM_SPAN, GM_SPAN)),
        _const_spec((GM_SPAN, GM_GROUPS)),
        _const_spec((GM_HALF, d)),
    ]
    args = [norm_g.reshape(1, d), w_in.astype(BF16), v_norm.reshape(1, GM_HALF),
            w_s.astype(F32), b_s.T.astype(F32), w_out.astype(BF16)]
    return _per_part(_gmlp_kernel, x, moe_out, tm, specs, args, "gmlp_layer")


def _norm_proj_kernel(x_ref, g_ref, w_ref, o_ref):
    xn = _rms(x_ref[...], g_ref[...]).astype(BF16)
    o_ref[...] = jnp.dot(xn, w_ref[...], preferred_element_type=F32).astype(o_ref.dtype)


def _norm_proj(x, g, w, out_dtype, tm=512, tn=1024):
    m, d = x.shape
    n = w.shape[1]
    return pl.pallas_call(
        _norm_proj_kernel,
        grid=(m // tm, n // tn),
        in_specs=[
            pl.BlockSpec((tm, d), lambda i, j: (i, 0)),
            pl.BlockSpec((1, d), lambda i, j: (0, 0)),
            pl.BlockSpec((d, tn), lambda i, j: (0, j)),
        ],
        out_specs=pl.BlockSpec((tm, tn), lambda i, j: (i, j)),
        out_shape=jax.ShapeDtypeStruct((m, n), out_dtype),
        compiler_params=pltpu.CompilerParams(
            dimension_semantics=("parallel", "parallel"), vmem_limit_bytes=VMEM_LIMIT),
        name="norm_proj",
    )(x, g.reshape(1, d), w.astype(BF16))


def _route(x, g_ref, wr_ref, bias_ref, tri_ref, run_ref):
    tm = x.shape[0]
    xn = _rms(x, g_ref[...])
    xhi = xn.astype(BF16)
    xlo = (xn - xhi.astype(F32)).astype(BF16)
    hi = jnp.dot(xhi, wr_ref[...], preferred_element_type=F32)
    lo = jnp.dot(xlo, wr_ref[:, :LANES], preferred_element_type=F32)
    logits = hi[:, :LANES] + hi[:, LANES:] + lo + bias_ref[...]
    lane = lax.broadcasted_iota(jnp.int32, (tm, LANES), 1).astype(F32)
    neg = jnp.float32(-jnp.inf)

    def first_argmax(vals):
        mx = jnp.max(vals, axis=-1, keepdims=True)
        idx = jnp.min(jnp.where(vals == mx, lane, float(LANES)), axis=-1, keepdims=True)
        return mx, idx

    gl = jnp.where(lane < MOE_GROUPS, logits, neg)
    gmax, gidx = first_argmax(gl)
    w_g = 1.0 / jnp.sum(jnp.exp(gl - gmax), axis=-1, keepdims=True)
    first = MOE_GROUPS + MOE_PER_GROUP * gidx
    el = jnp.where((lane >= first) & (lane < first + MOE_PER_GROUP), logits, neg)
    m1, i1 = first_argmax(el)
    m2, i2 = first_argmax(jnp.where(lane == i1, neg, el))
    e21 = jnp.exp(m2 - m1)
    w1 = w_g / (1.0 + e21)
    w2 = w_g * e21 / (1.0 + e21)
    e1 = i1 - MOE_GROUPS
    e2 = i2 - MOE_GROUPS

    onehot = ((lane == e1) | (lane == e2)).astype(BF16)
    tot = run_ref[...] + jnp.dot(tri_ref[...], onehot, preferred_element_type=F32)
    r1 = jnp.sum(jnp.where(lane == e1, tot, 0.0), axis=-1, keepdims=True)
    r2 = jnp.sum(jnp.where(lane == e2, tot, 0.0), axis=-1, keepdims=True)
    run_ref[...] = run_ref[...] + jnp.sum(onehot.astype(F32), axis=0, keepdims=True)

    meta = jnp.where(lane == 0, e1, jnp.where(lane == 1, e2,
                     jnp.where(lane == 2, r1, jnp.where(lane == 3, r2, 0.0))))
    meta_t = jnp.transpose(meta)[:SUBLANES, :].astype(jnp.int32)
    wts = jnp.where(lane == 0, w1, jnp.where(lane == 1, w2, 0.0))
    return meta_t, wts, xn


def _xattn_router_kernel(x_ref, g_ref, wq_ref, kt_ref, v_ref, wo_ref, mg_ref, wr_ref,
                         bias_ref, tri_ref, o_ref, meta_ref, wts_ref, cnt_ref, xp_ref, run_ref):
    @pl.when((pl.program_id(0) == 0) & (pl.program_id(1) == 0))
    def _():
        run_ref[...] = jnp.zeros_like(run_ref)

    tm = x_ref.shape[1]
    subs = range(0, tm, XA_SUB)
    head_cols = [slice(h * XA_HEAD_DIM, (h + 1) * XA_HEAD_DIM) for h in range(XA_HEADS)]
    xs = [x_ref[0, r0:r0 + XA_SUB, :] for r0 in subs]
    qs = []
    for x in xs:
        xn = _rms(x, g_ref[...]).astype(BF16)
        q = jnp.dot(xn, wq_ref[...], preferred_element_type=F32) * (XA_HEAD_DIM ** -0.5)
        qs.append(q.astype(BF16))
    scores = [[jnp.dot(q[:, cols], kt_ref[0, cols, :], preferred_element_type=F32)
               for cols in head_cols] for q in qs]
    probs = []
    for per_head in scores:
        row = []
        for s in per_head:
            e = jnp.exp(s - jnp.max(s, axis=-1, keepdims=True))
            row.append((e / jnp.sum(e, axis=-1, keepdims=True)).astype(BF16))
        probs.append(row)
    outs = []
    for x, row in zip(xs, probs):
        heads = [jnp.dot(p, v_ref[0, :, cols], preferred_element_type=F32)
                 for p, cols in zip(row, head_cols)]
        o = jnp.concatenate(heads, axis=1).astype(BF16)
        outs.append(x + jnp.dot(o, wo_ref[...], preferred_element_type=F32))
    for r0, out in zip(subs, outs):
        o_ref[0, r0:r0 + XA_SUB, :] = out
        meta, wts, xn_moe = _route(out, mg_ref, wr_ref, bias_ref, tri_ref, run_ref)
        meta_ref[:, r0:r0 + XA_SUB] = meta
        wts_ref[r0:r0 + XA_SUB, :] = wts
        xp_ref[r0:r0 + XA_SUB, :] = _pack_rows(xn_moe)
    cnt_ref[...] = run_ref[...]


XA_TM = 1024
GATHER_PARTS = 1


def _xattn_router_layer(x, mem, mem_g, norm_g, w_q, w_kv, w_o, moe_g, w_group, b_group,
                        w_expert, b_expert):
    bsz, m, d = mem.shape
    tm = XA_TM
    kv = _norm_proj(mem.reshape(bsz * m, d), mem_g, w_kv, BF16).reshape(bsz, m, 2 * d)
    kt = kv[..., :d].transpose(0, 2, 1)
    v = kv[..., d:]
    nr = MOE_GROUPS + MOE_EXPERTS
    w_r = jnp.zeros((d, LANES), F32).at[:, :MOE_GROUPS].set(w_group.astype(F32))
    w_r = w_r.at[:, MOE_GROUPS:nr].set(w_expert.astype(F32))
    bias = jnp.zeros((1, LANES), F32).at[0, :MOE_GROUPS].set(b_group.astype(F32))
    bias = bias.at[0, MOE_GROUPS:nr].set(b_expert.astype(F32))
    w_hi = w_r.astype(BF16)
    w_lo = (w_r - w_hi.astype(F32)).astype(BF16)
    w_hilo = jnp.concatenate([w_hi, w_lo], axis=1)
    earlier = jnp.tril(jnp.ones((XA_SUB, XA_SUB), BF16), -1)
    nb, seq = x.shape[0], x.shape[1]
    nt = seq // tm
    xspec = pl.BlockSpec((1, tm, d), lambda b, i: (b, i, 0))
    return pl.pallas_call(
        _xattn_router_kernel,
        grid=(nb, nt),
        in_specs=[
            xspec,
            _const_spec((1, d)),
            _const_spec((d, d)),
            pl.BlockSpec((1, d, m), lambda b, i: (b, 0, 0)),
            pl.BlockSpec((1, m, d), lambda b, i: (b, 0, 0)),
            _const_spec((d, d)),
            _const_spec((1, d)),
            _const_spec((d, 2 * LANES)),
            _const_spec((1, LANES)),
            _const_spec((XA_SUB, XA_SUB)),
        ],
        out_specs=[
            xspec,
            pl.BlockSpec((SUBLANES, tm), lambda b, i: (0, b * nt + i)),
            pl.BlockSpec((tm, LANES), lambda b, i: (b * nt + i, 0)),
            pl.BlockSpec((1, LANES), lambda b, i: (0, 0)),
            pl.BlockSpec((tm, d // 2), lambda b, i: (b * nt + i, 0)),
        ],
        out_shape=[
            jax.ShapeDtypeStruct((nb, seq, d), F32),
            jax.ShapeDtypeStruct((SUBLANES, nb * seq), jnp.int32),
            jax.ShapeDtypeStruct((nb * seq, LANES), F32),
            jax.ShapeDtypeStruct((1, LANES), F32),
            jax.ShapeDtypeStruct((nb * seq, d // 2), jnp.int32),
        ],
        scratch_shapes=[pltpu.VMEM((1, LANES), F32)],
        compiler_params=pltpu.CompilerParams(
            dimension_semantics=("arbitrary", "arbitrary"), vmem_limit_bytes=VMEM_LIMIT),
        name="xattn_router",
    )(x, norm_g.reshape(1, d), w_q.astype(BF16), kt, v, w_o.astype(BF16),
      moe_g.reshape(1, d), w_hilo, bias, earlier)


PACKED = D_MODEL // 2
I32 = jnp.int32
HI_HALF = -65536
LO_HALF = 65535


def _pack_rows(v):
    lo = lax.bitcast_convert_type(v[:, :PACKED].astype(BF16).astype(F32), I32)
    hi = lax.bitcast_convert_type(v[:, PACKED:].astype(BF16).astype(F32), I32)
    return (hi & HI_HALF) | ((lo >> 16) & LO_HALF)


def _unpack_rows(p):
    lo = lax.bitcast_convert_type(p << 16, F32)
    hi = lax.bitcast_convert_type(p & HI_HALF, F32)
    return jnp.concatenate([lo, hi], axis=1)


def _expert_kernel(row0_ref, cnt_ref, nfull_ref, rem_ref, ustart_ref, used_ref, xd_hbm, wg_ref,
                   wu_ref, wd_ref, y_hbm, wgb, wub, wdb, xbuf, ybuf, cls_ref, lsem, ssem):
    e = pl.program_id(0)
    used_rows = used_ref[0]
    row0 = row0_ref[e]
    cnt = cnt_ref[e]
    nfull = nfull_ref[e]
    rem = rem_ref[e]
    u0 = ustart_ref[e]
    sizes = (MOE_UNIT,) + MOE_TAILS

    def load(r, slot):
        src = xd_hbm.at[pl.ds(pl.multiple_of(r, MOE_BLK), MOE_UNIT), :]
        return pltpu.make_async_copy(src, xbuf.at[slot], lsem.at[slot])

    def store(r, slot, size):
        dst = y_hbm.at[pl.ds(pl.multiple_of(r, MOE_BLK), size), :]
        return pltpu.make_async_copy(ybuf.at[slot, pl.ds(0, size), :], dst, ssem.at[slot])

    def wait_store(slot):
        for c, size in enumerate(sizes):
            @pl.when(cls_ref[slot] == c + 1)
            def _():
                store(0, slot, size).wait()

    @pl.when(e == 0)
    def _():
        cls_ref[0] = 0
        cls_ref[1] = 0

        @pl.when(used_rows > 0)
        def _():
            load(0, 0).start()

    @pl.when(nfull + rem > 0)
    def _():
        wgb[...] = wg_ref[...].astype(BF16)
        wub[...] = wu_ref[...].astype(BF16)
        wdb[...] = wd_ref[...].astype(BF16)

    def unit(u, r, c):
        size = sizes[c]
        slot = u % 2
        load(r, slot).wait()

        @pl.when(r + size < used_rows)
        def _():
            load(r + size, 1 - slot).start()

        wait_store(slot)
        row = lax.broadcasted_iota(jnp.int32, (size, PACKED), 0)
        xd = jnp.where(row < cnt - (r - row0), xbuf[slot, pl.ds(0, size), :], 0)
        xb = _unpack_rows(xd).astype(BF16)
        a = jnp.dot(xb, wgb[...], preferred_element_type=F32)
        up = jnp.dot(xb, wub[...], preferred_element_type=F32)
        h = (a * (1.0 / (1.0 + jnp.exp(-a))) * up).astype(BF16)
        ybuf[slot, pl.ds(0, size), :] = _pack_rows(
            jnp.dot(h, wdb[...], preferred_element_type=F32))
        store(r, slot, size).start()
        cls_ref[slot] = c + 1

    def full_unit(k, carry):
        unit(u0 + k, row0 + k * MOE_UNIT, 0)
        return carry

    lax.fori_loop(0, nfull, full_unit, 0)

    u = u0 + nfull
    r = row0 + nfull * MOE_UNIT
    for c, size in enumerate(MOE_TAILS, start=1):
        has = (rem & (size // MOE_BLK)) != 0

        @pl.when(has)
        def _():
            unit(u, r, c)
        u = u + has.astype(jnp.int32)
        r = r + has.astype(jnp.int32) * size

    @pl.when(e == pl.num_programs(0) - 1)
    def _():
        wait_store(0)
        wait_store(1)


def _moe_experts(xd, seg_start, counts, layer, w_gate, w_up, w_down):
    n_pad = xd.shape[0]
    d = D_MODEL
    nblk = (counts + MOE_BLK - 1) // MOE_BLK
    per_unit = MOE_UNIT // MOE_BLK
    nfull = nblk // per_unit
    rem = nblk % per_unit
    n_units = nfull + sum(((rem & (size // MOE_BLK)) != 0).astype(jnp.int32)
                          for size in MOE_TAILS)
    ustart = jnp.cumsum(n_units) - n_units
    used_rows = (seg_start[-1:] + nblk[-1:] * MOE_BLK).astype(jnp.int32)

    def expert_blk(e, *_):
        return (layer, e, 0, 0)

    return pl.pallas_call(
        _expert_kernel,
        grid_spec=pltpu.PrefetchScalarGridSpec(
            num_scalar_prefetch=6,
            grid=(MOE_EXPERTS,),
            in_specs=[
                pl.BlockSpec(memory_space=pl.ANY),
                pl.BlockSpec((None, None, d, MOE_HIDDEN), expert_blk),
                pl.BlockSpec((None, None, d, MOE_HIDDEN), expert_blk),
                pl.BlockSpec((None, None, MOE_HIDDEN, d), expert_blk),
            ],
            out_specs=pl.BlockSpec(memory_space=pl.ANY),
            scratch_shapes=[pltpu.VMEM((d, MOE_HIDDEN), BF16), pltpu.VMEM((d, MOE_HIDDEN), BF16),
                            pltpu.VMEM((MOE_HIDDEN, d), BF16),
                            pltpu.VMEM((2, MOE_UNIT, PACKED), I32),
                            pltpu.VMEM((2, MOE_UNIT, PACKED), I32),
                            pltpu.SMEM((2,), jnp.int32),
                            pltpu.SemaphoreType.DMA((2,)), pltpu.SemaphoreType.DMA((2,))],
        ),
        out_shape=jax.ShapeDtypeStruct((n_pad, PACKED), I32),
        compiler_params=pltpu.CompilerParams(
            dimension_semantics=("arbitrary",), vmem_limit_bytes=VMEM_LIMIT),
        name="moe_experts",
    )(seg_start.astype(jnp.int32), counts.astype(jnp.int32), nfull.astype(jnp.int32),
      rem.astype(jnp.int32), ustart.astype(jnp.int32), used_rows, xd, w_gate, w_up, w_down)


def _pos_kernel(ps_ref, meta_ref, pos_ref):
    m = meta_ref[...]
    e = m[0:2, :]
    base = jnp.zeros_like(e)
    for k in range(MOE_EXPERTS):
        base = jnp.where(e == k, ps_ref[k], base)
    pos_ref[...] = base + m[2:4, :]


def _moe_positions(pad_start, meta, tm=2048):
    n_tok = meta.shape[1]
    return pl.pallas_call(
        _pos_kernel,
        grid_spec=pltpu.PrefetchScalarGridSpec(
            num_scalar_prefetch=1,
            grid=(n_tok // tm,),
            in_specs=[pl.BlockSpec((SUBLANES, tm), lambda i, ps: (0, i))],
            out_specs=pl.BlockSpec((2, tm), lambda i, ps: (0, i)),
        ),
        out_shape=jax.ShapeDtypeStruct((2, n_tok), jnp.int32),
        compiler_params=pltpu.CompilerParams(dimension_semantics=("parallel",)),
        name="moe_positions",
    )(pad_start, meta)


SC_CORES = 2
SC_SUBCORES = 16
SC_CHUNK = 32
SC_NBUF = 4


def _sc_gather_rows(table, idx, after):
    n_rows = idx.shape[0]
    width = table.shape[1]
    workers = SC_CORES * SC_SUBCORES
    per_worker = n_rows // workers
    n_chunks = per_worker // SC_CHUNK
    mesh = plsc.VectorSubcoreMesh(core_axis_name="c", subcore_axis_name="s")

    def body(table_hbm, idx_hbm, after_hbm, out_hbm, *scratch):
        idx_v = scratch[:SC_NBUF]
        rows_v = scratch[SC_NBUF:2 * SC_NBUF]
        gsem = scratch[2 * SC_NBUF:3 * SC_NBUF]
        ssem = scratch[3 * SC_NBUF:]
        wid = lax.axis_index("s") * SC_CORES + lax.axis_index("c")
        base = wid * per_worker

        def chunk(j):
            return pl.ds(pl.multiple_of(base + j * SC_CHUNK, SC_CHUNK), SC_CHUNK)

        def gather(b):
            return pltpu.make_async_copy(table_hbm.at[idx_v[b]], rows_v[b], gsem[b])

        def start_gather(j, b):
            pltpu.sync_copy(idx_hbm.at[chunk(j)], idx_v[b])
            gather(b).start()

        def store(j, b):
            return pltpu.make_async_copy(rows_v[b], out_hbm.at[chunk(j)], ssem[b])

        for b in range(SC_NBUF):
            start_gather(b, b)

        @pl.loop(0, n_chunks, step=SC_NBUF)
        def _(g):
            for b in range(SC_NBUF):
                gather(b).wait()
                store(g + b, b).start()
            for b in range(SC_NBUF):
                nxt = g + SC_NBUF + b

                @pl.when(nxt < n_chunks)
                def _():
                    store(g + b, b).wait()
                    start_gather(nxt, b)

        for b in range(SC_NBUF):
            store(0, b).wait()

    assert n_chunks % SC_NBUF == 0 and per_worker % SC_CHUNK == 0
    return pl.kernel(
        body,
        out_type=jax.ShapeDtypeStruct((n_rows, width), table.dtype),
        mesh=mesh,
        scratch_types=([pltpu.VMEM((SC_CHUNK,), jnp.int32)] * SC_NBUF
                       + [pltpu.VMEM((SC_CHUNK, width), table.dtype)] * SC_NBUF
                       + [pltpu.SemaphoreType.DMA] * (2 * SC_NBUF)),
        name="sc_gather_rows",
    )(table, idx, after)


def _sc_scatter_rows(rows, idx, n_out):
    n_rows, width = rows.shape
    workers = SC_CORES * SC_SUBCORES
    per_worker = n_rows // workers
    n_chunks = per_worker // SC_CHUNK
    mesh = plsc.VectorSubcoreMesh(core_axis_name="c", subcore_axis_name="s")

    def body(rows_hbm, idx_hbm, out_hbm, *scratch):
        idx_v = scratch[:2 * SC_NBUF]
        rows_v = scratch[2 * SC_NBUF:3 * SC_NBUF]
        lsem = scratch[3 * SC_NBUF:4 * SC_NBUF]
        ssem = scratch[4 * SC_NBUF:]
        wid = lax.axis_index("s") * SC_CORES + lax.axis_index("c")
        base = wid * per_worker

        def chunk(j, k=0):
            return pl.ds(pl.multiple_of(k * n_rows + base + j * SC_CHUNK, SC_CHUNK), SC_CHUNK)

        def load(j, b):
            return pltpu.make_async_copy(rows_hbm.at[chunk(j)], rows_v[b], lsem[b])

        def scatter(b, k):
            return pltpu.make_async_copy(rows_v[b], out_hbm.at[idx_v[2 * b + k]], ssem[b])

        for b in range(SC_NBUF):
            load(b, b).start()

        @pl.loop(0, n_chunks, step=SC_NBUF)
        def _(g):
            for b in range(SC_NBUF):
                for k in range(2):
                    pltpu.sync_copy(idx_hbm.at[chunk(g + b, k)], idx_v[2 * b + k])
                load(g + b, b).wait()
                for k in range(2):
                    scatter(b, k).start()
            for b in range(SC_NBUF):
                nxt = g + SC_NBUF + b

                @pl.when(nxt < n_chunks)
                def _():
                    for k in range(2):
                        scatter(b, k).wait()
                    load(nxt, b).start()

        for b in range(SC_NBUF):
            for k in range(2):
                scatter(b, k).wait()

    assert n_chunks % SC_NBUF == 0 and per_worker % SC_CHUNK == 0
    return pl.kernel(
        body,
        out_type=jax.ShapeDtypeStruct((n_out, width), rows.dtype),
        mesh=mesh,
        scratch_types=([pltpu.VMEM((SC_CHUNK,), jnp.int32)] * (2 * SC_NBUF)
                       + [pltpu.VMEM((SC_CHUNK, width), rows.dtype)] * SC_NBUF
                       + [pltpu.SemaphoreType.DMA] * (2 * SC_NBUF)),
        name="sc_scatter_rows",
    )(rows, idx)


def _combine_kernel(x_ref, wts_ref, g1_ref, g2_ref, fg_ref, *rest, final):
    o_ref = rest[-1]
    w = wts_ref[...]
    out = (x_ref[...] + w[:, 0:1] * _unpack_rows(g1_ref[...])
           + w[:, 1:2] * _unpack_rows(g2_ref[...]))
    if final:
        out = _rms(out, fg_ref[...])
    o_ref[...] = out


def _moe_layer(meta, cnt, xp, layer, w_gate, w_up, w_down):
    n_tok = xp.shape[0]
    counts = cnt[0, :MOE_EXPERTS].astype(jnp.int32)
    padded = ((counts + MOE_BLK - 1) // MOE_BLK) * MOE_BLK
    pad_start = jnp.cumsum(padded) - padded
    n_pad = n_tok * 2 + MOE_EXPERTS * MOE_BLK + MOE_UNIT

    pos = _moe_positions(pad_start, meta)
    xd = _sc_scatter_rows(xp, pos.reshape(2 * n_tok), n_pad)
    y = _moe_experts(xd, pad_start, counts, layer, w_gate, w_up, w_down)

    per_part = n_tok // GATHER_PARTS
    gathered = []
    for h in range(GATHER_PARTS):
        idx = pos[:, h * per_part:(h + 1) * per_part].reshape(2 * per_part)
        gathered.append(_sc_gather_rows(y, idx, gathered[-1] if gathered else idx))
    return gathered


def _moe_combine(x, moe_out, final_g=None):
    d = x.shape[-1]
    final = final_g is not None
    fg = (final_g if final else jnp.ones((d,), F32)).reshape(1, d)
    return _per_part(functools.partial(_combine_kernel, final=final), x, moe_out, COMBINE_TM,
                     [pl.BlockSpec((1, d), lambda i: (0, 0))], [fg], "moe_combine")


def kernel(x, mem, s5_norm, s5_w_in, s5_lambda_re, s5_lambda_im, s5_log_dt, s5_b_re, s5_b_im, s5_c_re, s5_c_im, s5_d, s5_w_out, gm_norm, gm_w_in, gm_v_norm, gm_w_s, gm_b_s, gm_w_out, mem_norm, xa_norm, xa_w_q, xa_w_kv, xa_w_o, moe_norm, moe_w_group, moe_b_group, moe_w_expert, moe_b_expert, moe_w_gate, moe_w_up, moe_w_down, final_norm):
    moe_out = None
    for i in range(DEPTH):
        j = i // 2
        if i % 2 == 0:
            if moe_out is not None:
                x = _moe_combine(x, moe_out)
            x = _s5_layer(x, s5_norm[j], s5_w_in[j], s5_lambda_re[j], s5_lambda_im[j],
                          s5_log_dt[j], s5_b_re[j], s5_b_im[j], s5_c_re[j], s5_c_im[j],
                          s5_d[j], s5_w_out[j])
        else:
            x = _gmlp_layer(x, moe_out, gm_norm[j], gm_w_in[j], gm_v_norm[j], gm_w_s[j],
                            gm_b_s[j], gm_w_out[j])
        x, meta, wts, cnt, xp = _xattn_router_layer(
            x, mem, mem_norm, xa_norm[i], xa_w_q[i], xa_w_kv[i], xa_w_o[i], moe_norm[i],
            moe_w_group[i], moe_b_group[i], moe_w_expert[i], moe_b_expert[i])
        moe_out = (wts, _moe_layer(meta, cnt, xp, i, moe_w_gate, moe_w_up, moe_w_down))
    return _moe_combine(x, moe_out, final_g=final_norm)
```
